```python
import math
import jax, jax.numpy as jnp
from jax import lax
import numpy as np

D_MODEL = 1024
BATCH = 16
SEQ = 256
DEPTH = 2
DEC_BATCH = 8
DEC_SEQ = 2048
PAST_LEN = 512

GRID_W = 64
EPS = 1e-6
ROPE_BASE = 10000.0
N_EVEN = (DEPTH + 1) // 2
N_ODD = DEPTH // 2
DA_HEADS = 4
DA_DIM = 64
DA_WIDTH = DA_HEADS * 2 * DA_DIM
Q_BLOCK = 128
GLA_HEADS = 4
GLA_DK = 64
GLA_DV = 128
GLA_WIDTH = GLA_HEADS * GLA_DV
GLA_GATE_RANK = 16
GLA_TAU = 16.0
GLA_CHUNK = 64
MIX_WIDTH = DA_WIDTH + GLA_WIDTH
IN0_WIDTHS = (2 * DA_HEADS * DA_DIM, 2 * DA_HEADS * DA_DIM, DA_HEADS * 2 * DA_DIM,
              GLA_HEADS * GLA_DK, GLA_HEADS * GLA_DK, GLA_HEADS * GLA_DV, GLA_HEADS * GLA_DV,
              2 * GLA_GATE_RANK)
IN0_WIDTH = sum(IN0_WIDTHS)
S5_GROUP_CH = 16
S5_GROUPS = D_MODEL // S5_GROUP_CH
S5_P = 64
PEER_HEADS = 8
PEER_NKEYS = 128
PEER_N = PEER_NKEYS * PEER_NKEYS
PEER_QDIM = 256
PEER_TOPK = 16
PEER_TOKEN_BLOCK = 128

kernel_name = "hybrid_diffusion_prefix_step"

F32 = jnp.float32


def rms_norm(x, g):
    xf = x.astype(F32)
    y = xf * lax.rsqrt(jnp.mean(xf * xf, axis=-1, keepdims=True) + EPS)
    return (y * g.astype(F32)).astype(x.dtype)


def split_cols(x, widths):
    out, start = [], 0
    for w in widths:
        out.append(x[..., start:start + w])
        start += w
    return out


def lambda_init(layer):
    return 0.8 - 0.6 * math.exp(-0.3 * layer)


def axial_rope(length, dim):
    rows = length // GRID_W
    row_id = jnp.repeat(jnp.arange(rows), GRID_W).astype(F32)
    col_id = jnp.tile(jnp.arange(GRID_W), rows).astype(F32)
    quarter = dim // 4
    inv = ROPE_BASE ** (-jnp.arange(quarter, dtype=F32) / quarter)
    ang_r = row_id[:, None] * inv
    ang_c = col_id[:, None] * inv
    ang = jnp.concatenate([ang_r, ang_r, ang_c, ang_c], axis=-1)
    return jnp.cos(ang), jnp.sin(ang)


def apply_rope(x, cos, sin):
    xf = x.astype(F32)
    xr = xf.reshape(xf.shape[:-1] + (2, 2, xf.shape[-1] // 4))
    rot = jnp.concatenate([-xr[..., 1:, :], xr[..., :1, :]], axis=-2).reshape(xf.shape)
    return (xf * cos[:, None] + rot * sin[:, None]).astype(x.dtype)


def diff_attention(q, k, v, lam, lam_init, subln_g):
    B, Lq = q.shape[:2]
    Lk = k.shape[1]
    qb = q.reshape(B, Lq // Q_BLOCK, Q_BLOCK, DA_HEADS, 2, DA_DIM)
    qb = jnp.moveaxis(qb, 1, 0)
    kk = k.reshape(B, Lk, DA_HEADS, 2, DA_DIM)
    scale = DA_DIM ** -0.5

    def block(qblk):
        s = jnp.einsum('bqhmd,bkhmd->bhmqk', qblk, kk).astype(F32) * scale
        p = jax.nn.softmax(s, axis=-1)
        a = p[:, :, 0] - lam * p[:, :, 1]
        return jnp.einsum('bhqk,bkhe->bqhe', a.astype(v.dtype), v)

    o = lax.map(block, qb)
    o = jnp.moveaxis(o, 0, 1).reshape(B, Lq, DA_HEADS, 2 * DA_DIM)
    o = rms_norm(o, subln_g) * (1.0 - lam_init)
    return o.reshape(B, Lq, DA_WIDTH)


def gla_chunked(q, k, v, log_a, s0):
    B, L = q.shape[:2]
    n = L // GLA_CHUNK

    def to_chunks(t):
        return jnp.moveaxis(t.reshape((B, n, GLA_CHUNK) + t.shape[2:]), 1, 0)

    qc, kc, vc, gc = to_chunks(q), to_chunks(k), to_chunks(v), to_chunks(log_a)
    mask = jnp.tril(jnp.ones((GLA_CHUNK, GLA_CHUNK), dtype=bool))

    def step(s, inp):
        qi, ki, vi, gi = inp
        b = jnp.cumsum(gi, axis=1)
        kf = ki.astype(F32)
        vf = vi.astype(F32)
        q_dec = qi.astype(F32) * jnp.exp(b)
        k_inv = kf * jnp.exp(-b)
        inter = jnp.einsum('bchk,bhkv->bchv', q_dec, s)
        att = jnp.where(mask, jnp.einsum('bchk,bshk->bhcs', q_dec, k_inv), 0.0)
        intra = jnp.einsum('bhcs,bshv->bchv', att, vf)
        b_last = b[:, -1]
        k_end = kf * jnp.exp(b_last[:, None] - b)
        s_new = jnp.exp(b_last)[..., None] * s + jnp.einsum('bchk,bchv->bhkv', k_end, vf)
        return s_new, inter + intra

    s_final, o = lax.scan(step, s0, (qc, kc, vc, gc))
    o = jnp.moveaxis(o, 0, 1).reshape(B, L, GLA_HEADS, GLA_DV)
    return o, s_final


def even_mixer(h, w_in, w_out, da_lam, lam_init, subln_g, gate_up, gate_bias, gla_norm_g, rope, ctx):
    B, L, _ = h.shape
    q_da, k_da, v_da, q_g, k_g, v_g, r_g, g_low = split_cols(h @ w_in, IN0_WIDTHS)
    q_da = q_da.reshape(B, L, 2 * DA_HEADS, DA_DIM)
    k_da = k_da.reshape(B, L, 2 * DA_HEADS, DA_DIM)
    v_da = v_da.reshape(B, L, DA_HEADS, 2 * DA_DIM)
    if rope is not None:
        q_da = apply_rope(q_da, *rope)
        k_da = apply_rope(k_da, *rope)
    lf = da_lam.astype(F32)
    lam = jnp.exp(jnp.sum(lf[0] * lf[1])) - jnp.exp(jnp.sum(lf[2] * lf[3])) + lam_init
    if ctx is None:
        k_keys, v_keys = k_da, v_da
        s0 = jnp.zeros((2, B, GLA_HEADS, GLA_DK, GLA_DV), F32)
    else:
        k_ctx, v_ctx, s_ctx = ctx
        k_keys = jnp.concatenate([k_da, k_ctx.astype(k_da.dtype)], axis=1)
        v_keys = jnp.concatenate([v_da, v_ctx.astype(v_da.dtype)], axis=1)
        s0 = jnp.moveaxis(s_ctx.astype(F32), 1, 0)
    o_da = diff_attention(q_da, k_keys, v_keys, lam, lam_init, subln_g)

    q_g = q_g.reshape(B, L, GLA_HEADS, GLA_DK) * (GLA_DK ** -0.5)
    k_g = k_g.reshape(B, L, GLA_HEADS, GLA_DK)
    v_g = v_g.reshape(B, L, GLA_HEADS, GLA_DV)
    low = g_low.reshape(B, L, 2, GLA_GATE_RANK)
    z = (jnp.einsum('bldr,drk->bldk', low, gate_up) + gate_bias).astype(F32)
    log_a = (jax.nn.log_sigmoid(z) / GLA_TAU).reshape(B, L, 2, GLA_HEADS, GLA_DK)
    o_f, s_f = gla_chunked(q_g, k_g, v_g, log_a[:, :, 0], s0[0])
    flip = lambda t: jnp.flip(t, axis=1)
    o_b, s_b = gla_chunked(flip(q_g), flip(k_g), flip(v_g), flip(log_a[:, :, 1]), s0[1])
    o_g = rms_norm(o_f + flip(o_b), gla_norm_g).astype(h.dtype)
    o_g = o_g * jax.nn.silu(r_g.reshape(B, L, GLA_HEADS, GLA_DV))
    out = jnp.concatenate([o_da, o_g.reshape(B, L, GLA_WIDTH)], axis=-1) @ w_out
    return out, (k_da, v_da, jnp.stack([s_f, s_b], axis=1))


def s5_scan(u, lam_bar, b_bar, h0):
    bu = jnp.einsum('gpc,blgc->blgp', b_bar, u.astype(jnp.complex64))
    bu = bu.at[:, 0].add(lam_bar * h0)
    a = jnp.broadcast_to(lam_bar, bu.shape)

    def combine(x, y):
        a1, b1 = x
        a2, b2 = y
        return a1 * a2, a2 * b1 + b2

    _, states = lax.associative_scan(combine, (a, bu), axis=1)
    return states


def s5_mixer(h, w_in, a_re, a_im, log_dt, b_re, b_im, c_re, c_im, d_skip, w_out, h0):
    B, L, _ = h.shape
    u = (h @ w_in).astype(F32)
    ug = u.reshape(B, L, S5_GROUPS, S5_GROUP_CH)
    if h0 is None:
        h0c = jnp.zeros((B, 2, S5_GROUPS, S5_P), jnp.complex64)
    else:
        hf = h0.astype(F32)
        h0c = lax.complex(hf[..., 0], hf[..., 1])
    y = u * d_skip.astype(F32)
    finals = []
    for dirn in range(2):
        lam = lax.complex(a_re[dirn].astype(F32), a_im[dirn].astype(F32))
        dt = jnp.exp(log_dt[dirn].astype(F32))[:, None]
        lam_bar = jnp.exp(lam * dt)
        b_mat = lax.complex(b_re[dirn].astype(F32), b_im[dirn].astype(F32))
        b_bar = ((lam_bar - 1.0) / lam)[..., None] * b_mat
        c_mat = lax.complex(c_re[dirn].astype(F32), c_im[dirn].astype(F32))
        u_dir = ug if dirn == 0 else jnp.flip(ug, axis=1)
        states = s5_scan(u_dir, lam_bar, b_bar, h0c[:, dirn])
        y_dir = jnp.real(jnp.einsum('gcp,blgp->blgc', c_mat, states))
        if dirn == 1:
            y_dir = jnp.flip(y_dir, axis=1)
        y = y + y_dir.reshape(B, L, D_MODEL)
        finals.append(states[:, -1])
    zz = jax.nn.gelu(y).astype(h.dtype) @ w_out
    za, zb = jnp.split(zz, 2, axis=-1)
    fin = jnp.stack(finals, axis=1)
    return za * jax.nn.sigmoid(zb), jnp.stack([jnp.real(fin), jnp.imag(fin)], axis=-1)


def peer_ffn(h, w_query, sub_keys, u_tab, v_tab):
    B, L, D = h.shape
    q = (h @ w_query).reshape(B, L, PEER_HEADS, 2, PEER_QDIM // 2)
    s = jnp.einsum('blhtd,htnd->blhtn', q, sub_keys).astype(F32)
    s1, i1 = lax.top_k(s[..., 0, :], PEER_TOPK)
    s2, i2 = lax.top_k(s[..., 1, :], PEER_TOPK)
    cand = (s1[..., :, None] + s2[..., None, :]).reshape(B, L, PEER_HEADS, PEER_TOPK * PEER_TOPK)
    cidx = (i1[..., :, None] * PEER_NKEYS + i2[..., None, :]).reshape(B, L, PEER_HEADS, PEER_TOPK * PEER_TOPK)
    top, pos = lax.top_k(cand, PEER_TOPK)
    idx = jnp.take_along_axis(cidx, pos, axis=-1)
    g = jax.nn.softmax(top, axis=-1).astype(h.dtype)
    nb = (B * L) // PEER_TOKEN_BLOCK
    hb = h.reshape(nb, PEER_TOKEN_BLOCK, D)
    ib = idx.reshape(nb, PEER_TOKEN_BLOCK, PEER_HEADS * PEER_TOPK)
    gb = g.reshape(nb, PEER_TOKEN_BLOCK, PEER_HEADS * PEER_TOPK)

    def block(args):
        hx, ix, gx = args
        act = jax.nn.gelu(jnp.einsum('td,tkd->tk', hx, u_tab[ix]))
        return jnp.einsum('tk,tkd->td', gx * act, v_tab[ix])

    return lax.map(block, (hb, ib, gb)).reshape(B, L, D)


def setup_inputs(seed: int = 0) -> dict:
    key = jax.random.key(seed)
    ks = iter(jax.random.split(key, 48))

    def nrm(shape, scale):
        return scale * jax.random.normal(next(ks), shape, F32)

    D = D_MODEL
    n_idx = jnp.arange(S5_P, dtype=F32)
    return {
        "x_prompt": nrm((BATCH, SEQ, D), 1.0),
        "x_sample": nrm((DEC_BATCH, DEC_SEQ, D), 1.0),
        "cache_da_k": nrm((DEC_BATCH, N_EVEN, PAST_LEN, 2 * DA_HEADS, DA_DIM), 1.0),
        "cache_da_v": nrm((DEC_BATCH, N_EVEN, PAST_LEN, DA_HEADS, 2 * DA_DIM), 1.0),
        "state_gla": nrm((DEC_BATCH, N_EVEN, 2, GLA_HEADS, GLA_DK, GLA_DV), 1.0),
        "state_s5": nrm((DEC_BATCH, N_ODD, 2, S5_GROUPS, S5_P, 2), 0.3),
        "c": nrm((DEC_BATCH, D), 1.0),
        "c_ctx": nrm((D,), 1.0),
        "ada_w": nrm((DEPTH, D, 6 * D), 0.5 * D ** -0.5),
        "ada_b": nrm((DEPTH, 6 * D), 0.02),
        "norm_g": 1.0 + nrm((DEPTH, 2, D), 0.05),
        "final_norm_g": 1.0 + nrm((D,), 0.05),
        "mix0_w_in": nrm((N_EVEN, D, IN0_WIDTH), D ** -0.5),
        "mix0_w_out": nrm((N_EVEN, MIX_WIDTH, D), MIX_WIDTH ** -0.5),
        "da_lam": nrm((N_EVEN, 4, DA_DIM), 0.1),
        "da_subln_g": 1.0 + nrm((N_EVEN, 2 * DA_DIM), 0.05),
        "gla_gate_up": nrm((N_EVEN, 2, GLA_GATE_RANK, GLA_HEADS * GLA_DK), GLA_GATE_RANK ** -0.5),
        "gla_gate_bias": nrm((N_EVEN, 2, GLA_HEADS * GLA_DK), 0.1),
        "gla_norm_g": 1.0 + nrm((N_EVEN, GLA_DV), 0.05),
        "s5_w_in": nrm((N_ODD, D, D), D ** -0.5),
        "s5_a_re": -0.5 + nrm((N_ODD, 2, S5_GROUPS, S5_P), 0.01),
        "s5_a_im": jnp.pi * n_idx + nrm((N_ODD, 2, S5_GROUPS, S5_P), 0.01),
        "s5_log_dt": jax.random.uniform(next(ks), (N_ODD, 2, S5_GROUPS), F32,
                                        minval=math.log(1e-3), maxval=math.log(1e-1)),
        "s5_b_re": nrm((N_ODD, 2, S5_GROUPS, S5_P, S5_GROUP_CH), (2 * S5_GROUP_CH) ** -0.5),
        "s5_b_im": nrm((N_ODD, 2, S5_GROUPS, S5_P, S5_GROUP_CH), (2 * S5_GROUP_CH) ** -0.5),
        "s5_c_re": nrm((N_ODD, 2, S5_GROUPS, S5_GROUP_CH, S5_P), (2 * S5_P) ** -0.5),
        "s5_c_im": nrm((N_ODD, 2, S5_GROUPS, S5_GROUP_CH, S5_P), (2 * S5_P) ** -0.5),
        "s5_d": nrm((N_ODD, D), 0.5),
        "s5_w_out": nrm((N_ODD, D, 2 * D), D ** -0.5),
        "peer_w_query": nrm((DEPTH, D, PEER_HEADS * PEER_QDIM), D ** -0.5),
        "peer_sub_keys": nrm((DEPTH, PEER_HEADS, 2, PEER_NKEYS, PEER_QDIM // 2), (PEER_QDIM // 2) ** -0.5),
        "peer_u": nrm((DEPTH, PEER_N, D), D ** -0.5),
        "peer_v": nrm((DEPTH, PEER_N, D), PEER_HEADS ** -0.5),
    }


def reference(x_prompt, x_sample, cache_da_k, cache_da_v, state_gla, state_s5, c, c_ctx,
              ada_w, ada_b, norm_g, final_norm_g, mix0_w_in, mix0_w_out, da_lam, da_subln_g,
              gla_gate_up, gla_gate_bias, gla_norm_g, s5_w_in, s5_a_re, s5_a_im, s5_log_dt,
              s5_b_re, s5_b_im, s5_c_re, s5_c_im, s5_d, s5_w_out,
              peer_w_query, peer_sub_keys, peer_u, peer_v):

    def trunk(x, cond, rope, caches):
        new_k, new_v, new_gla, new_s5 = [], [], [], []
        for layer in range(DEPTH):
            mod = jax.nn.silu(cond) @ ada_w[layer] + ada_b[layer]
            sh1, sc1, g1, sh2, sc2, g2 = jnp.split(mod[..., None, :], 6, axis=-1)
            h = rms_norm(x, norm_g[layer, 0]) * (1.0 + sc1) + sh1
            j = layer // 2
            if layer % 2 == 0:
                ctx = None if caches is None else (caches[0][:, j], caches[1][:, j], caches[2][:, j])
                m, (k_c, v_c, s_c) = even_mixer(h, mix0_w_in[j], mix0_w_out[j], da_lam[j],
                                                lambda_init(layer), da_subln_g[j], gla_gate_up[j],
                                                gla_gate_bias[j], gla_norm_g[j], rope, ctx)
                if caches is None:
                    new_k.append(k_c)
                    new_v.append(v_c)
                    new_gla.append(s_c)
            else:
                h0 = None if caches is None else caches[3][:, j]
                m, s_c = s5_mixer(h, s5_w_in[j], s5_a_re[j], s5_a_im[j], s5_log_dt[j], s5_b_re[j],
                                  s5_b_im[j], s5_c_re[j], s5_c_im[j], s5_d[j], s5_w_out[j], h0)
                if caches is None:
                    new_s5.append(s_c)
            x = x + g1 * m
            h = rms_norm(x, norm_g[layer, 1]) * (1.0 + sc2) + sh2
            x = x + g2 * peer_ffn(h, peer_w_query[layer], peer_sub_keys[layer], peer_u[layer], peer_v[layer])
        return rms_norm(x, final_norm_g), new_k, new_v, new_gla, new_s5

    y_prompt, k_list, v_list, gla_list, s5_list = trunk(x_prompt, c_ctx, None, None)
    rope = axial_rope(x_sample.shape[1], DA_DIM)
    y_sample = trunk(x_sample, c, rope, (cache_da_k, cache_da_v, state_gla, state_s5))[0]

    dt = x_prompt.dtype
    new_cache_da_k = jnp.stack(k_list, axis=1).astype(dt)
    new_cache_da_v = jnp.stack(v_list, axis=1).astype(dt)
    new_state_gla = jnp.stack(gla_list, axis=1).astype(dt)
    new_state_s5 = jnp.stack(s5_list, axis=1).astype(dt)
    return (y_prompt, y_sample, new_cache_da_k, new_cache_da_v, new_state_gla, new_state_s5)
```

```python
import functools
import math

import jax
import jax.numpy as jnp
from jax import lax
from jax.experimental import pallas as pl
from jax.experimental.pallas import tpu as pltpu

F32 = jnp.float32
BF16 = jnp.bfloat16
HI = lax.Precision.HIGHEST

D = 1024
EPS = 1e-6
ROPE_BASE = 10000.0
GRID_W = 64
DA_HEADS = 4
DA_DIM = 64
GLA_HEADS = 4
GLA_DK = 64
GLA_DV = 128
GLA_RANK = 16
GLA_TAU = 16.0
GLA_CHUNK = 64
S5_CH = 16
S5_GROUPS = D // S5_CH
S5_P = 64
S5_CHUNK = 16
PEER_HEADS = 8
PEER_NKEYS = 128
PEER_TOPK = 16
IN0_PAD = 3200

ROW_TILE = 256
PEER_TOK = 512
PEER_EXP = 1024
VMEM_LIMIT = 48 * 1024 * 1024

NT = (((1,), (1,)), ((), ()))
TN = (((0,), (0,)), ((), ()))


def _cp(*sem):
    return pltpu.CompilerParams(dimension_semantics=sem, vmem_limit_bytes=VMEM_LIMIT)


def _norm_mod(x, g, sc, sh):
    ms = jnp.mean(x * x, axis=-1, keepdims=True)
    return x * lax.rsqrt(ms + EPS) * g * (1.0 + sc) + sh


def _gelu(x):
    return 0.5 * x * (1.0 + jnp.tanh(0.7978845608028654 * (x + 0.044715 * (x * x * x))))


def _mod_row_map(np_blocks, seq_blocks):
    def f(i):
        return jnp.where(i < np_blocks, 0, 1 + (i - np_blocks) // seq_blocks)
    return f


def _ada_body(c_ref, w_ref, b_ref, o_ref):
    c = c_ref[...]
    s = c / (1.0 + jnp.exp(-c))
    o_ref[0] = jnp.dot(s, w_ref[0], precision=HI, preferred_element_type=F32) + b_ref[0]


def _ada_mod(cond, ada_w, ada_b):
    depth, _, n = ada_w.shape
    rows = cond.shape[0]
    tn = 1536
    return pl.pallas_call(
        _ada_body,
        grid=(depth, n // tn),
        in_specs=[pl.BlockSpec((rows, D), lambda l, j: (0, 0)),
                  pl.BlockSpec((1, D, tn), lambda l, j: (l, 0, j)),
                  pl.BlockSpec((1, 1, tn), lambda l, j: (l, 0, j))],
        out_specs=pl.BlockSpec((1, rows, tn), lambda l, j: (l, 0, j)),
        out_shape=jax.ShapeDtypeStruct((depth, rows, n), F32),
        compiler_params=_cp("arbitrary", "arbitrary"),
        name="ada_mod",
    )(cond, ada_w, ada_b.reshape(depth, 1, n))


def _inproj0_body(x_ref, mod_ref, g_ref, w_ref, cos_ref, sin_ref,
                  q_o, kb_o, vb_o, k32_o, v32_o, gqk_o, gv_o, gr_o, low_o):
    mod = mod_ref[0]
    h = _norm_mod(x_ref[...], g_ref[...], mod[:, D:2 * D], mod[:, 0:D]).astype(BF16)
    y = jnp.dot(h, w_ref[...], preferred_element_type=F32)
    cos = cos_ref[...]
    sin = sin_ref[...]
    lane = lax.broadcasted_iota(jnp.int32, cos.shape, 1)
    first = (lane & 31) < 16

    def rope(t):
        rot = jnp.where(first, -pltpu.roll(t, 512 - 16, 1), pltpu.roll(t, 16, 1))
        return t * cos + rot * sin

    q = rope(y[:, 0:512])
    k = rope(y[:, 512:1024])
    v = y[:, 1024:1536]
    q_o[...] = q.astype(BF16)
    kb_o[...] = k.astype(BF16)
    vb_o[...] = v.astype(BF16)
    k32_o[...] = k
    v32_o[...] = v
    gqk_o[...] = y[:, 1536:2048].astype(BF16)
    gv_o[...] = y[:, 2048:2560].astype(BF16)
    gr_o[...] = y[:, 2560:3072].astype(BF16)
    low_o[...] = y[:, 3072:3200]


def _inproj0(x, mod, norm_g, w, cos_t, sin_t, np_rows, seq_rows):
    n = x.shape[0]
    tm = ROW_TILE
    npb, sqb = np_rows // tm, seq_rows // tm
    rowmap = _mod_row_map(npb, sqb)
    posmap = lambda i: (jnp.where(i < npb, 0, 1 + (i - npb) % sqb), 0)
    tok = lambda w_: pl.BlockSpec((tm, w_), lambda i: (i, 0))
    outs = [(512, BF16), (512, BF16), (512, BF16), (512, F32), (512, F32),
            (512, BF16), (512, BF16), (512, BF16), (128, F32)]
    return pl.pallas_call(
        _inproj0_body,
        grid=(n // tm,),
        in_specs=[tok(D),
                  pl.BlockSpec((1, 1, 6 * D), lambda i: (rowmap(i), 0, 0)),
                  pl.BlockSpec((1, D), lambda i: (0, 0)),
                  pl.BlockSpec((D, IN0_PAD), lambda i: (0, 0)),
                  pl.BlockSpec((tm, 512), posmap),
                  pl.BlockSpec((tm, 512), posmap)],
        out_specs=[tok(w_) for w_, _ in outs],
        out_shape=[jax.ShapeDtypeStruct((n, w_), dt) for w_, dt in outs],
        compiler_params=_cp("arbitrary"),
        name="inproj0",
    )(x, mod, norm_g, w, cos_t, sin_t)


def _attn_body(lam_ref, sg_ref, q_ref, k_ref, v_ref, *rest, lam_init, has_cache):
    if has_cache:
        kc_ref, vc_ref, o_ref = rest
    else:
        (o_ref,) = rest
    lf = lam_ref[...]
    lam = (jnp.exp(jnp.sum(lf[0:1] * lf[1:2], axis=-1, keepdims=True))
           - jnp.exp(jnp.sum(lf[2:3] * lf[3:4], axis=-1, keepdims=True)) + lam_init)
    q = q_ref[...] * (DA_DIM ** -0.5)
    lane = lax.broadcasted_iota(jnp.int32, q.shape, 1)
    zero = jnp.zeros_like(q)
    qm = (jnp.where(lane < DA_DIM, q, zero), jnp.where(lane >= DA_DIM, q, zero))
    k = k_ref[...]
    s = [lax.dot_general(qm[m], k, NT, preferred_element_type=F32) for m in range(2)]
    mx = [jnp.max(s[m], axis=-1, keepdims=True) for m in range(2)]
    if has_cache:
        kc = kc_ref[...].astype(BF16)
        sc = [lax.dot_general(qm[m], kc, NT, preferred_element_type=F32) for m in range(2)]
        mx = [jnp.maximum(mx[m], jnp.max(sc[m], axis=-1, keepdims=True)) for m in range(2)]
    e = [jnp.exp(s[m] - mx[m]) for m in range(2)]
    z = [jnp.sum(e[m], axis=-1, keepdims=True) for m in range(2)]
    if has_cache:
        ec = [jnp.exp(sc[m] - mx[m]) for m in range(2)]
        z = [z[m] + jnp.sum(ec[m], axis=-1, keepdims=True) for m in range(2)]
    w0 = 1.0 / z[0]
    w1 = lam / z[1]
    o = jnp.dot((e[0] * w0 - e[1] * w1).astype(BF16), v_ref[...], preferred_element_type=F32)
    if has_cache:
        o = o + jnp.dot((ec[0] * w0 - ec[1] * w1).astype(BF16), vc_ref[...].astype(BF16),
                        preferred_element_type=F32)
    ms = jnp.mean(o * o, axis=-1, keepdims=True)
    o_ref[...] = (o * lax.rsqrt(ms + EPS) * sg_ref[...] * (1.0 - lam_init)).astype(o_ref.dtype)


def _diff_attention(q, k, v, da_lam, subln_g, lam_init, row0, batch, seq, cache=None):
    tq = 256
    qb0, kb0 = row0 // tq, row0 // seq
    in_specs = [pl.BlockSpec((4, DA_DIM), lambda b, h, i: (0, 0)),
                pl.BlockSpec((1, 128), lambda b, h, i: (0, 0)),
                pl.BlockSpec((tq, 128), lambda b, h, i: (qb0 + b * (seq // tq) + i, h)),
                pl.BlockSpec((seq, 128), lambda b, h, i: (kb0 + b, h)),
                pl.BlockSpec((seq, 128), lambda b, h, i: (kb0 + b, h))]
    args = [da_lam, subln_g, q, k, v]
    if cache is not None:
        past = cache[0].shape[1]
        in_specs += [pl.BlockSpec((None, past, 128), lambda b, h, i: (b, 0, h))] * 2
        args += list(cache)
    return pl.pallas_call(
        functools.partial(_attn_body, lam_init=lam_init, has_cache=cache is not None),
        grid=(batch, DA_HEADS, seq // tq),
        in_specs=in_specs,
        out_specs=pl.BlockSpec((tq, 128), lambda b, h, i: (b * (seq // tq) + i, h)),
        out_shape=jax.ShapeDtypeStruct((batch * seq, 512), BF16),
        compiler_params=_cp("arbitrary", "arbitrary", "arbitrary"),
        name="diff_attn_ctx" if cache is not None else "diff_attn",
    )(*args)


def _gla_body(qk_ref, v_ref, r_ref, low_ref, gup_ref, gb_ref, ng_ref, *rest, seq, has_init):
    if has_init:
        s0_ref, o_ref, sf_ref, st_scr, of_scr, ob_scr, la_scr = rest
    else:
        o_ref, sf_ref, st_scr, of_scr, ob_scr, la_scr = rest
    c = GLA_CHUNK
    n = seq // c
    hk = GLA_HEADS * GLA_DK
    low = low_ref[...]
    for d in range(2):
        z = jnp.dot(low, gup_ref[d], precision=HI, preferred_element_type=F32) + gb_ref[d]
        la_scr[d] = (jnp.minimum(z, 0.0) - jnp.log(1.0 + jnp.exp(-jnp.abs(z)))) * (1.0 / GLA_TAU)
    if has_init:
        st_scr[...] = s0_ref[0]
    else:
        st_scr[...] = jnp.zeros(st_scr.shape, F32)
    row = lax.broadcasted_iota(jnp.int32, (c, c), 0)
    col = lax.broadcasted_iota(jnp.int32, (c, c), 1)
    keep = (col <= row, col >= row)
    lane = lax.broadcasted_iota(jnp.int32, (c, 128), 1)
    low_half = lane < GLA_DK

    def chunk(d, r0):
        rows = pl.ds(r0, c)
        qk = qk_ref[rows, :].astype(F32)
        q = qk[:, 0:hk] * (GLA_DK ** -0.5)
        k = qk[:, hk:2 * hk]
        v = v_ref[rows, :]
        g = la_scr[d, rows, :]
        b = jnp.dot(keep[d].astype(F32), g, precision=HI, preferred_element_type=F32)
        b_tot = b[c - 1:c] if d == 0 else b[0:1]
        q_dec = q * jnp.exp(b)
        k_inv = k * jnp.exp(-b)
        k_end = k * jnp.exp(b_tot - b)
        st = st_scr[d]
        st_b = st.astype(BF16)
        outs, news = [], []
        for pair in range(GLA_HEADS // 2):
            ps = slice(pair * 128, (pair + 1) * 128)
            kin = k_inv[:, ps].astype(BF16)
            new = None
            for sub in range(2):
                hh = pair * 2 + sub
                vs = slice(hh * GLA_DV, (hh + 1) * GLA_DV)
                sel = low_half if sub == 0 else jnp.logical_not(low_half)
                qd = jnp.where(sel, q_dec[:, ps], 0.0).astype(BF16)
                ke = jnp.where(sel, k_end[:, ps], 0.0).astype(BF16)
                inter = lax.dot_general(qd, st_b[:, ps], NT, preferred_element_type=F32)
                att = lax.dot_general(qd, kin, NT, preferred_element_type=F32)
                att = jnp.where(keep[d], att, 0.0).astype(BF16)
                outs.append(inter + jnp.dot(att, v[:, vs], preferred_element_type=F32))
                upd = lax.dot_general(v[:, vs], ke, TN, preferred_element_type=F32)
                new = upd if new is None else new + upd
            news.append(new)
        st_scr[d] = st * jnp.exp(b_tot) + jnp.concatenate(news, axis=1)
        return jnp.concatenate(outs, axis=1)

    def step(i, carry):
        rf = pl.multiple_of(i * c, c)
        of_scr[pl.ds(rf, c), :] = chunk(0, rf)
        rb = pl.multiple_of((n - 1 - i) * c, c)
        ob_scr[pl.ds(rb, c), :] = chunk(1, rb)
        return carry

    lax.fori_loop(0, n, step, 0)
    sf_ref[0] = st_scr[...]

    fin = 256

    def finish(i, carry):
        rows = pl.ds(pl.multiple_of(i * fin, fin), fin)
        o = of_scr[rows, :] + ob_scr[rows, :]
        r = r_ref[rows, :].astype(F32)
        gate = r / (1.0 + jnp.exp(-r))
        ng = ng_ref[...]
        for hh in range(GLA_HEADS):
            vs = slice(hh * GLA_DV, (hh + 1) * GLA_DV)
            oh = o[:, vs]
            ms = jnp.mean(oh * oh, axis=-1, keepdims=True)
            o_ref[rows, vs] = (oh * lax.rsqrt(ms + EPS) * ng * gate[:, vs]).astype(BF16)
        return carry

    lax.fori_loop(0, seq // fin, finish, 0)


def _gla(gqk, gv, gr, low, gup, gb, ng, row0, batch, seq, s0=None):
    b0 = row0 // seq
    tokmap = lambda b: (b0 + b, 0)
    hk = GLA_HEADS * GLA_DK
    in_specs = [pl.BlockSpec((seq, 512), tokmap), pl.BlockSpec((seq, 512), tokmap),
                pl.BlockSpec((seq, 512), tokmap), pl.BlockSpec((seq, 128), tokmap),
                pl.BlockSpec((2, 128, hk), lambda b: (0, 0, 0)),
                pl.BlockSpec((2, 1, hk), lambda b: (0, 0, 0)),
                pl.BlockSpec((1, GLA_DV), lambda b: (0, 0))]
    args = [gqk, gv, gr, low, gup, gb, ng]
    if s0 is not None:
        in_specs.append(pl.BlockSpec((1, 2, GLA_DV, hk), lambda b: (b, 0, 0, 0)))
        args.append(s0)
    return pl.pallas_call(
        functools.partial(_gla_body, seq=seq, has_init=s0 is not None),
        grid=(batch,),
        in_specs=in_specs,
        out_specs=[pl.BlockSpec((seq, 512), lambda b: (b, 0)),
                   pl.BlockSpec((1, 2, GLA_DV, hk), lambda b: (b, 0, 0, 0))],
        out_shape=[jax.ShapeDtypeStruct((batch * seq, 512), BF16),
                   jax.ShapeDtypeStruct((batch, 2, GLA_DV, hk), F32)],
        scratch_shapes=[pltpu.VMEM((2, GLA_DV, hk), F32),
                        pltpu.VMEM((seq, 512), F32), pltpu.VMEM((seq, 512), F32),
                        pltpu.VMEM((2, seq, hk), F32)],
        compiler_params=_cp("arbitrary"),
        name="gla_ctx" if s0 is not None else "gla",
    )(*args)


def _outproj0_body(x_ref, mod_ref, oda_ref, og_ref, w_ref, o_ref):
    m = (jnp.dot(oda_ref[...], w_ref[0:512, :], preferred_element_type=F32)
         + jnp.dot(og_ref[...], w_ref[512:1024, :], preferred_element_type=F32))
    o_ref[...] = x_ref[...] + mod_ref[0][:, 2 * D:3 * D] * m


def _outproj0(x, mod, oda, og, w, np_rows, seq_rows):
    n = x.shape[0]
    tm = ROW_TILE
    rowmap = _mod_row_map(np_rows // tm, seq_rows // tm)
    return pl.pallas_call(
        _outproj0_body,
        grid=(n // tm,),
        in_specs=[pl.BlockSpec((tm, D), lambda i: (i, 0)),
                  pl.BlockSpec((1, 1, 6 * D), lambda i: (rowmap(i), 0, 0)),
                  pl.BlockSpec((tm, 512), lambda i: (i, 0)),
                  pl.BlockSpec((tm, 512), lambda i: (i, 0)),
                  pl.BlockSpec((D, D), lambda i: (0, 0))],
        out_specs=pl.BlockSpec((tm, D), lambda i: (i, 0)),
        out_shape=jax.ShapeDtypeStruct((n, D), F32),
        compiler_params=_cp("arbitrary"),
        name="outproj0",
    )(x, mod, oda, og, w)


def _route_body(x_ref, mod_ref, g_ref, wq_ref, sk_ref, ht_o, thr_o, e1_o, s2_o, e2_o, top_scr):
    mod = mod_ref[0]
    h = _norm_mod(x_ref[...], g_ref[...], mod[:, 4 * D:5 * D], mod[:, 3 * D:4 * D])
    ht_o[...] = h.T.astype(BF16)
    q = jnp.dot(h.astype(BF16), wq_ref[...], preferred_element_type=F32).astype(BF16)
    tm = q.shape[0]
    neg = -jnp.inf
    k = PEER_TOPK
    row8 = lax.broadcasted_iota(jnp.int32, (8, tm), 0)
    top_scr[...] = jnp.full(top_scr.shape, neg, F32)
    for hh in range(PEER_HEADS):
        st = []
        for t in range(2):
            c0 = (hh * 2 + t) * PEER_NKEYS
            s = lax.dot_general(sk_ref[hh, t], q[:, c0:c0 + PEER_NKEYS], NT,
                                preferred_element_type=F32)
            st.append(s)
            cur = s
            for j in range(k + 1):
                m = jnp.max(cur, axis=0, keepdims=True)
                top_scr[t, j:j + 1, :] = m
                cur = jnp.where(cur == m, neg, cur)
        t1 = top_scr[0]
        t2 = top_scr[1]
        slabs = [t1[0:1] + t2, t1[1:2] + t2[0:8], t1[2:3] + t2[0:8], t1[3:4] + t2[0:8],
                 t2[0:1] + t1[8:24]]
        for j in range(3):
            slabs.append(jnp.where(row8 >= 4, t2[j:j + 1] + t1[0:8], neg))
        cand = jnp.concatenate(slabs, axis=0)
        top = t1[0:1] + t2[0:1]
        zsum = jnp.zeros_like(top)
        kth = top
        for j in range(k):
            kth = jnp.max(cand, axis=0, keepdims=True)
            zsum = zsum + jnp.exp(kth - top)
            cand = jnp.where(cand == kth, neg, cand)
        nxt = jnp.max(cand, axis=0, keepdims=True)
        thr_o[hh] = 0.5 * (kth + nxt) - st[0]
        e1_o[hh] = jnp.exp(st[0] - t1[0:1]) * (1.0 / zsum)
        s2_o[hh] = st[1]
        e2_o[hh] = jnp.exp(st[1] - t2[0:1])


def _route(x, mod, norm_g, wq, sk, np_rows, seq_rows):
    n = x.shape[0]
    tm = ROW_TILE
    rowmap = _mod_row_map(np_rows // tm, seq_rows // tm)
    rt = jax.ShapeDtypeStruct((PEER_HEADS, PEER_NKEYS, n), F32)
    rspec = pl.BlockSpec((PEER_HEADS, PEER_NKEYS, tm), lambda i: (0, 0, i))
    return pl.pallas_call(
        _route_body,
        grid=(n // tm,),
        in_specs=[pl.BlockSpec((tm, D), lambda i: (i, 0)),
                  pl.BlockSpec((1, 1, 6 * D), lambda i: (rowmap(i), 0, 0)),
                  pl.BlockSpec((1, D), lambda i: (0, 0)),
                  pl.BlockSpec((D, 2 * PEER_HEADS * PEER_NKEYS), lambda i: (0, 0)),
                  pl.BlockSpec((PEER_HEADS, 2, PEER_NKEYS, 128), lambda i: (0, 0, 0, 0))],
        out_specs=[pl.BlockSpec((D, tm), lambda i: (0, i)), rspec, rspec, rspec, rspec],
        out_shape=[jax.ShapeDtypeStruct((D, n), BF16), rt, rt, rt, rt],
        scratch_shapes=[pltpu.VMEM((2, PEER_TOPK + 8, tm), F32)],
        compiler_params=_cp("arbitrary"),
        name="peer_route",
    )(x, mod, norm_g, wq, sk)


def _peer_body(ht_ref, u_ref, vt_ref, thr_ref, e1_ref, s2_ref, e2_ref, x_ref, mod_ref, fg_ref,
               o_ref, acc_ref, act_ref, p_ref, *, final_norm):
    j = pl.program_id(1)

    @pl.when(j == 0)
    def _():
        acc_ref[...] = jnp.zeros(acc_ref.shape, F32)

    act_ref[...] = jnp.dot(u_ref[...], ht_ref[...], preferred_element_type=F32)
    nk = PEER_NKEYS
    tok = act_ref.shape[1]
    na = PEER_EXP // nk
    a0 = pl.multiple_of(j * na, na)
    row8 = lax.broadcasted_iota(jnp.int32, (na, 128), 0)

    def first_key(r, carry):
        rows = pl.ds(pl.multiple_of(r * nk, nk), nk)
        pick = row8 == r
        for c0 in range(0, tok, 128):
            cs = slice(c0, c0 + 128)
            gate = jnp.zeros((nk, 128), F32)
            for hh in range(PEER_HEADS):
                thr = jnp.sum(jnp.where(pick, thr_ref[hh, pl.ds(a0, na), cs], 0.0), axis=0, keepdims=True)
                e1 = jnp.sum(jnp.where(pick, e1_ref[hh, pl.ds(a0, na), cs], 0.0), axis=0, keepdims=True)
                gate = gate + jnp.where(s2_ref[hh, :, cs] >= thr, e2_ref[hh, :, cs], 0.0) * e1
            p_ref[rows, cs] = (_gelu(act_ref[rows, cs]) * gate).astype(BF16)
        return carry

    lax.fori_loop(0, PEER_EXP // nk, first_key, 0)
    acc_ref[...] += jnp.dot(vt_ref[...], p_ref[...], preferred_element_type=F32)

    @pl.when(j == pl.num_programs(1) - 1)
    def _():
        y = x_ref[...] + mod_ref[0][:, 5 * D:6 * D] * acc_ref[...].T
        if final_norm:
            ms = jnp.mean(y * y, axis=-1, keepdims=True)
            y = y * lax.rsqrt(ms + EPS) * fg_ref[...]
        o_ref[...] = y


def _peer_dense(x, mod, ht, u, vt, thr, e1, s2, e2, fg, np_rows, seq_rows, final_norm):
    n = x.shape[0]
    tk, ex = PEER_TOK, PEER_EXP
    rowmap = _mod_row_map(np_rows // tk, seq_rows // tk)
    rspec = pl.BlockSpec((PEER_HEADS, PEER_NKEYS, tk), lambda i, j: (0, 0, i))
    return pl.pallas_call(
        functools.partial(_peer_body, final_norm=final_norm),
        grid=(n // tk, u.shape[0] // ex),
        in_specs=[pl.BlockSpec((D, tk), lambda i, j: (0, i)),
                  pl.BlockSpec((ex, D), lambda i, j: (j, 0)),
                  pl.BlockSpec((D, ex), lambda i, j: (0, j)),
                  rspec, rspec, rspec, rspec,
                  pl.BlockSpec((tk, D), lambda i, j: (i, 0)),
                  pl.BlockSpec((1, 1, 6 * D), lambda i, j: (rowmap(i), 0, 0)),
                  pl.BlockSpec((1, D), lambda i, j: (0, 0))],
        out_specs=pl.BlockSpec((tk, D), lambda i, j: (i, 0)),
        out_shape=jax.ShapeDtypeStruct((n, D), F32),
        scratch_shapes=[pltpu.VMEM((D, tk), F32), pltpu.VMEM((ex, tk), F32), pltpu.VMEM((ex, tk), BF16)],
        compiler_params=_cp("arbitrary", "arbitrary"),
        name="peer_dense",
    )(ht, u, vt, thr, e1, s2, e2, x, mod, fg)


def _peer(x, mod, norm_g, wq, sk, u, v, fg, np_rows, seq_rows, final_norm):
    ht, thr, e1, s2, e2 = _route(x, mod, norm_g, wq.astype(BF16), sk.astype(BF16), np_rows, seq_rows)
    return _peer_dense(x, mod, ht, u.astype(BF16), v.T.astype(BF16), thr, e1, s2, e2, fg,
                       np_rows, seq_rows, final_norm)


def _s5_in_body(x_ref, mod_ref, g_ref, w_ref, o_ref):
    mod = mod_ref[0]
    h = _norm_mod(x_ref[...], g_ref[...], mod[:, D:2 * D], mod[:, 0:D]).astype(BF16)
    o_ref[...] = jnp.dot(h, w_ref[...], preferred_element_type=F32).astype(BF16)


def _s5_in(x, mod, norm_g, w, np_rows, seq_rows):
    n = x.shape[0]
    tm = ROW_TILE
    rowmap = _mod_row_map(np_rows // tm, seq_rows // tm)
    return pl.pallas_call(
        _s5_in_body,
        grid=(n // tm,),
        in_specs=[pl.BlockSpec((tm, D), lambda i: (i, 0)),
                  pl.BlockSpec((1, 1, 6 * D), lambda i: (rowmap(i), 0, 0)),
                  pl.BlockSpec((1, D), lambda i: (0, 0)),
                  pl.BlockSpec((D, D), lambda i: (0, 0))],
        out_specs=pl.BlockSpec((tm, D), lambda i: (i, 0)),
        out_shape=jax.ShapeDtypeStruct((n, D), BF16),
        compiler_params=_cp("arbitrary"),
        name="s5_in",
    )(x, mod, norm_g, w)


def _s5_core_body(u_ref, m_ref, win_ref, cout_ref, pw_ref, *rest, kseq, has_init, emit_states):
    rest = list(rest)
    h0_ref = rest.pop(0) if has_init else None
    y_ref = rest.pop(0)
    sf_ref = rest.pop(0) if emit_states else None
    u = u_ref[...]
    kb = u.shape[0]
    y = jnp.dot(u, m_ref[...], preferred_element_type=F32)
    row = lax.broadcasted_iota(jnp.int32, (kb, 128), 0)
    kk = row & (kseq - 1)

    def cmul(a1, a2, x):
        return a1 * x + a2 * pltpu.roll(x, 64, 1)

    for d in range(2):
        s = jnp.dot(u, win_ref[d], preferred_element_type=F32)
        edge = (kk == 0) if d == 0 else (kk == kseq - 1)
        if has_init:
            h0 = h0_ref[d]
            s = s + jnp.where(edge, cmul(pw_ref[d, 0, 0:1], pw_ref[d, 0, 1:2], h0), 0.0)
        step, lvl = 1, 0
        while step < kseq:
            if d == 0:
                sh = pltpu.roll(s, step, 0)
                ok = kk >= step
            else:
                sh = pltpu.roll(s, kb - step, 0)
                ok = kk < kseq - step
            s = s + jnp.where(ok, cmul(pw_ref[d, lvl, 0:1], pw_ref[d, lvl, 1:2], sh), 0.0)
            step, lvl = step * 2, lvl + 1
        if emit_states:
            sf_ref[d] = s
        hin = pltpu.roll(s, 1, 0) if d == 0 else pltpu.roll(s, kb - 1, 0)
        if has_init:
            hin = jnp.where(edge, h0, hin)
        else:
            hin = jnp.where(edge, 0.0, hin)
        y = y + jnp.dot(hin.astype(BF16), cout_ref[d], preferred_element_type=F32)
    y_ref[...] = _gelu(y).astype(BF16)


def _s5_core(ug, mats, row0, rows, kseq, h0=None, emit_states=False):
    m, win, cout, pw = mats
    kb = 128
    b0 = row0 // kb
    nlv = pw.shape[2]
    in_specs = [pl.BlockSpec((None, kb, 256), lambda g, i: (g, b0 + i, 0)),
                pl.BlockSpec((None, 256, 256), lambda g, i: (g, 0, 0)),
                pl.BlockSpec((None, 2, 256, 128), lambda g, i: (g, 0, 0, 0)),
                pl.BlockSpec((None, 2, 128, 256), lambda g, i: (g, 0, 0, 0)),
                pl.BlockSpec((None, 2, nlv, 2, 128), lambda g, i: (g, 0, 0, 0, 0))]
    args = [ug, m, win, cout, pw]
    if h0 is not None:
        assert kseq == kb
        in_specs.append(pl.BlockSpec((None, None, 2, 1, 128), lambda g, i: (g, i, 0, 0, 0)))
        args.append(h0)
    out_specs = [pl.BlockSpec((None, kb, 256), lambda g, i: (g, i, 0))]
    out_shape = [jax.ShapeDtypeStruct((S5_GROUPS, rows, 256), BF16)]
    if emit_states:
        out_specs.append(pl.BlockSpec((None, 2, kb, 128), lambda g, i: (g, 0, i, 0)))
        out_shape.append(jax.ShapeDtypeStruct((S5_GROUPS, 2, rows, 128), F32))
    return pl.pallas_call(
        functools.partial(_s5_core_body, kseq=kseq, has_init=h0 is not None, emit_states=emit_states),
        grid=(S5_GROUPS, rows // kb),
        in_specs=in_specs,
        out_specs=out_specs,
        out_shape=out_shape,
        compiler_params=_cp("arbitrary", "arbitrary"),
        name="s5_core_ctx" if h0 is not None else "s5_core",
    )(*args)


def _s5_out_body(x_ref, mod_ref, y_ref, w_ref, o_ref):
    zz = jnp.dot(y_ref[...], w_ref[...], preferred_element_type=F32)
    za = zz[:, 0:D]
    zb = zz[:, D:2 * D]
    o_ref[...] = x_ref[...] + mod_ref[0][:, 2 * D:3 * D] * (za / (1.0 + jnp.exp(-zb)))


def _s5_out(x, mod, y, w, np_rows, seq_rows):
    n = x.shape[0]
    tm = ROW_TILE
    rowmap = _mod_row_map(np_rows // tm, seq_rows // tm)
    return pl.pallas_call(
        _s5_out_body,
        grid=(n // tm,),
        in_specs=[pl.BlockSpec((tm, D), lambda i: (i, 0)),
                  pl.BlockSpec((1, 1, 6 * D), lambda i: (rowmap(i), 0, 0)),
                  pl.BlockSpec((tm, D), lambda i: (i, 0)),
                  pl.BlockSpec((D, 2 * D), lambda i: (0, 0))],
        out_specs=pl.BlockSpec((tm, D), lambda i: (i, 0)),
        out_shape=jax.ShapeDtypeStruct((n, D), F32),
        compiler_params=_cp("arbitrary"),
        name="s5_out",
    )(x, mod, y, w)


def _s5_matrices(a_re, a_im, log_dt, b_re, b_im, c_re, c_im, d_skip):
    cs = S5_CHUNK
    dt = jnp.exp(log_dt)[..., None]
    lr, li = a_re * dt, a_im * dt

    def lam_pow(tau):
        mag = jnp.exp(lr[..., None] * tau)
        ang = li[..., None] * tau
        return mag * jnp.cos(ang), mag * jnp.sin(ang)

    l1r, l1i = jnp.exp(lr) * jnp.cos(li), jnp.exp(lr) * jnp.sin(li)
    den = a_re * a_re + a_im * a_im
    cr = ((l1r - 1.0) * a_re + l1i * a_im) / den
    ci = (l1i * a_re - (l1r - 1.0) * a_im) / den
    bbr = cr[..., None] * b_re - ci[..., None] * b_im
    bbi = cr[..., None] * b_im + ci[..., None] * b_re

    tau = jnp.arange(cs + 1, dtype=F32)
    pr, pi = lam_pow(tau)
    clr = c_re[..., None] * pr[:, :, None] - c_im[..., None] * pi[:, :, None]
    cli = c_re[..., None] * pi[:, :, None] + c_im[..., None] * pr[:, :, None]
    kern = (jnp.einsum('dgcpt,dgpe->dgtce', clr[..., :cs], bbr, precision=HI)
            - jnp.einsum('dgcpt,dgpe->dgtce', cli[..., :cs], bbi, precision=HI))
    s_idx = jnp.arange(cs)[:, None]
    t_idx = jnp.arange(cs)[None, :]
    lag_f = jnp.clip(t_idx - s_idx, 0, cs - 1)
    lag_b = jnp.clip(s_idx - t_idx, 0, cs - 1)
    mf = jnp.where((s_idx <= t_idx)[None, :, :, None, None], kern[0][:, lag_f], 0.0)
    mb = jnp.where((s_idx >= t_idx)[None, :, :, None, None], kern[1][:, lag_b], 0.0)
    eye_t = (s_idx == t_idx)[None, :, :, None, None]
    eye_c = jnp.eye(S5_CH, dtype=F32)[None, None, None]
    dsk = d_skip.reshape(S5_GROUPS, 1, 1, S5_CH, 1)
    m = mf + mb + jnp.where(eye_t, eye_c * dsk, 0.0)
    m = m.transpose(0, 1, 4, 2, 3).reshape(S5_GROUPS, cs * S5_CH, cs * S5_CH)

    def win_dir(d, powers):
        wr = pr[d][..., powers][:, :, :, None] * bbr[d][:, :, None, :] - pi[d][..., powers][:, :, :, None] * bbi[d][:, :, None, :]
        wi = pr[d][..., powers][:, :, :, None] * bbi[d][:, :, None, :] + pi[d][..., powers][:, :, :, None] * bbr[d][:, :, None, :]
        w = jnp.concatenate([wr, wi], axis=1)
        return w.transpose(0, 2, 3, 1).reshape(S5_GROUPS, cs * S5_CH, 2 * S5_P)

    win = jnp.stack([win_dir(0, jnp.arange(cs - 1, -1, -1)), win_dir(1, jnp.arange(cs))], axis=1)

    def cout_dir(d, powers):
        zr = clr[d][..., powers]
        zi = cli[d][..., powers]
        z = jnp.concatenate([zr, -zi], axis=2)
        return z.transpose(0, 2, 3, 1).reshape(S5_GROUPS, 2 * S5_P, cs * S5_CH)

    cout = jnp.stack([cout_dir(0, jnp.arange(1, cs + 1)), cout_dir(1, jnp.arange(cs, 0, -1))], axis=1)

    lv = cs * (2.0 ** jnp.arange(7, dtype=F32))
    qr, qi = lam_pow(lv)
    a1 = jnp.concatenate([qr, qr], axis=2)
    a2 = jnp.concatenate([-qi, qi], axis=2)
    pw = jnp.stack([a1, a2], axis=-1).transpose(1, 0, 3, 4, 2)
    return m.astype(BF16), win.astype(BF16), cout.astype(BF16), pw


def _rope_tables(dec_seq):
    rows = dec_seq // GRID_W
    row_id = jnp.repeat(jnp.arange(rows), GRID_W).astype(F32)
    col_id = jnp.tile(jnp.arange(GRID_W), rows).astype(F32)
    quarter = DA_DIM // 4
    inv = ROPE_BASE ** (-jnp.arange(quarter, dtype=F32) / quarter)
    ang_r = row_id[:, None] * inv
    ang_c = col_id[:, None] * inv
    ang = jnp.concatenate([ang_r, ang_r, ang_c, ang_c], axis=-1)
    ang = jnp.tile(ang, (1, 2 * DA_HEADS))
    pad = jnp.zeros((ROW_TILE, ang.shape[1]), F32)
    ang = jnp.concatenate([pad, ang], axis=0)
    return jnp.cos(ang), jnp.sin(ang)


def kernel(x_prompt, x_sample, cache_da_k, cache_da_v, state_gla, state_s5, c, c_ctx,
           ada_w, ada_b, norm_g, final_norm_g, mix0_w_in, mix0_w_out, da_lam, da_subln_g,
           gla_gate_up, gla_gate_bias, gla_norm_g, s5_w_in, s5_a_re, s5_a_im, s5_log_dt,
           s5_b_re, s5_b_im, s5_c_re, s5_c_im, s5_d, s5_w_out,
           peer_w_query, peer_sub_keys, peer_u, peer_v):
    bp, lp, _ = x_prompt.shape
    bs, ls, _ = x_sample.shape
    npr, nsr = bp * lp, bs * ls
    assert npr % ls == 0 and ls % ROW_TILE == 0 and lp % ROW_TILE == 0
    x = jnp.concatenate([x_prompt.reshape(npr, D), x_sample.reshape(nsr, D)], axis=0)

    cond = jnp.concatenate([c_ctx[None], c, jnp.zeros((15 - bs, D), F32)], axis=0)
    mod_all = _ada_mod(cond, ada_w, ada_b)
    mods = [mod_all[l].reshape(16, 1, 6 * D) for l in range(mod_all.shape[0])]
    fg = final_norm_g.reshape(1, D)

    lam_init = 0.8 - 0.6 * math.exp(-0.3 * 0)
    cos_t, sin_t = _rope_tables(ls)
    w_in = jnp.pad(mix0_w_in[0], ((0, 0), (0, IN0_PAD - mix0_w_in.shape[2]))).astype(BF16)
    q, kb, vb, k32, v32, gqk, gv, gr, low = _inproj0(
        x, mods[0], norm_g[0, 0].reshape(1, D), w_in, cos_t, sin_t, npr, ls)
    sub_g = da_subln_g[0].reshape(1, 2 * DA_DIM)
    past = cache_da_k.shape[2]
    o_da = jnp.concatenate([
        _diff_attention(q, kb, vb, da_lam[0], sub_g, lam_init, 0, bp, lp),
        _diff_attention(q, kb, vb, da_lam[0], sub_g, lam_init, npr, bs, ls,
                        cache=(cache_da_k[:, 0].reshape(bs, past, 512), cache_da_v[:, 0].reshape(bs, past, 512))),
    ], axis=0)
    hk = GLA_HEADS * GLA_DK
    gup = jnp.zeros((2, 128, hk), F32)
    for d in range(2):
        gup = gup.at[d, d * GLA_RANK:(d + 1) * GLA_RANK].set(gla_gate_up[0, d])
    gbias = gla_gate_bias[0].reshape(2, 1, hk)
    ng = gla_norm_g[0].reshape(1, GLA_DV)
    s0 = state_gla[:, 0].transpose(0, 1, 4, 2, 3).reshape(bs, 2, GLA_DV, hk)
    og_p, st_p = _gla(gqk, gv, gr, low, gup, gbias, ng, 0, bp, lp)
    og_s, _ = _gla(gqk, gv, gr, low, gup, gbias, ng, npr, bs, ls, s0=s0)
    o_g = jnp.concatenate([og_p, og_s], axis=0)
    x = _outproj0(x, mods[0], o_da, o_g, mix0_w_out[0].astype(BF16), npr, ls)
    x = _peer(x, mods[0], norm_g[0, 1].reshape(1, D), peer_w_query[0],
              peer_sub_keys[0], peer_u[0], peer_v[0], fg, npr, ls, final_norm=False)

    u = _s5_in(x, mods[1], norm_g[1, 0].reshape(1, D), s5_w_in[0].astype(BF16), npr, ls)
    n = npr + nsr
    cs = S5_CHUNK
    ug = u.reshape(n // cs, cs, S5_GROUPS, S5_CH).transpose(2, 0, 1, 3).reshape(S5_GROUPS, n // cs, cs * S5_CH)
    mats = _s5_matrices(s5_a_re[0], s5_a_im[0], s5_log_dt[0], s5_b_re[0], s5_b_im[0],
                        s5_c_re[0], s5_c_im[0], s5_d[0])
    h0 = state_s5[:, 0]
    h0 = h0.transpose(2, 0, 1, 4, 3).reshape(S5_GROUPS, bs, 2, 1, 2 * S5_P)
    y_p, sf = _s5_core(ug, mats, 0, npr // cs, lp // cs, emit_states=True)
    (y_s,) = _s5_core(ug, mats, npr // cs, nsr // cs, ls // cs, h0=h0)
    yg = jnp.concatenate([y_p, y_s], axis=1)
    y = yg.reshape(S5_GROUPS, n // cs, cs, S5_CH).transpose(1, 2, 0, 3).reshape(n, D)
    x = _s5_out(x, mods[1], y, s5_w_out[0].astype(BF16), npr, ls)
    x = _peer(x, mods[1], norm_g[1, 1].reshape(1, D), peer_w_query[1],
              peer_sub_keys[1], peer_u[1], peer_v[1], fg, npr, ls, final_norm=True)

    y_prompt = x[:npr].reshape(bp, lp, D)
    y_sample = x[npr:].reshape(bs, ls, D)
    new_k = k32[:npr].reshape(bp, 1, lp, 2 * DA_HEADS, DA_DIM)
    new_v = v32[:npr].reshape(bp, 1, lp, DA_HEADS, 2 * DA_DIM)
    new_gla = st_p.reshape(bp, 2, GLA_DV, GLA_HEADS, GLA_DK).transpose(0, 1, 3, 4, 2)[:, None]
    kc = lp // cs
    sf = sf.reshape(S5_GROUPS, 2, bp, kc, 2, S5_P)
    fin = jnp.stack([sf[:, 0, :, kc - 1], sf[:, 1, :, 0]], axis=1)
    new_s5 = fin.transpose(2, 1, 0, 4, 3)[:, None]
    return (y_prompt, y_sample, new_k, new_v, new_gla, new_s5)
```

```python
import functools
import math

import jax
import jax.numpy as jnp
from jax import lax
from jax.experimental import pallas as pl
from jax.experimental.pallas import tpu as pltpu

F32 = jnp.float32
BF16 = jnp.bfloat16
HI = lax.Precision.HIGHEST

D = 1024
EPS = 1e-6
ROPE_BASE = 10000.0
GRID_W = 64
DA_HEADS = 4
DA_DIM = 64
GLA_HEADS = 4
GLA_DK = 64
GLA_DV = 128
GLA_RANK = 16
GLA_TAU = 16.0
GLA_CHUNK = 64
S5_CH = 16
S5_GROUPS = D // S5_CH
S5_P = 64
S5_CHUNK = 16
PEER_HEADS = 8
PEER_NKEYS = 128
PEER_TOPK = 16
IN0_PAD = 3200

ROW_TILE = 256
PEER_TOK = 512
PEER_EXP = 1024
PEER_SUB = 256
VMEM_LIMIT = 48 * 1024 * 1024

NT = (((1,), (1,)), ((), ()))
TN = (((0,), (0,)), ((), ()))


def _cp(*sem):
    return pltpu.CompilerParams(dimension_semantics=sem, vmem_limit_bytes=VMEM_LIMIT)


def _norm_mod(x, g, sc, sh):
    ms = jnp.mean(x * x, axis=-1, keepdims=True)
    return x * lax.rsqrt(ms + EPS) * g * (1.0 + sc) + sh


def _gelu(x):
    return 0.5 * x * (1.0 + jnp.tanh(0.7978845608028654 * (x + 0.044715 * (x * x * x))))


def _mod_row_map(np_blocks, seq_blocks):
    def f(i):
        return jnp.where(i < np_blocks, 0, 1 + (i - np_blocks) // seq_blocks)
    return f


def _ada_body(c_ref, w_ref, b_ref, o_ref):
    c = c_ref[...]
    s = c / (1.0 + jnp.exp(-c))
    o_ref[0] = jnp.dot(s, w_ref[0], precision=HI, preferred_element_type=F32) + b_ref[0]


def _ada_mod(cond, ada_w, ada_b):
    depth, _, n = ada_w.shape
    rows = cond.shape[0]
    tn = 1536
    return pl.pallas_call(
        _ada_body,
        grid=(depth, n // tn),
        in_specs=[pl.BlockSpec((rows, D), lambda l, j: (0, 0)),
                  pl.BlockSpec((1, D, tn), lambda l, j: (l, 0, j)),
                  pl.BlockSpec((1, 1, tn), lambda l, j: (l, 0, j))],
        out_specs=pl.BlockSpec((1, rows, tn), lambda l, j: (l, 0, j)),
        out_shape=jax.ShapeDtypeStruct((depth, rows, n), F32),
        compiler_params=_cp("arbitrary", "arbitrary"),
        name="ada_mod",
    )(cond, ada_w, ada_b.reshape(depth, 1, n))


def _inproj0_body(x_ref, mod_ref, g_ref, w_ref, cos_ref, sin_ref,
                  q_o, kb_o, vb_o, k32_o, v32_o, gqk_o, gv_o, gr_o, low_o):
    mod = mod_ref[0]
    h = _norm_mod(x_ref[...], g_ref[...], mod[:, D:2 * D], mod[:, 0:D]).astype(BF16)
    y = jnp.dot(h, w_ref[...], preferred_element_type=F32)
    cos = cos_ref[...]
    sin = sin_ref[...]
    lane = lax.broadcasted_iota(jnp.int32, cos.shape, 1)
    first = (lane & 31) < 16

    def rope(t):
        rot = jnp.where(first, -pltpu.roll(t, 512 - 16, 1), pltpu.roll(t, 16, 1))
        return t * cos + rot * sin

    q = rope(y[:, 0:512])
    k = rope(y[:, 512:1024])
    v = y[:, 1024:1536]
    q_o[...] = q.astype(BF16)
    kb_o[...] = k.astype(BF16)
    vb_o[...] = v.astype(BF16)
    k32_o[...] = k
    v32_o[...] = v
    gqk_o[...] = y[:, 1536:2048].astype(BF16)
    gv_o[...] = y[:, 2048:2560].astype(BF16)
    gr_o[...] = y[:, 2560:3072].astype(BF16)
    low_o[...] = y[:, 3072:3200]


def _inproj0(x, mod, norm_g, w, cos_t, sin_t, np_rows, seq_rows):
    n = x.shape[0]
    tm = ROW_TILE
    npb, sqb = np_rows // tm, seq_rows // tm
    rowmap = _mod_row_map(npb, sqb)
    posmap = lambda i: (jnp.where(i < npb, 0, 1 + (i - npb) % sqb), 0)
    tok = lambda w_: pl.BlockSpec((tm, w_), lambda i: (i, 0))
    outs = [(512, BF16), (512, BF16), (512, BF16), (512, F32), (512, F32),
            (512, BF16), (512, BF16), (512, BF16), (128, F32)]
    return pl.pallas_call(
        _inproj0_body,
        grid=(n // tm,),
        in_specs=[tok(D),
                  pl.BlockSpec((1, 1, 6 * D), lambda i: (rowmap(i), 0, 0)),
                  pl.BlockSpec((1, D), lambda i: (0, 0)),
                  pl.BlockSpec((D, IN0_PAD), lambda i: (0, 0)),
                  pl.BlockSpec((tm, 512), posmap),
                  pl.BlockSpec((tm, 512), posmap)],
        out_specs=[tok(w_) for w_, _ in outs],
        out_shape=[jax.ShapeDtypeStruct((n, w_), dt) for w_, dt in outs],
        compiler_params=_cp("arbitrary"),
        name="inproj0",
    )(x, mod, norm_g, w, cos_t, sin_t)


def _attn_body(lam_ref, sg_ref, q_ref, k_ref, v_ref, *rest, lam_init, has_cache):
    if has_cache:
        kc_ref, vc_ref, o_ref = rest
    else:
        (o_ref,) = rest
    lf = lam_ref[...]
    lam = (jnp.exp(jnp.sum(lf[0:1] * lf[1:2], axis=-1, keepdims=True))
           - jnp.exp(jnp.sum(lf[2:3] * lf[3:4], axis=-1, keepdims=True)) + lam_init)
    q = q_ref[...] * (DA_DIM ** -0.5)
    lane = lax.broadcasted_iota(jnp.int32, q.shape, 1)
    zero = jnp.zeros_like(q)
    qm = (jnp.where(lane < DA_DIM, q, zero), jnp.where(lane >= DA_DIM, q, zero))
    k = k_ref[...]
    s = [lax.dot_general(qm[m], k, NT, preferred_element_type=F32) for m in range(2)]
    mx = [jnp.max(s[m], axis=-1, keepdims=True) for m in range(2)]
    if has_cache:
        kc = kc_ref[...].astype(BF16)
        sc = [lax.dot_general(qm[m], kc, NT, preferred_element_type=F32) for m in range(2)]
        mx = [jnp.maximum(mx[m], jnp.max(sc[m], axis=-1, keepdims=True)) for m in range(2)]
    e = [jnp.exp(s[m] - mx[m]) for m in range(2)]
    z = [jnp.sum(e[m], axis=-1, keepdims=True) for m in range(2)]
    if has_cache:
        ec = [jnp.exp(sc[m] - mx[m]) for m in range(2)]
        z = [z[m] + jnp.sum(ec[m], axis=-1, keepdims=True) for m in range(2)]
    w0 = 1.0 / z[0]
    w1 = lam / z[1]
    o = jnp.dot((e[0] * w0 - e[1] * w1).astype(BF16), v_ref[...], preferred_element_type=F32)
    if has_cache:
        o = o + jnp.dot((ec[0] * w0 - ec[1] * w1).astype(BF16), vc_ref[...].astype(BF16),
                        preferred_element_type=F32)
    ms = jnp.mean(o * o, axis=-1, keepdims=True)
    o_ref[...] = (o * lax.rsqrt(ms + EPS) * sg_ref[...] * (1.0 - lam_init)).astype(o_ref.dtype)


def _diff_attention(q, k, v, da_lam, subln_g, lam_init, row0, batch, seq, cache=None):
    tq = 256
    qb0, kb0 = row0 // tq, row0 // seq
    in_specs = [pl.BlockSpec((4, DA_DIM), lambda b, h, i: (0, 0)),
                pl.BlockSpec((1, 128), lambda b, h, i: (0, 0)),
                pl.BlockSpec((tq, 128), lambda b, h, i: (qb0 + b * (seq // tq) + i, h)),
                pl.BlockSpec((seq, 128), lambda b, h, i: (kb0 + b, h)),
                pl.BlockSpec((seq, 128), lambda b, h, i: (kb0 + b, h))]
    args = [da_lam, subln_g, q, k, v]
    if cache is not None:
        past = cache[0].shape[1]
        in_specs += [pl.BlockSpec((None, past, 128), lambda b, h, i: (b, 0, h))] * 2
        args += list(cache)
    return pl.pallas_call(
        functools.partial(_attn_body, lam_init=lam_init, has_cache=cache is not None),
        grid=(batch, DA_HEADS, seq // tq),
        in_specs=in_specs,
        out_specs=pl.BlockSpec((tq, 128), lambda b, h, i: (b * (seq // tq) + i, h)),
        out_shape=jax.ShapeDtypeStruct((batch * seq, 512), BF16),
        compiler_params=_cp("arbitrary", "arbitrary", "arbitrary"),
        name="diff_attn_ctx" if cache is not None else "diff_attn",
    )(*args)


def _gla_body(qk_ref, v_ref, r_ref, low_ref, gup_ref, gb_ref, ng_ref, *rest, seq, has_init):
    if has_init:
        s0_ref, o_ref, sf_ref, st_scr, of_scr, ob_scr, la_scr = rest
    else:
        o_ref, sf_ref, st_scr, of_scr, ob_scr, la_scr = rest
    c = GLA_CHUNK
    n = seq // c
    hk = GLA_HEADS * GLA_DK
    low = low_ref[...]
    for d in range(2):
        z = jnp.dot(low, gup_ref[d], precision=HI, preferred_element_type=F32) + gb_ref[d]
        la_scr[d] = (jnp.minimum(z, 0.0) - jnp.log(1.0 + jnp.exp(-jnp.abs(z)))) * (1.0 / GLA_TAU)
    if has_init:
        st_scr[...] = s0_ref[0]
    else:
        st_scr[...] = jnp.zeros(st_scr.shape, F32)
    row = lax.broadcasted_iota(jnp.int32, (c, c), 0)
    col = lax.broadcasted_iota(jnp.int32, (c, c), 1)
    keep = (col <= row, col >= row)
    lane = lax.broadcasted_iota(jnp.int32, (c, 128), 1)
    low_half = lane < GLA_DK

    def chunk(d, r0):
        rows = pl.ds(r0, c)
        qk = qk_ref[rows, :].astype(F32)
        q = qk[:, 0:hk] * (GLA_DK ** -0.5)
        k = qk[:, hk:2 * hk]
        v = v_ref[rows, :]
        g = la_scr[d, rows, :]
        b = jnp.dot(keep[d].astype(F32), g, precision=HI, preferred_element_type=F32)
        b_tot = b[c - 1:c] if d == 0 else b[0:1]
        q_dec = q * jnp.exp(b)
        k_inv = k * jnp.exp(-b)
        k_end = k * jnp.exp(b_tot - b)
        st = st_scr[d]
        st_b = st.astype(BF16)
        outs, news = [], []
        for pair in range(GLA_HEADS // 2):
            ps = slice(pair * 128, (pair + 1) * 128)
            kin = k_inv[:, ps].astype(BF16)
            new = None
            for sub in range(2):
                hh = pair * 2 + sub
                vs = slice(hh * GLA_DV, (hh + 1) * GLA_DV)
                sel = low_half if sub == 0 else jnp.logical_not(low_half)
                qd = jnp.where(sel, q_dec[:, ps], 0.0).astype(BF16)
                ke = jnp.where(sel, k_end[:, ps], 0.0).astype(BF16)
                inter = lax.dot_general(qd, st_b[:, ps], NT, preferred_element_type=F32)
                att = lax.dot_general(qd, kin, NT, preferred_element_type=F32)
                att = jnp.where(keep[d], att, 0.0).astype(BF16)
                outs.append(inter + jnp.dot(att, v[:, vs], preferred_element_type=F32))
                upd = lax.dot_general(v[:, vs], ke, TN, preferred_element_type=F32)
                new = upd if new is None else new + upd
            news.append(new)
        st_scr[d] = st * jnp.exp(b_tot) + jnp.concatenate(news, axis=1)
        return jnp.concatenate(outs, axis=1)

    def step(i, carry):
        rf = pl.multiple_of(i * c, c)
        of_scr[pl.ds(rf, c), :] = chunk(0, rf)
        rb = pl.multiple_of((n - 1 - i) * c, c)
        ob_scr[pl.ds(rb, c), :] = chunk(1, rb)
        return carry

    lax.fori_loop(0, n, step, 0)
    sf_ref[0] = st_scr[...]

    fin = 256

    def finish(i, carry):
        rows = pl.ds(pl.multiple_of(i * fin, fin), fin)
        o = of_scr[rows, :] + ob_scr[rows, :]
        r = r_ref[rows, :].astype(F32)
        gate = r / (1.0 + jnp.exp(-r))
        ng = ng_ref[...]
        for hh in range(GLA_HEADS):
            vs = slice(hh * GLA_DV, (hh + 1) * GLA_DV)
            oh = o[:, vs]
            ms = jnp.mean(oh * oh, axis=-1, keepdims=True)
            o_ref[rows, vs] = (oh * lax.rsqrt(ms + EPS) * ng * gate[:, vs]).astype(BF16)
        return carry

    lax.fori_loop(0, seq // fin, finish, 0)


def _gla(gqk, gv, gr, low, gup, gb, ng, row0, batch, seq, s0=None):
    b0 = row0 // seq
    tokmap = lambda b: (b0 + b, 0)
    hk = GLA_HEADS * GLA_DK
    in_specs = [pl.BlockSpec((seq, 512), tokmap), pl.BlockSpec((seq, 512), tokmap),
                pl.BlockSpec((seq, 512), tokmap), pl.BlockSpec((seq, 128), tokmap),
                pl.BlockSpec((2, 128, hk), lambda b: (0, 0, 0)),
                pl.BlockSpec((2, 1, hk), lambda b: (0, 0, 0)),
                pl.BlockSpec((1, GLA_DV), lambda b: (0, 0))]
    args = [gqk, gv, gr, low, gup, gb, ng]
    if s0 is not None:
        in_specs.append(pl.BlockSpec((1, 2, GLA_DV, hk), lambda b: (b, 0, 0, 0)))
        args.append(s0)
    return pl.pallas_call(
        functools.partial(_gla_body, seq=seq, has_init=s0 is not None),
        grid=(batch,),
        in_specs=in_specs,
        out_specs=[pl.BlockSpec((seq, 512), lambda b: (b, 0)),
                   pl.BlockSpec((1, 2, GLA_DV, hk), lambda b: (b, 0, 0, 0))],
        out_shape=[jax.ShapeDtypeStruct((batch * seq, 512), BF16),
                   jax.ShapeDtypeStruct((batch, 2, GLA_DV, hk), F32)],
        scratch_shapes=[pltpu.VMEM((2, GLA_DV, hk), F32),
                        pltpu.VMEM((seq, 512), F32), pltpu.VMEM((seq, 512), F32),
                        pltpu.VMEM((2, seq, hk), F32)],
        compiler_params=_cp("arbitrary"),
        name="gla_ctx" if s0 is not None else "gla",
    )(*args)


def _outproj0_body(x_ref, mod_ref, oda_ref, og_ref, w_ref, o_ref):
    m = (jnp.dot(oda_ref[...], w_ref[0:512, :], preferred_element_type=F32)
         + jnp.dot(og_ref[...], w_ref[512:1024, :], preferred_element_type=F32))
    o_ref[...] = x_ref[...] + mod_ref[0][:, 2 * D:3 * D] * m


def _outproj0(x, mod, oda, og, w, np_rows, seq_rows):
    n = x.shape[0]
    tm = ROW_TILE
    rowmap = _mod_row_map(np_rows // tm, seq_rows // tm)
    return pl.pallas_call(
        _outproj0_body,
        grid=(n // tm,),
        in_specs=[pl.BlockSpec((tm, D), lambda i: (i, 0)),
                  pl.BlockSpec((1, 1, 6 * D), lambda i: (rowmap(i), 0, 0)),
                  pl.BlockSpec((tm, 512), lambda i: (i, 0)),
                  pl.BlockSpec((tm, 512), lambda i: (i, 0)),
                  pl.BlockSpec((D, D), lambda i: (0, 0))],
        out_specs=pl.BlockSpec((tm, D), lambda i: (i, 0)),
        out_shape=jax.ShapeDtypeStruct((n, D), F32),
        compiler_params=_cp("arbitrary"),
        name="outproj0",
    )(x, mod, oda, og, w)


def _route_body(x_ref, mod_ref, g_ref, wq_ref, sk_ref, ht_o, nsel_o, e1_o, rank_o, e2_o, top_scr):
    mod = mod_ref[0]
    h = _norm_mod(x_ref[...], g_ref[...], mod[:, 4 * D:5 * D], mod[:, 3 * D:4 * D])
    ht_o[...] = h.T.astype(BF16)
    q = jnp.dot(h.astype(BF16), wq_ref[...], preferred_element_type=F32).astype(BF16)
    tm = q.shape[0]
    neg = -jnp.inf
    k = PEER_TOPK
    row8 = lax.broadcasted_iota(jnp.int32, (8, tm), 0)
    for hh in range(PEER_HEADS):
        st = []
        rank = jnp.full((PEER_NKEYS, tm), float(k), F32)
        for t in range(2):
            c0 = (hh * 2 + t) * PEER_NKEYS
            s = lax.dot_general(sk_ref[hh, t], q[:, c0:c0 + PEER_NKEYS], NT,
                                preferred_element_type=F32)
            st.append(s)
            cur = s
            for j in range(k):
                m = jnp.max(cur, axis=0, keepdims=True)
                top_scr[t, j:j + 1, :] = m
                hit = cur == m
                if t == 1:
                    rank = jnp.where(hit, float(j), rank)
                cur = jnp.where(hit, neg, cur)
        t1 = top_scr[0]
        t2 = top_scr[1]
        slabs = [t1[0:1] + t2, t1[1:2] + t2[0:8], t1[2:3] + t2[0:8], t1[3:4] + t2[0:8],
                 t2[0:1] + t1[8:16]]
        for j in range(3):
            slabs.append(jnp.where(row8 >= 4, t2[j:j + 1] + t1[0:8], neg))
        cand = jnp.concatenate(slabs, axis=0)
        top = t1[0:1] + t2[0:1]
        zsum = jnp.zeros_like(top)
        kth = top
        for j in range(k):
            kth = jnp.max(cand, axis=0, keepdims=True)
            zsum = zsum + jnp.exp(kth - top)
            cand = jnp.where(cand == kth, neg, cand)
        nsel = jnp.zeros((PEER_NKEYS, tm), F32)
        for j in range(k):
            nsel = jnp.where(st[0] + t2[j:j + 1] >= kth, float(j + 1), nsel)
        hs = slice(hh * PEER_NKEYS, (hh + 1) * PEER_NKEYS)
        nsel_o[hs, :] = nsel
        e1_o[hs, :] = jnp.exp(st[0] - t1[0:1]) * (1.0 / zsum)
        rank_o[hs, :] = rank.astype(BF16)
        e2_o[hs, :] = jnp.exp(st[1] - t2[0:1]).astype(BF16)


def _route(x, mod, norm_g, wq, sk, np_rows, seq_rows):
    n = x.shape[0]
    tm = ROW_TILE
    rowmap = _mod_row_map(np_rows // tm, seq_rows // tm)
    rt = lambda dt: jax.ShapeDtypeStruct((PEER_HEADS * PEER_NKEYS, n), dt)
    rspec = pl.BlockSpec((PEER_HEADS * PEER_NKEYS, tm), lambda i: (0, i))
    return pl.pallas_call(
        _route_body,
        grid=(n // tm,),
        in_specs=[pl.BlockSpec((tm, D), lambda i: (i, 0)),
                  pl.BlockSpec((1, 1, 6 * D), lambda i: (rowmap(i), 0, 0)),
                  pl.BlockSpec((1, D), lambda i: (0, 0)),
                  pl.BlockSpec((D, 2 * PEER_HEADS * PEER_NKEYS), lambda i: (0, 0)),
                  pl.BlockSpec((PEER_HEADS, 2, PEER_NKEYS, 128), lambda i: (0, 0, 0, 0))],
        out_specs=[pl.BlockSpec((D, tm), lambda i: (0, i)), rspec, rspec, rspec, rspec],
        out_shape=[jax.ShapeDtypeStruct((D, n), BF16), rt(F32), rt(F32), rt(BF16), rt(BF16)],
        scratch_shapes=[pltpu.VMEM((2, PEER_TOPK, tm), F32)],
        compiler_params=_cp("arbitrary"),
        name="peer_route",
    )(x, mod, norm_g, wq, sk)


def _peer_body(ht_ref, u_ref, vt_ref, nsel_ref, e1_ref, rank_ref, e2_ref, x_ref, mod_ref, fg_ref,
               o_ref, acc_ref, p_ref, *, final_norm):
    j = pl.program_id(1)

    @pl.when(j == 0)
    def _():
        acc_ref[...] = jnp.zeros(acc_ref.shape, F32)

    nk = PEER_NKEYS
    tok = ht_ref.shape[1]
    na = PEER_EXP // nk
    a0 = pl.multiple_of(j * na, na)
    rg = 8
    zero = jnp.zeros((rg, PEER_SUB), BF16)

    def row_bcast(ref, hh, r, ts):
        w = ref[pl.ds(hh * nk + a0, na), ts][r:r + 1]
        return jnp.broadcast_to(w, (rg, PEER_SUB)).astype(BF16)

    for t0 in range(0, tok, PEER_SUB):
        ts = slice(t0, t0 + PEER_SUB)
        act = jnp.dot(u_ref[...], ht_ref[:, ts], preferred_element_type=F32)
        for r in range(na):
            gate = [zero] * (nk // rg)
            for hh in range(PEER_HEADS):
                ns = row_bcast(nsel_ref, hh, r, ts)
                e1 = row_bcast(e1_ref, hh, r, ts)
                for g in range(nk // rg):
                    bs = slice(hh * nk + g * rg, hh * nk + (g + 1) * rg)
                    gate[g] = gate[g] + jnp.where(rank_ref[bs, ts] < ns, e2_ref[bs, ts], zero) * e1
            for g in range(nk // rg):
                rows = slice(r * nk + g * rg, r * nk + (g + 1) * rg)
                p_ref[rows, ts] = _gelu(act[rows, :]).astype(BF16) * gate[g]
        acc_ref[:, ts] += jnp.dot(vt_ref[...], p_ref[:, ts], preferred_element_type=F32)

    @pl.when(j == pl.num_programs(1) - 1)
    def _():
        y = x_ref[...] + mod_ref[0][:, 5 * D:6 * D] * acc_ref[...].T
        if final_norm:
            ms = jnp.mean(y * y, axis=-1, keepdims=True)
            y = y * lax.rsqrt(ms + EPS) * fg_ref[...]
        o_ref[...] = y


def _peer_dense(x, mod, ht, u, vt, nsel, e1, rank, e2, fg, np_rows, seq_rows, final_norm):
    n = x.shape[0]
    tk, ex = PEER_TOK, PEER_EXP
    rowmap = _mod_row_map(np_rows // tk, seq_rows // tk)
    rspec = pl.BlockSpec((PEER_HEADS * PEER_NKEYS, tk), lambda i, j: (0, i))
    return pl.pallas_call(
        functools.partial(_peer_body, final_norm=final_norm),
        grid=(n // tk, u.shape[0] // ex),
        in_specs=[pl.BlockSpec((D, tk), lambda i, j: (0, i)),
                  pl.BlockSpec((ex, D), lambda i, j: (j, 0)),
                  pl.BlockSpec((D, ex), lambda i, j: (0, j)),
                  rspec, rspec, rspec, rspec,
                  pl.BlockSpec((tk, D), lambda i, j: (i, 0)),
                  pl.BlockSpec((1, 1, 6 * D), lambda i, j: (rowmap(i), 0, 0)),
                  pl.BlockSpec((1, D), lambda i, j: (0, 0))],
        out_specs=pl.BlockSpec((tk, D), lambda i, j: (i, 0)),
        out_shape=jax.ShapeDtypeStruct((n, D), F32),
        scratch_shapes=[pltpu.VMEM((D, tk), F32), pltpu.VMEM((ex, tk), BF16)],
        compiler_params=_cp("arbitrary", "arbitrary"),
        name="peer_dense",
    )(ht, u, vt, nsel, e1, rank, e2, x, mod, fg)


def _peer(x, mod, norm_g, wq, sk, u, v, fg, np_rows, seq_rows, final_norm):
    ht, nsel, e1, rank, e2 = _route(x, mod, norm_g, wq.astype(BF16), sk.astype(BF16), np_rows, seq_rows)
    return _peer_dense(x, mod, ht, u.astype(BF16), v.T.astype(BF16), nsel, e1, rank, e2, fg,
                       np_rows, seq_rows, final_norm)


def _s5_in_body(x_ref, mod_ref, g_ref, w_ref, o_ref):
    mod = mod_ref[0]
    h = _norm_mod(x_ref[...], g_ref[...], mod[:, D:2 * D], mod[:, 0:D]).astype(BF16)
    o_ref[...] = jnp.dot(h, w_ref[...], preferred_element_type=F32).astype(BF16)


def _s5_in(x, mod, norm_g, w, np_rows, seq_rows):
    n = x.shape[0]
    tm = ROW_TILE
    rowmap = _mod_row_map(np_rows // tm, seq_rows // tm)
    return pl.pallas_call(
        _s5_in_body,
        grid=(n // tm,),
        in_specs=[pl.BlockSpec((tm, D), lambda i: (i, 0)),
                  pl.BlockSpec((1, 1, 6 * D), lambda i: (rowmap(i), 0, 0)),
                  pl.BlockSpec((1, D), lambda i: (0, 0)),
                  pl.BlockSpec((D, D), lambda i: (0, 0))],
        out_specs=pl.BlockSpec((tm, D), lambda i: (i, 0)),
        out_shape=jax.ShapeDtypeStruct((n, D), BF16),
        compiler_params=_cp("arbitrary"),
        name="s5_in",
    )(x, mod, norm_g, w)


def _s5_core_body(u_ref, m_ref, win_ref, cout_ref, pw_ref, *rest, kseq, has_init, emit_states):
    rest = list(rest)
    h0_ref = rest.pop(0) if has_init else None
    y_ref = rest.pop(0)
    sf_ref = rest.pop(0) if emit_states else None
    u = u_ref[...]
    kb = u.shape[0]
    y = jnp.dot(u, m_ref[...], preferred_element_type=F32)
    row = lax.broadcasted_iota(jnp.int32, (kb, 128), 0)
    kk = row & (kseq - 1)

    def cmul(a1, a2, x):
        return a1 * x + a2 * pltpu.roll(x, 64, 1)

    for d in range(2):
        s = jnp.dot(u, win_ref[d], preferred_element_type=F32)
        edge = (kk == 0) if d == 0 else (kk == kseq - 1)
        if has_init:
            h0 = h0_ref[d]
            s = s + jnp.where(edge, cmul(pw_ref[d, 0, 0:1], pw_ref[d, 0, 1:2], h0), 0.0)
        step, lvl = 1, 0
        while step < kseq:
            if d == 0:
                sh = pltpu.roll(s, step, 0)
                ok = kk >= step
            else:
                sh = pltpu.roll(s, kb - step, 0)
                ok = kk < kseq - step
            s = s + jnp.where(ok, cmul(pw_ref[d, lvl, 0:1], pw_ref[d, lvl, 1:2], sh), 0.0)
            step, lvl = step * 2, lvl + 1
        if emit_states:
            sf_ref[d] = s
        hin = pltpu.roll(s, 1, 0) if d == 0 else pltpu.roll(s, kb - 1, 0)
        if has_init:
            hin = jnp.where(edge, h0, hin)
        else:
            hin = jnp.where(edge, 0.0, hin)
        y = y + jnp.dot(hin.astype(BF16), cout_ref[d], preferred_element_type=F32)
    y_ref[...] = _gelu(y).astype(BF16)


def _s5_core(ug, mats, row0, rows, kseq, h0=None, emit_states=False):
    m, win, cout, pw = mats
    kb = 128
    b0 = row0 // kb
    nlv = pw.shape[2]
    in_specs = [pl.BlockSpec((None, kb, 256), lambda g, i: (g, b0 + i, 0)),
                pl.BlockSpec((None, 256, 256), lambda g, i: (g, 0, 0)),
                pl.BlockSpec((None, 2, 256, 128), lambda g, i: (g, 0, 0, 0)),
                pl.BlockSpec((None, 2, 128, 256), lambda g, i: (g, 0, 0, 0)),
                pl.BlockSpec((None, 2, nlv, 2, 128), lambda g, i: (g, 0, 0, 0, 0))]
    args = [ug, m, win, cout, pw]
    if h0 is not None:
        assert kseq == kb
        in_specs.append(pl.BlockSpec((None, None, 2, 1, 128), lambda g, i: (g, i, 0, 0, 0)))
        args.append(h0)
    out_specs = [pl.BlockSpec((None, kb, 256), lambda g, i: (g, i, 0))]
    out_shape = [jax.ShapeDtypeStruct((S5_GROUPS, rows, 256), BF16)]
    if emit_states:
        out_specs.append(pl.BlockSpec((None, 2, kb, 128), lambda g, i: (g, 0, i, 0)))
        out_shape.append(jax.ShapeDtypeStruct((S5_GROUPS, 2, rows, 128), F32))
    return pl.pallas_call(
        functools.partial(_s5_core_body, kseq=kseq, has_init=h0 is not None, emit_states=emit_states),
        grid=(S5_GROUPS, rows // kb),
        in_specs=in_specs,
        out_specs=out_specs,
        out_shape=out_shape,
        compiler_params=_cp("arbitrary", "arbitrary"),
        name="s5_core_ctx" if h0 is not None else "s5_core",
    )(*args)


def _s5_out_body(x_ref, mod_ref, y_ref, w_ref, o_ref):
    zz = jnp.dot(y_ref[...], w_ref[...], preferred_element_type=F32)
    za = zz[:, 0:D]
    zb = zz[:, D:2 * D]
    o_ref[...] = x_ref[...] + mod_ref[0][:, 2 * D:3 * D] * (za / (1.0 + jnp.exp(-zb)))


def _s5_out(x, mod, y, w, np_rows, seq_rows):
    n = x.shape[0]
    tm = ROW_TILE
    rowmap = _mod_row_map(np_rows // tm, seq_rows // tm)
    return pl.pallas_call(
        _s5_out_body,
        grid=(n // tm,),
        in_specs=[pl.BlockSpec((tm, D), lambda i: (i, 0)),
                  pl.BlockSpec((1, 1, 6 * D), lambda i: (rowmap(i), 0, 0)),
                  pl.BlockSpec((tm, D), lambda i: (i, 0)),
                  pl.BlockSpec((D, 2 * D), lambda i: (0, 0))],
        out_specs=pl.BlockSpec((tm, D), lambda i: (i, 0)),
        out_shape=jax.ShapeDtypeStruct((n, D), F32),
        compiler_params=_cp("arbitrary"),
        name="s5_out",
    )(x, mod, y, w)


def _s5_matrices(a_re, a_im, log_dt, b_re, b_im, c_re, c_im, d_skip):
    cs = S5_CHUNK
    dt = jnp.exp(log_dt)[..., None]
    lr, li = a_re * dt, a_im * dt

    def lam_pow(tau):
        mag = jnp.exp(lr[..., None] * tau)
        ang = li[..., None] * tau
        return mag * jnp.cos(ang), mag * jnp.sin(ang)

    l1r, l1i = jnp.exp(lr) * jnp.cos(li), jnp.exp(lr) * jnp.sin(li)
    den = a_re * a_re + a_im * a_im
    cr = ((l1r - 1.0) * a_re + l1i * a_im) / den
    ci = (l1i * a_re - (l1r - 1.0) * a_im) / den
    bbr = cr[..., None] * b_re - ci[..., None] * b_im
    bbi = cr[..., None] * b_im + ci[..., None] * b_re

    tau = jnp.arange(cs + 1, dtype=F32)
    pr, pi = lam_pow(tau)
    clr = c_re[..., None] * pr[:, :, None] - c_im[..., None] * pi[:, :, None]
    cli = c_re[..., None] * pi[:, :, None] + c_im[..., None] * pr[:, :, None]
    kern = (jnp.einsum('dgcpt,dgpe->dgtce', clr[..., :cs], bbr, precision=HI)
            - jnp.einsum('dgcpt,dgpe->dgtce', cli[..., :cs], bbi, precision=HI))
    s_idx = jnp.arange(cs)[:, None]
    t_idx = jnp.arange(cs)[None, :]
    lag_f = jnp.clip(t_idx - s_idx, 0, cs - 1)
    lag_b = jnp.clip(s_idx - t_idx, 0, cs - 1)
    mf = jnp.where((s_idx <= t_idx)[None, :, :, None, None], kern[0][:, lag_f], 0.0)
    mb = jnp.where((s_idx >= t_idx)[None, :, :, None, None], kern[1][:, lag_b], 0.0)
    eye_t = (s_idx == t_idx)[None, :, :, None, None]
    eye_c = jnp.eye(S5_CH, dtype=F32)[None, None, None]
    dsk = d_skip.reshape(S5_GROUPS, 1, 1, S5_CH, 1)
    m = mf + mb + jnp.where(eye_t, eye_c * dsk, 0.0)
    m = m.transpose(0, 1, 4, 2, 3).reshape(S5_GROUPS, cs * S5_CH, cs * S5_CH)

    def win_dir(d, powers):
        wr = pr[d][..., powers][:, :, :, None] * bbr[d][:, :, None, :] - pi[d][..., powers][:, :, :, None] * bbi[d][:, :, None, :]
        wi = pr[d][..., powers][:, :, :, None] * bbi[d][:, :, None, :] + pi[d][..., powers][:, :, :, None] * bbr[d][:, :, None, :]
        w = jnp.concatenate([wr, wi], axis=1)
        return w.transpose(0, 2, 3, 1).reshape(S5_GROUPS, cs * S5_CH, 2 * S5_P)

    win = jnp.stack([win_dir(0, jnp.arange(cs - 1, -1, -1)), win_dir(1, jnp.arange(cs))], axis=1)

    def cout_dir(d, powers):
        zr = clr[d][..., powers]
        zi = cli[d][..., powers]
        z = jnp.concatenate([zr, -zi], axis=2)
        return z.transpose(0, 2, 3, 1).reshape(S5_GROUPS, 2 * S5_P, cs * S5_CH)

    cout = jnp.stack([cout_dir(0, jnp.arange(1, cs + 1)), cout_dir(1, jnp.arange(cs, 0, -1))], axis=1)

    lv = cs * (2.0 ** jnp.arange(7, dtype=F32))
    qr, qi = lam_pow(lv)
    a1 = jnp.concatenate([qr, qr], axis=2)
    a2 = jnp.concatenate([-qi, qi], axis=2)
    pw = jnp.stack([a1, a2], axis=-1).transpose(1, 0, 3, 4, 2)
    return m.astype(BF16), win.astype(BF16), cout.astype(BF16), pw


def _rope_tables(dec_seq):
    rows = dec_seq // GRID_W
    row_id = jnp.repeat(jnp.arange(rows), GRID_W).astype(F32)
    col_id = jnp.tile(jnp.arange(GRID_W), rows).astype(F32)
    quarter = DA_DIM // 4
    inv = ROPE_BASE ** (-jnp.arange(quarter, dtype=F32) / quarter)
    ang_r = row_id[:, None] * inv
    ang_c = col_id[:, None] * inv
    ang = jnp.concatenate([ang_r, ang_r, ang_c, ang_c], axis=-1)
    ang = jnp.tile(ang, (1, 2 * DA_HEADS))
    pad = jnp.zeros((ROW_TILE, ang.shape[1]), F32)
    ang = jnp.concatenate([pad, ang], axis=0)
    return jnp.cos(ang), jnp.sin(ang)


def kernel(x_prompt, x_sample, cache_da_k, cache_da_v, state_gla, state_s5, c, c_ctx,
           ada_w, ada_b, norm_g, final_norm_g, mix0_w_in, mix0_w_out, da_lam, da_subln_g,
           gla_gate_up, gla_gate_bias, gla_norm_g, s5_w_in, s5_a_re, s5_a_im, s5_log_dt,
           s5_b_re, s5_b_im, s5_c_re, s5_c_im, s5_d, s5_w_out,
           peer_w_query, peer_sub_keys, peer_u, peer_v):
    bp, lp, _ = x_prompt.shape
    bs, ls, _ = x_sample.shape
    npr, nsr = bp * lp, bs * ls
    assert npr % ls == 0 and ls % ROW_TILE == 0 and lp % ROW_TILE == 0
    x = jnp.concatenate([x_prompt.reshape(npr, D), x_sample.reshape(nsr, D)], axis=0)

    cond = jnp.concatenate([c_ctx[None], c, jnp.zeros((15 - bs, D), F32)], axis=0)
    mod_all = _ada_mod(cond, ada_w, ada_b)
    mods = [mod_all[l].reshape(16, 1, 6 * D) for l in range(mod_all.shape[0])]
    fg = final_norm_g.reshape(1, D)

    lam_init = 0.8 - 0.6 * math.exp(-0.3 * 0)
    cos_t, sin_t = _rope_tables(ls)
    w_in = jnp.pad(mix0_w_in[0], ((0, 0), (0, IN0_PAD - mix0_w_in.shape[2]))).astype(BF16)
    q, kb, vb, k32, v32, gqk, gv, gr, low = _inproj0(
        x, mods[0], norm_g[0, 0].reshape(1, D), w_in, cos_t, sin_t, npr, ls)
    sub_g = da_subln_g[0].reshape(1, 2 * DA_DIM)
    past = cache_da_k.shape[2]
    o_da = jnp.concatenate([
        _diff_attention(q, kb, vb, da_lam[0], sub_g, lam_init, 0, bp, lp),
        _diff_attention(q, kb, vb, da_lam[0], sub_g, lam_init, npr, bs, ls,
                        cache=(cache_da_k[:, 0].reshape(bs, past, 512), cache_da_v[:, 0].reshape(bs, past, 512))),
    ], axis=0)
    hk = GLA_HEADS * GLA_DK
    gup = jnp.zeros((2, 128, hk), F32)
    for d in range(2):
        gup = gup.at[d, d * GLA_RANK:(d + 1) * GLA_RANK].set(gla_gate_up[0, d])
    gbias = gla_gate_bias[0].reshape(2, 1, hk)
    ng = gla_norm_g[0].reshape(1, GLA_DV)
    s0 = state_gla[:, 0].transpose(0, 1, 4, 2, 3).reshape(bs, 2, GLA_DV, hk)
    og_p, st_p = _gla(gqk, gv, gr, low, gup, gbias, ng, 0, bp, lp)
    og_s, _ = _gla(gqk, gv, gr, low, gup, gbias, ng, npr, bs, ls, s0=s0)
    o_g = jnp.concatenate([og_p, og_s], axis=0)
    x = _outproj0(x, mods[0], o_da, o_g, mix0_w_out[0].astype(BF16), npr, ls)
    x = _peer(x, mods[0], norm_g[0, 1].reshape(1, D), peer_w_query[0],
              peer_sub_keys[0], peer_u[0], peer_v[0], fg, npr, ls, final_norm=False)

    u = _s5_in(x, mods[1], norm_g[1, 0].reshape(1, D), s5_w_in[0].astype(BF16), npr, ls)
    n = npr + nsr
    cs = S5_CHUNK
    ug = u.reshape(n // cs, cs, S5_GROUPS, S5_CH).transpose(2, 0, 1, 3).reshape(S5_GROUPS, n // cs, cs * S5_CH)
    mats = _s5_matrices(s5_a_re[0], s5_a_im[0], s5_log_dt[0], s5_b_re[0], s5_b_im[0],
                        s5_c_re[0], s5_c_im[0], s5_d[0])
    h0 = state_s5[:, 0]
    h0 = h0.transpose(2, 0, 1, 4, 3).reshape(S5_GROUPS, bs, 2, 1, 2 * S5_P)
    y_p, sf = _s5_core(ug, mats, 0, npr // cs, lp // cs, emit_states=True)
    (y_s,) = _s5_core(ug, mats, npr // cs, nsr // cs, ls // cs, h0=h0)
    yg = jnp.concatenate([y_p, y_s], axis=1)
    y = yg.reshape(S5_GROUPS, n // cs, cs, S5_CH).transpose(1, 2, 0, 3).reshape(n, D)
    x = _s5_out(x, mods[1], y, s5_w_out[0].astype(BF16), npr, ls)
    x = _peer(x, mods[1], norm_g[1, 1].reshape(1, D), peer_w_query[1],
              peer_sub_keys[1], peer_u[1], peer_v[1], fg, npr, ls, final_norm=True)

    y_prompt = x[:npr].reshape(bp, lp, D)
    y_sample = x[npr:].reshape(bs, ls, D)
    new_k = k32[:npr].reshape(bp, 1, lp, 2 * DA_HEADS, DA_DIM)
    new_v = v32[:npr].reshape(bp, 1, lp, DA_HEADS, 2 * DA_DIM)
    new_gla = st_p.reshape(bp, 2, GLA_DV, GLA_HEADS, GLA_DK).transpose(0, 1, 3, 4, 2)[:, None]
    kc = lp // cs
    sf = sf.reshape(S5_GROUPS, 2, bp, kc, 2, S5_P)
    fin = jnp.stack([sf[:, 0, :, kc - 1], sf[:, 1, :, 0]], axis=1)
    new_s5 = fin.transpose(2, 1, 0, 4, 3)[:, None]
    return (y_prompt, y_sample, new_k, new_v, new_gla, new_s5)
```

```python
import functools
import math

import jax
import jax.numpy as jnp
from jax import lax
from jax.experimental import pallas as pl
from jax.experimental.pallas import tpu as pltpu

F32 = jnp.float32
BF16 = jnp.bfloat16
HI = lax.Precision.HIGHEST

D = 1024
EPS = 1e-6
ROPE_BASE = 10000.0
GRID_W = 64
DA_HEADS = 4
DA_DIM = 64
GLA_HEADS = 4
GLA_DK = 64
GLA_DV = 128
GLA_RANK = 16
GLA_TAU = 16.0
GLA_CHUNK = 64
S5_CH = 16
S5_GROUPS = D // S5_CH
S5_P = 64
S5_CHUNK = 16
S5_GROUP_BLOCK = 4
PEER_HEADS = 8
PEER_NKEYS = 128
PEER_TOPK = 16
IN0_PAD = 3200

ROW_TILE = 256
PEER_TOK = 512
PEER_EXP = 1024
PEER_SUB = 256
VMEM_LIMIT = 48 * 1024 * 1024

NT = (((1,), (1,)), ((), ()))
TN = (((0,), (0,)), ((), ()))


def _cp(*sem, flags=None):
    return pltpu.CompilerParams(dimension_semantics=sem, vmem_limit_bytes=VMEM_LIMIT, flags=flags)


def _norm_mod(x, g, sc, sh):
    ms = jnp.mean(x * x, axis=-1, keepdims=True)
    return x * lax.rsqrt(ms + EPS) * g * (1.0 + sc) + sh


_GELU_C2 = -2.0 * 0.7978845608028654 * 1.4426950408889634
_GELU_C1 = _GELU_C2 * 0.044715


def _gelu(x):
    return x / (1.0 + jnp.exp2(x * (_GELU_C1 * (x * x) + _GELU_C2)))


def _mod_row_map(np_blocks, seq_blocks):
    def f(i):
        return jnp.where(i < np_blocks, 0, 1 + (i - np_blocks) // seq_blocks)
    return f


def _ada_body(c_ref, w_ref, b_ref, o_ref):
    c = c_ref[...]
    s = c / (1.0 + jnp.exp(-c))
    o_ref[0] = jnp.dot(s, w_ref[0], precision=HI, preferred_element_type=F32) + b_ref[0]


def _ada_mod(cond, ada_w, ada_b):
    depth, _, n = ada_w.shape
    rows = cond.shape[0]
    tn = 1536
    return pl.pallas_call(
        _ada_body,
        grid=(depth, n // tn),
        in_specs=[pl.BlockSpec((rows, D), lambda l, j: (0, 0)),
                  pl.BlockSpec((1, D, tn), lambda l, j: (l, 0, j)),
                  pl.BlockSpec((1, 1, tn), lambda l, j: (l, 0, j))],
        out_specs=pl.BlockSpec((1, rows, tn), lambda l, j: (l, 0, j)),
        out_shape=jax.ShapeDtypeStruct((depth, rows, n), F32),
        compiler_params=_cp("arbitrary", "arbitrary"),
        name="ada_mod",
    )(cond, ada_w, ada_b.reshape(depth, 1, n))


def _inproj0_body(x_ref, mod_ref, g_ref, w_ref, cos_ref, sin_ref,
                  q_o, kb_o, vb_o, k32_o, v32_o, gqk_o, gv_o, gr_o, low_o):
    mod = mod_ref[0]
    h = _norm_mod(x_ref[...], g_ref[...], mod[:, D:2 * D], mod[:, 0:D]).astype(BF16)
    y = jnp.dot(h, w_ref[...], preferred_element_type=F32)
    cos = cos_ref[...]
    sin = sin_ref[...]
    lane = lax.broadcasted_iota(jnp.int32, cos.shape, 1)
    first = (lane & 31) < 16

    def rope(t):
        rot = jnp.where(first, -pltpu.roll(t, 512 - 16, 1), pltpu.roll(t, 16, 1))
        return t * cos + rot * sin

    q = rope(y[:, 0:512])
    k = rope(y[:, 512:1024])
    v = y[:, 1024:1536]
    q_o[...] = q.astype(BF16)
    kb_o[...] = k.astype(BF16)
    vb_o[...] = v.astype(BF16)
    k32_o[...] = k
    v32_o[...] = v
    gqk_o[...] = y[:, 1536:2048].astype(BF16)
    gv_o[...] = y[:, 2048:2560].astype(BF16)
    gr_o[...] = y[:, 2560:3072].astype(BF16)
    low_o[...] = y[:, 3072:3200]


def _inproj0(x, mod, norm_g, w, cos_t, sin_t, np_rows, seq_rows):
    n = x.shape[0]
    tm = ROW_TILE
    npb, sqb = np_rows // tm, seq_rows // tm
    rowmap = _mod_row_map(npb, sqb)
    posmap = lambda i: (jnp.where(i < npb, 0, 1 + (i - npb) % sqb), 0)
    tok = lambda w_: pl.BlockSpec((tm, w_), lambda i: (i, 0))
    outs = [(512, BF16), (512, BF16), (512, BF16), (512, F32), (512, F32),
            (512, BF16), (512, BF16), (512, BF16), (128, F32)]
    return pl.pallas_call(
        _inproj0_body,
        grid=(n // tm,),
        in_specs=[tok(D),
                  pl.BlockSpec((1, 1, 6 * D), lambda i: (rowmap(i), 0, 0)),
                  pl.BlockSpec((1, D), lambda i: (0, 0)),
                  pl.BlockSpec((D, IN0_PAD), lambda i: (0, 0)),
                  pl.BlockSpec((tm, 512), posmap),
                  pl.BlockSpec((tm, 512), posmap)],
        out_specs=[tok(w_) for w_, _ in outs],
        out_shape=[jax.ShapeDtypeStruct((n, w_), dt) for w_, dt in outs],
        compiler_params=_cp("arbitrary"),
        name="inproj0",
    )(x, mod, norm_g, w, cos_t, sin_t)


def _attn_body(lam_ref, sg_ref, q_ref, k_ref, v_ref, *rest, lam_init, has_cache):
    if has_cache:
        kc_ref, vc_ref, o_ref = rest
    else:
        (o_ref,) = rest
    lf = lam_ref[...]
    lam = (jnp.exp(jnp.sum(lf[0:1] * lf[1:2], axis=-1, keepdims=True))
           - jnp.exp(jnp.sum(lf[2:3] * lf[3:4], axis=-1, keepdims=True)) + lam_init)
    q = q_ref[...] * (DA_DIM ** -0.5)
    lane = lax.broadcasted_iota(jnp.int32, q.shape, 1)
    zero = jnp.zeros_like(q)
    qm = (jnp.where(lane < DA_DIM, q, zero), jnp.where(lane >= DA_DIM, q, zero))
    k = k_ref[...]
    s = [lax.dot_general(qm[m], k, NT, preferred_element_type=F32) for m in range(2)]
    mx = [jnp.max(s[m], axis=-1, keepdims=True) for m in range(2)]
    if has_cache:
        kc = kc_ref[...].astype(BF16)
        sc = [lax.dot_general(qm[m], kc, NT, preferred_element_type=F32) for m in range(2)]
        mx = [jnp.maximum(mx[m], jnp.max(sc[m], axis=-1, keepdims=True)) for m in range(2)]
    e = [jnp.exp(s[m] - mx[m]) for m in range(2)]
    z = [jnp.sum(e[m], axis=-1, keepdims=True) for m in range(2)]
    if has_cache:
        ec = [jnp.exp(sc[m] - mx[m]) for m in range(2)]
        z = [z[m] + jnp.sum(ec[m], axis=-1, keepdims=True) for m in range(2)]
    w0 = 1.0 / z[0]
    w1 = lam / z[1]
    o = jnp.dot((e[0] * w0 - e[1] * w1).astype(BF16), v_ref[...], preferred_element_type=F32)
    if has_cache:
        o = o + jnp.dot((ec[0] * w0 - ec[1] * w1).astype(BF16), vc_ref[...].astype(BF16),
                        preferred_element_type=F32)
    ms = jnp.mean(o * o, axis=-1, keepdims=True)
    o_ref[...] = (o * lax.rsqrt(ms + EPS) * sg_ref[...] * (1.0 - lam_init)).astype(o_ref.dtype)


def _diff_attention(q, k, v, da_lam, subln_g, lam_init, row0, batch, seq, cache=None):
    tq = 256
    qb0, kb0 = row0 // tq, row0 // seq
    in_specs = [pl.BlockSpec((4, DA_DIM), lambda b, h, i: (0, 0)),
                pl.BlockSpec((1, 128), lambda b, h, i: (0, 0)),
                pl.BlockSpec((tq, 128), lambda b, h, i: (qb0 + b * (seq // tq) + i, h)),
                pl.BlockSpec((seq, 128), lambda b, h, i: (kb0 + b, h)),
                pl.BlockSpec((seq, 128), lambda b, h, i: (kb0 + b, h))]
    args = [da_lam, subln_g, q, k, v]
    if cache is not None:
        past = cache[0].shape[1]
        in_specs += [pl.BlockSpec((None, past, 128), lambda b, h, i: (b, 0, h))] * 2
        args += list(cache)
    return pl.pallas_call(
        functools.partial(_attn_body, lam_init=lam_init, has_cache=cache is not None),
        grid=(batch, DA_HEADS, seq // tq),
        in_specs=in_specs,
        out_specs=pl.BlockSpec((tq, 128), lambda b, h, i: (b * (seq // tq) + i, h)),
        out_shape=jax.ShapeDtypeStruct((batch * seq, 512), BF16),
        compiler_params=_cp("arbitrary", "arbitrary", "arbitrary"),
        name="diff_attn_ctx" if cache is not None else "diff_attn",
    )(*args)


def _gla_body(qk_ref, v_ref, r_ref, low_ref, gup_ref, gb_ref, ng_ref, *rest, seq, has_init):
    if has_init:
        s0_ref, o_ref, sf_ref, st_scr, of_scr, ob_scr, la_scr = rest
    else:
        o_ref, sf_ref, st_scr, of_scr, ob_scr, la_scr = rest
    c = GLA_CHUNK
    n = seq // c
    hk = GLA_HEADS * GLA_DK
    low = low_ref[...]
    for d in range(2):
        z = jnp.dot(low, gup_ref[d], precision=HI, preferred_element_type=F32) + gb_ref[d]
        la_scr[d] = (jnp.minimum(z, 0.0) - jnp.log(1.0 + jnp.exp(-jnp.abs(z)))) * (1.0 / GLA_TAU)
    if has_init:
        st_scr[...] = s0_ref[0]
    else:
        st_scr[...] = jnp.zeros(st_scr.shape, F32)
    row = lax.broadcasted_iota(jnp.int32, (c, c), 0)
    col = lax.broadcasted_iota(jnp.int32, (c, c), 1)
    keep = (col <= row, col >= row)
    lane = lax.broadcasted_iota(jnp.int32, (c, 128), 1)
    low_half = lane < GLA_DK

    def chunk(d, r0):
        rows = pl.ds(r0, c)
        qk = qk_ref[rows, :].astype(F32)
        q = qk[:, 0:hk] * (GLA_DK ** -0.5)
        k = qk[:, hk:2 * hk]
        v = v_ref[rows, :]
        g = la_scr[d, rows, :]
        b = jnp.dot(keep[d].astype(F32), g, precision=HI, preferred_element_type=F32)
        b_tot = b[c - 1:c] if d == 0 else b[0:1]
        q_dec = q * jnp.exp(b)
        k_inv = k * jnp.exp(-b)
        k_end = k * jnp.exp(b_tot - b)
        st = st_scr[d]
        st_b = st.astype(BF16)
        outs, news = [], []
        for pair in range(GLA_HEADS // 2):
            ps = slice(pair * 128, (pair + 1) * 128)
            kin = k_inv[:, ps].astype(BF16)
            new = None
            for sub in range(2):
                hh = pair * 2 + sub
                vs = slice(hh * GLA_DV, (hh + 1) * GLA_DV)
                sel = low_half if sub == 0 else jnp.logical_not(low_half)
                qd = jnp.where(sel, q_dec[:, ps], 0.0).astype(BF16)
                ke = jnp.where(sel, k_end[:, ps], 0.0).astype(BF16)
                inter = lax.dot_general(qd, st_b[:, ps], NT, preferred_element_type=F32)
                att = lax.dot_general(qd, kin, NT, preferred_element_type=F32)
                att = jnp.where(keep[d], att, 0.0).astype(BF16)
                outs.append(inter + jnp.dot(att, v[:, vs], preferred_element_type=F32))
                upd = lax.dot_general(v[:, vs], ke, TN, preferred_element_type=F32)
                new = upd if new is None else new + upd
            news.append(new)
        st_scr[d] = st * jnp.exp(b_tot) + jnp.concatenate(news, axis=1)
        return jnp.concatenate(outs, axis=1)

    def step(i, carry):
        rf = pl.multiple_of(i * c, c)
        of_scr[pl.ds(rf, c), :] = chunk(0, rf)
        rb = pl.multiple_of((n - 1 - i) * c, c)
        ob_scr[pl.ds(rb, c), :] = chunk(1, rb)
        return carry

    lax.fori_loop(0, n, step, 0, unroll=2)
    sf_ref[0] = st_scr[...]

    fin = 256

    def finish(i, carry):
        rows = pl.ds(pl.multiple_of(i * fin, fin), fin)
        o = of_scr[rows, :] + ob_scr[rows, :]
        r = r_ref[rows, :].astype(F32)
        gate = r / (1.0 + jnp.exp(-r))
        ng = ng_ref[...]
        for hh in range(GLA_HEADS):
            vs = slice(hh * GLA_DV, (hh + 1) * GLA_DV)
            oh = o[:, vs]
            ms = jnp.mean(oh * oh, axis=-1, keepdims=True)
            o_ref[rows, vs] = (oh * lax.rsqrt(ms + EPS) * ng * gate[:, vs]).astype(BF16)
        return carry

    lax.fori_loop(0, seq // fin, finish, 0)


def _gla(gqk, gv, gr, low, gup, gb, ng, row0, batch, seq, s0=None):
    b0 = row0 // seq
    tokmap = lambda b: (b0 + b, 0)
    hk = GLA_HEADS * GLA_DK
    in_specs = [pl.BlockSpec((seq, 512), tokmap), pl.BlockSpec((seq, 512), tokmap),
                pl.BlockSpec((seq, 512), tokmap), pl.BlockSpec((seq, 128), tokmap),
                pl.BlockSpec((2, 128, hk), lambda b: (0, 0, 0)),
                pl.BlockSpec((2, 1, hk), lambda b: (0, 0, 0)),
                pl.BlockSpec((1, GLA_DV), lambda b: (0, 0))]
    args = [gqk, gv, gr, low, gup, gb, ng]
    if s0 is not None:
        in_specs.append(pl.BlockSpec((1, 2, GLA_DV, hk), lambda b: (b, 0, 0, 0)))
        args.append(s0)
    return pl.pallas_call(
        functools.partial(_gla_body, seq=seq, has_init=s0 is not None),
        grid=(batch,),
        in_specs=in_specs,
        out_specs=[pl.BlockSpec((seq, 512), lambda b: (b, 0)),
                   pl.BlockSpec((1, 2, GLA_DV, hk), lambda b: (b, 0, 0, 0))],
        out_shape=[jax.ShapeDtypeStruct((batch * seq, 512), BF16),
                   jax.ShapeDtypeStruct((batch, 2, GLA_DV, hk), F32)],
        scratch_shapes=[pltpu.VMEM((2, GLA_DV, hk), F32),
                        pltpu.VMEM((seq, 512), F32), pltpu.VMEM((seq, 512), F32),
                        pltpu.VMEM((2, seq, hk), F32)],
        compiler_params=_cp("arbitrary"),
        name="gla_ctx" if s0 is not None else "gla",
    )(*args)


def _outproj0_body(x_ref, mod_ref, oda_ref, og_ref, w_ref, o_ref):
    m = (jnp.dot(oda_ref[...], w_ref[0:512, :], preferred_element_type=F32)
         + jnp.dot(og_ref[...], w_ref[512:1024, :], preferred_element_type=F32))
    o_ref[...] = x_ref[...] + mod_ref[0][:, 2 * D:3 * D] * m


def _outproj0(x, mod, oda, og, w, np_rows, seq_rows):
    n = x.shape[0]
    tm = ROW_TILE
    rowmap = _mod_row_map(np_rows // tm, seq_rows // tm)
    return pl.pallas_call(
        _outproj0_body,
        grid=(n // tm,),
        in_specs=[pl.BlockSpec((tm, D), lambda i: (i, 0)),
                  pl.BlockSpec((1, 1, 6 * D), lambda i: (rowmap(i), 0, 0)),
                  pl.BlockSpec((tm, 512), lambda i: (i, 0)),
                  pl.BlockSpec((tm, 512), lambda i: (i, 0)),
                  pl.BlockSpec((D, D), lambda i: (0, 0))],
        out_specs=pl.BlockSpec((tm, D), lambda i: (i, 0)),
        out_shape=jax.ShapeDtypeStruct((n, D), F32),
        compiler_params=_cp("arbitrary"),
        name="outproj0",
    )(x, mod, oda, og, w)


def _route_body(x_ref, mod_ref, g_ref, wq_ref, sk_ref, ht_o, nsel_o, e1_o, rank_o, e2_o, top_scr):
    mod = mod_ref[0]
    h = _norm_mod(x_ref[...], g_ref[...], mod[:, 4 * D:5 * D], mod[:, 3 * D:4 * D])
    ht_o[...] = h.T.astype(BF16)
    q = jnp.dot(h.astype(BF16), wq_ref[...], preferred_element_type=F32).astype(BF16)
    tm = q.shape[0]
    neg = -jnp.inf
    k = PEER_TOPK
    row8 = lax.broadcasted_iota(jnp.int32, (8, tm), 0)
    for hh in range(PEER_HEADS):
        st = []
        rank = jnp.full((PEER_NKEYS, tm), float(k), F32)
        for t in range(2):
            c0 = (hh * 2 + t) * PEER_NKEYS
            s = lax.dot_general(sk_ref[hh, t], q[:, c0:c0 + PEER_NKEYS], NT,
                                preferred_element_type=F32)
            st.append(s)
            cur = s
            for j in range(k):
                m = jnp.max(cur, axis=0, keepdims=True)
                top_scr[t, j:j + 1, :] = m
                hit = cur == m
                if t == 1:
                    rank = jnp.where(hit, float(j), rank)
                cur = jnp.where(hit, neg, cur)
        t1 = top_scr[0]
        t2 = top_scr[1]
        slabs = [t1[0:1] + t2, t1[1:2] + t2[0:8], t1[2:3] + t2[0:8], t1[3:4] + t2[0:8],
                 t2[0:1] + t1[8:16]]
        for j in range(3):
            slabs.append(jnp.where(row8 >= 4, t2[j:j + 1] + t1[0:8], neg))
        cand = jnp.concatenate(slabs, axis=0)
        top = t1[0:1] + t2[0:1]
        zsum = jnp.zeros_like(top)
        kth = top
        for j in range(k):
            kth = jnp.max(cand, axis=0, keepdims=True)
            zsum = zsum + jnp.exp(kth - top)
            cand = jnp.where(cand == kth, neg, cand)
        nsel = jnp.zeros((PEER_NKEYS, tm), F32)
        for j in range(k):
            nsel = jnp.where(st[0] + t2[j:j + 1] >= kth, float(j + 1), nsel)
        hs = slice(hh * PEER_NKEYS, (hh + 1) * PEER_NKEYS)
        nsel_o[hs, :] = nsel
        e1_o[hs, :] = jnp.exp(st[0] - t1[0:1]) * (1.0 / zsum)
        rank_o[hs, :] = rank.astype(BF16)
        e2_o[hs, :] = jnp.exp(st[1] - t2[0:1]).astype(BF16)


def _route(x, mod, norm_g, wq, sk, np_rows, seq_rows):
    n = x.shape[0]
    tm = ROW_TILE
    rowmap = _mod_row_map(np_rows // tm, seq_rows // tm)
    rt = lambda dt: jax.ShapeDtypeStruct((PEER_HEADS * PEER_NKEYS, n), dt)
    rspec = pl.BlockSpec((PEER_HEADS * PEER_NKEYS, tm), lambda i: (0, i))
    return pl.pallas_call(
        _route_body,
        grid=(n // tm,),
        in_specs=[pl.BlockSpec((tm, D), lambda i: (i, 0)),
                  pl.BlockSpec((1, 1, 6 * D), lambda i: (rowmap(i), 0, 0)),
                  pl.BlockSpec((1, D), lambda i: (0, 0)),
                  pl.BlockSpec((D, 2 * PEER_HEADS * PEER_NKEYS), lambda i: (0, 0)),
                  pl.BlockSpec((PEER_HEADS, 2, PEER_NKEYS, 128), lambda i: (0, 0, 0, 0))],
        out_specs=[pl.BlockSpec((D, tm), lambda i: (0, i)), rspec, rspec, rspec, rspec],
        out_shape=[jax.ShapeDtypeStruct((D, n), BF16), rt(F32), rt(F32), rt(BF16), rt(BF16)],
        scratch_shapes=[pltpu.VMEM((2, PEER_TOPK, tm), F32)],
        compiler_params=_cp("arbitrary"),
        name="peer_route",
    )(x, mod, norm_g, wq, sk)


def _peer_body(ht_ref, u_ref, vt_ref, nsel_ref, e1_ref, rank_ref, e2_ref, x_ref, mod_ref, fg_ref,
               o_ref, acc_ref, p_ref, *, final_norm):
    j = pl.program_id(1)

    @pl.when(j == 0)
    def _():
        acc_ref[...] = jnp.zeros(acc_ref.shape, F32)

    nk = PEER_NKEYS
    tok = ht_ref.shape[1]
    na = PEER_EXP // nk
    a0 = pl.multiple_of(j * na, na)
    half = PEER_EXP // 2
    rg = 8
    zero = jnp.zeros((rg, PEER_SUB), BF16)

    def row_bcast(ref, hh, r, ts):
        w = ref[pl.ds(hh * nk + a0, na), ts][r:r + 1]
        return jnp.broadcast_to(w, (rg, PEER_SUB)).astype(BF16)

    for t0 in range(0, tok, PEER_SUB):
        ts = slice(t0, t0 + PEER_SUB)
        act = [jnp.dot(u_ref[hf * half:(hf + 1) * half, :], ht_ref[:, ts], preferred_element_type=F32)
               for hf in range(2)]
        for r in range(na):
            gate = [zero] * (nk // rg)
            for hh in range(PEER_HEADS):
                ns = row_bcast(nsel_ref, hh, r, ts)
                e1 = row_bcast(e1_ref, hh, r, ts)
                for g in range(nk // rg):
                    bs = slice(hh * nk + g * rg, hh * nk + (g + 1) * rg)
                    gate[g] = gate[g] + jnp.where(rank_ref[bs, ts] < ns, e2_ref[bs, ts], zero) * e1
            for g in range(nk // rg):
                lo = r * nk + g * rg
                a = act[lo // half][lo % half:lo % half + rg, :]
                p_ref[lo:lo + rg, ts] = _gelu(a).astype(BF16) * gate[g]
        for hf in range(2):
            ds = slice(hf * (D // 2), (hf + 1) * (D // 2))
            acc_ref[ds, ts] += jnp.dot(vt_ref[ds, :], p_ref[:, ts], preferred_element_type=F32)

    @pl.when(j == pl.num_programs(1) - 1)
    def _():
        y = x_ref[...] + mod_ref[0][:, 5 * D:6 * D] * acc_ref[...].T
        if final_norm:
            ms = jnp.mean(y * y, axis=-1, keepdims=True)
            y = y * lax.rsqrt(ms + EPS) * fg_ref[...]
        o_ref[...] = y


def _peer_dense(x, mod, ht, u, vt, nsel, e1, rank, e2, fg, np_rows, seq_rows, final_norm):
    n = x.shape[0]
    tk, ex = PEER_TOK, PEER_EXP
    rowmap = _mod_row_map(np_rows // tk, seq_rows // tk)
    rspec = pl.BlockSpec((PEER_HEADS * PEER_NKEYS, tk), lambda i, j: (0, i))
    return pl.pallas_call(
        functools.partial(_peer_body, final_norm=final_norm),
        grid=(n // tk, u.shape[0] // ex),
        in_specs=[pl.BlockSpec((D, tk), lambda i, j: (0, i)),
                  pl.BlockSpec((ex, D), lambda i, j: (j, 0)),
                  pl.BlockSpec((D, ex), lambda i, j: (0, j)),
                  rspec, rspec, rspec, rspec,
                  pl.BlockSpec((tk, D), lambda i, j: (i, 0)),
                  pl.BlockSpec((1, 1, 6 * D), lambda i, j: (rowmap(i), 0, 0)),
                  pl.BlockSpec((1, D), lambda i, j: (0, 0))],
        out_specs=pl.BlockSpec((tk, D), lambda i, j: (i, 0)),
        out_shape=jax.ShapeDtypeStruct((n, D), F32),
        scratch_shapes=[pltpu.VMEM((D, tk), F32), pltpu.VMEM((ex, tk), BF16)],
        compiler_params=_cp("arbitrary", "arbitrary"),
        name="peer_dense",
    )(ht, u, vt, nsel, e1, rank, e2, x, mod, fg)


def _peer(x, mod, norm_g, wq, sk, u, v, fg, np_rows, seq_rows, final_norm):
    ht, nsel, e1, rank, e2 = _route(x, mod, norm_g, wq.astype(BF16), sk.astype(BF16), np_rows, seq_rows)
    return _peer_dense(x, mod, ht, u.astype(BF16), v.T.astype(BF16), nsel, e1, rank, e2, fg,
                       np_rows, seq_rows, final_norm)


def _s5_in_body(x_ref, mod_ref, g_ref, w_ref, o_ref):
    mod = mod_ref[0]
    h = _norm_mod(x_ref[...], g_ref[...], mod[:, D:2 * D], mod[:, 0:D]).astype(BF16)
    o_ref[...] = jnp.dot(h, w_ref[...], preferred_element_type=F32).astype(BF16)


def _s5_in(x, mod, norm_g, w, np_rows, seq_rows):
    n = x.shape[0]
    tm = ROW_TILE
    rowmap = _mod_row_map(np_rows // tm, seq_rows // tm)
    return pl.pallas_call(
        _s5_in_body,
        grid=(n // tm,),
        in_specs=[pl.BlockSpec((tm, D), lambda i: (i, 0)),
                  pl.BlockSpec((1, 1, 6 * D), lambda i: (rowmap(i), 0, 0)),
                  pl.BlockSpec((1, D), lambda i: (0, 0)),
                  pl.BlockSpec((D, D), lambda i: (0, 0))],
        out_specs=pl.BlockSpec((tm, D), lambda i: (i, 0)),
        out_shape=jax.ShapeDtypeStruct((n, D), BF16),
        compiler_params=_cp("arbitrary"),
        name="s5_in",
    )(x, mod, norm_g, w)


def _s5_core_body(u_ref, m_ref, win_ref, cout_ref, pw_ref, *rest, kseq, has_init, emit_states):
    rest = list(rest)
    h0_ref = rest.pop(0) if has_init else None
    y_ref = rest.pop(0)
    sf_ref = rest.pop(0) if emit_states else None
    kb = u_ref.shape[1]
    row = lax.broadcasted_iota(jnp.int32, (kb, 128), 0)
    kk = row & (kseq - 1)

    def cmul(a1, a2, x):
        return a1 * x + a2 * pltpu.roll(x, 64, 1)

    for gi in range(u_ref.shape[0]):
        u = u_ref[gi]
        y = jnp.dot(u, m_ref[gi], preferred_element_type=F32)
        for d in range(2):
            s = jnp.dot(u, win_ref[gi, d], preferred_element_type=F32)
            edge = (kk == 0) if d == 0 else (kk == kseq - 1)
            if has_init:
                h0 = h0_ref[gi, d]
                s = s + jnp.where(edge, cmul(pw_ref[gi, d, 0, 0:1], pw_ref[gi, d, 0, 1:2], h0), 0.0)
            step, lvl = 1, 0
            while step < kseq:
                if d == 0:
                    sh = pltpu.roll(s, step, 0)
                    ok = kk >= step
                else:
                    sh = pltpu.roll(s, kb - step, 0)
                    ok = kk < kseq - step
                s = s + jnp.where(ok, cmul(pw_ref[gi, d, lvl, 0:1], pw_ref[gi, d, lvl, 1:2], sh), 0.0)
                step, lvl = step * 2, lvl + 1
            if emit_states:
                sf_ref[gi, d] = s
            hin = pltpu.roll(s, 1, 0) if d == 0 else pltpu.roll(s, kb - 1, 0)
            if has_init:
                hin = jnp.where(edge, h0, hin)
            else:
                hin = jnp.where(edge, 0.0, hin)
            y = y + jnp.dot(hin.astype(BF16), cout_ref[gi, d], preferred_element_type=F32)
        y_ref[gi] = _gelu(y).astype(BF16)


def _s5_core(ug, mats, row0, rows, kseq, h0=None, emit_states=False):
    m, win, cout, pw = mats
    kb = 128
    b0 = row0 // kb
    nlv = pw.shape[2]
    gb = S5_GROUP_BLOCK
    in_specs = [pl.BlockSpec((gb, kb, 256), lambda g, i: (g, b0 + i, 0)),
                pl.BlockSpec((gb, 256, 256), lambda g, i: (g, 0, 0)),
                pl.BlockSpec((gb, 2, 256, 128), lambda g, i: (g, 0, 0, 0)),
                pl.BlockSpec((gb, 2, 128, 256), lambda g, i: (g, 0, 0, 0)),
                pl.BlockSpec((gb, 2, nlv, 2, 128), lambda g, i: (g, 0, 0, 0, 0))]
    args = [ug, m, win, cout, pw]
    if h0 is not None:
        assert kseq == kb
        in_specs.append(pl.BlockSpec((gb, None, 2, 1, 128), lambda g, i: (g, i, 0, 0, 0)))
        args.append(h0)
    out_specs = [pl.BlockSpec((gb, kb, 256), lambda g, i: (g, i, 0))]
    out_shape = [jax.ShapeDtypeStruct((S5_GROUPS, rows, 256), BF16)]
    if emit_states:
        out_specs.append(pl.BlockSpec((gb, 2, kb, 128), lambda g, i: (g, 0, i, 0)))
        out_shape.append(jax.ShapeDtypeStruct((S5_GROUPS, 2, rows, 128), F32))
    return pl.pallas_call(
        functools.partial(_s5_core_body, kseq=kseq, has_init=h0 is not None, emit_states=emit_states),
        grid=(S5_GROUPS // gb, rows // kb),
        in_specs=in_specs,
        out_specs=out_specs,
        out_shape=out_shape,
        compiler_params=_cp("arbitrary", "arbitrary"),
        name="s5_core_ctx" if h0 is not None else "s5_core",
    )(*args)


def _s5_out_body(x_ref, mod_ref, y_ref, w_ref, o_ref):
    zz = jnp.dot(y_ref[...], w_ref[...], preferred_element_type=F32)
    za = zz[:, 0:D]
    zb = zz[:, D:2 * D]
    o_ref[...] = x_ref[...] + mod_ref[0][:, 2 * D:3 * D] * (za / (1.0 + jnp.exp(-zb)))


def _s5_out(x, mod, y, w, np_rows, seq_rows):
    n = x.shape[0]
    tm = ROW_TILE
    rowmap = _mod_row_map(np_rows // tm, seq_rows // tm)
    return pl.pallas_call(
        _s5_out_body,
        grid=(n // tm,),
        in_specs=[pl.BlockSpec((tm, D), lambda i: (i, 0)),
                  pl.BlockSpec((1, 1, 6 * D), lambda i: (rowmap(i), 0, 0)),
                  pl.BlockSpec((tm, D), lambda i: (i, 0)),
                  pl.BlockSpec((D, 2 * D), lambda i: (0, 0))],
        out_specs=pl.BlockSpec((tm, D), lambda i: (i, 0)),
        out_shape=jax.ShapeDtypeStruct((n, D), F32),
        compiler_params=_cp("arbitrary"),
        name="s5_out",
    )(x, mod, y, w)


def _s5_matrices(a_re, a_im, log_dt, b_re, b_im, c_re, c_im, d_skip):
    cs = S5_CHUNK
    dt = jnp.exp(log_dt)[..., None]
    lr, li = a_re * dt, a_im * dt

    def lam_pow(tau):
        mag = jnp.exp(lr[:, :, None, :] * tau[:, None])
        ang = li[:, :, None, :] * tau[:, None]
        return mag * jnp.cos(ang), mag * jnp.sin(ang)

    l1r, l1i = jnp.exp(lr) * jnp.cos(li), jnp.exp(lr) * jnp.sin(li)
    den = a_re * a_re + a_im * a_im
    cr = ((l1r - 1.0) * a_re + l1i * a_im) / den
    ci = (l1i * a_re - (l1r - 1.0) * a_im) / den
    bt_re = b_re.transpose(0, 1, 3, 2)
    bt_im = b_im.transpose(0, 1, 3, 2)
    bbr = cr[:, :, None, :] * bt_re - ci[:, :, None, :] * bt_im
    bbi = cr[:, :, None, :] * bt_im + ci[:, :, None, :] * bt_re

    pr, pi = lam_pow(jnp.arange(cs + 1, dtype=F32))
    clr = c_re[:, :, None] * pr[:, :, :, None, :] - c_im[:, :, None] * pi[:, :, :, None, :]
    cli = c_re[:, :, None] * pi[:, :, :, None, :] + c_im[:, :, None] * pr[:, :, :, None, :]
    cl = jnp.concatenate([clr[:, :, :cs], -cli[:, :, :cs]], axis=-1).reshape(2, S5_GROUPS, cs * S5_CH, 2 * S5_P)
    bb = jnp.concatenate([bbr, bbi], axis=-1)
    kern = jnp.einsum('dgeq,dgjq->dgej', bb, cl, precision=HI)
    s_idx = jnp.arange(cs)[:, None]
    j_idx = jnp.arange(cs * S5_CH)[None, :]
    src_f = j_idx - S5_CH * s_idx
    src_b = j_idx + S5_CH * (cs - 1 - s_idx)
    kern_b = kern[1].reshape(S5_GROUPS, S5_CH, cs, S5_CH)[:, :, ::-1].reshape(S5_GROUPS, S5_CH, cs * S5_CH)
    mf = jnp.where(src_f >= 0, jnp.take(kern[0], jnp.clip(src_f, 0, cs * S5_CH - 1), axis=2), 0.0)
    mb = jnp.where(src_b < cs * S5_CH, jnp.take(kern_b, jnp.clip(src_b, 0, cs * S5_CH - 1), axis=2), 0.0)
    m = (mf + mb).transpose(0, 2, 1, 3).reshape(S5_GROUPS, cs * S5_CH, cs * S5_CH)
    m = m + jnp.eye(cs * S5_CH, dtype=F32) * jnp.tile(d_skip.reshape(S5_GROUPS, 1, S5_CH), (1, 1, cs))

    def win_dir(d, powers):
        qr_, qi_ = pr[d][:, powers][:, :, None, :], pi[d][:, powers][:, :, None, :]
        wr = qr_ * bbr[d][:, None] - qi_ * bbi[d][:, None]
        wi = qr_ * bbi[d][:, None] + qi_ * bbr[d][:, None]
        return jnp.concatenate([wr, wi], axis=-1).reshape(S5_GROUPS, cs * S5_CH, 2 * S5_P)

    win = jnp.stack([win_dir(0, jnp.arange(cs - 1, -1, -1)), win_dir(1, jnp.arange(cs))], axis=1)

    def cout_dir(d, powers):
        z = jnp.concatenate([clr[d][:, powers], -cli[d][:, powers]], axis=-1)
        return z.reshape(S5_GROUPS, cs * S5_CH, 2 * S5_P).transpose(0, 2, 1)

    cout = jnp.stack([cout_dir(0, jnp.arange(1, cs + 1)), cout_dir(1, jnp.arange(cs, 0, -1))], axis=1)

    qr, qi = lam_pow(cs * (2.0 ** jnp.arange(7, dtype=F32)))
    a1 = jnp.concatenate([qr, qr], axis=-1)
    a2 = jnp.concatenate([-qi, qi], axis=-1)
    pw = jnp.stack([a1, a2], axis=3).transpose(1, 0, 2, 3, 4)
    return m.astype(BF16), win.astype(BF16), cout.astype(BF16), pw


def _rope_tables(dec_seq):
    rows = dec_seq // GRID_W
    row_id = jnp.repeat(jnp.arange(rows), GRID_W).astype(F32)
    col_id = jnp.tile(jnp.arange(GRID_W), rows).astype(F32)
    quarter = DA_DIM // 4
    inv = ROPE_BASE ** (-jnp.arange(quarter, dtype=F32) / quarter)
    ang_r = row_id[:, None] * inv
    ang_c = col_id[:, None] * inv
    ang = jnp.concatenate([ang_r, ang_r, ang_c, ang_c], axis=-1)
    ang = jnp.tile(ang, (1, 2 * DA_HEADS))
    pad = jnp.zeros((ROW_TILE, ang.shape[1]), F32)
    ang = jnp.concatenate([pad, ang], axis=0)
    return jnp.cos(ang), jnp.sin(ang)


def kernel(x_prompt, x_sample, cache_da_k, cache_da_v, state_gla, state_s5, c, c_ctx,
           ada_w, ada_b, norm_g, final_norm_g, mix0_w_in, mix0_w_out, da_lam, da_subln_g,
           gla_gate_up, gla_gate_bias, gla_norm_g, s5_w_in, s5_a_re, s5_a_im, s5_log_dt,
           s5_b_re, s5_b_im, s5_c_re, s5_c_im, s5_d, s5_w_out,
           peer_w_query, peer_sub_keys, peer_u, peer_v):
    bp, lp, _ = x_prompt.shape
    bs, ls, _ = x_sample.shape
    npr, nsr = bp * lp, bs * ls
    assert npr % ls == 0 and ls % ROW_TILE == 0 and lp % ROW_TILE == 0
    x = jnp.concatenate([x_prompt.reshape(npr, D), x_sample.reshape(nsr, D)], axis=0)

    cond = jnp.concatenate([c_ctx[None], c, jnp.zeros((15 - bs, D), F32)], axis=0)
    mod_all = _ada_mod(cond, ada_w, ada_b)
    mods = [mod_all[l].reshape(16, 1, 6 * D) for l in range(mod_all.shape[0])]
    fg = final_norm_g.reshape(1, D)

    lam_init = 0.8 - 0.6 * math.exp(-0.3 * 0)
    cos_t, sin_t = _rope_tables(ls)
    w_in = jnp.pad(mix0_w_in[0], ((0, 0), (0, IN0_PAD - mix0_w_in.shape[2]))).astype(BF16)
    q, kb, vb, k32, v32, gqk, gv, gr, low = _inproj0(
        x, mods[0], norm_g[0, 0].reshape(1, D), w_in, cos_t, sin_t, npr, ls)
    sub_g = da_subln_g[0].reshape(1, 2 * DA_DIM)
    past = cache_da_k.shape[2]
    o_da = jnp.concatenate([
        _diff_attention(q, kb, vb, da_lam[0], sub_g, lam_init, 0, bp, lp),
        _diff_attention(q, kb, vb, da_lam[0], sub_g, lam_init, npr, bs, ls,
                        cache=(cache_da_k[:, 0].reshape(bs, past, 512), cache_da_v[:, 0].reshape(bs, past, 512))),
    ], axis=0)
    hk = GLA_HEADS * GLA_DK
    gup = jnp.zeros((2, 128, hk), F32)
    for d in range(2):
        gup = gup.at[d, d * GLA_RANK:(d + 1) * GLA_RANK].set(gla_gate_up[0, d])
    gbias = gla_gate_bias[0].reshape(2, 1, hk)
    ng = gla_norm_g[0].reshape(1, GLA_DV)
    s0 = state_gla[:, 0].transpose(0, 1, 4, 2, 3).reshape(bs, 2, GLA_DV, hk)
    og_p, st_p = _gla(gqk, gv, gr, low, gup, gbias, ng, 0, bp, lp)
    og_s, _ = _gla(gqk, gv, gr, low, gup, gbias, ng, npr, bs, ls, s0=s0)
    o_g = jnp.concatenate([og_p, og_s], axis=0)
    x = _outproj0(x, mods[0], o_da, o_g, mix0_w_out[0].astype(BF16), npr, ls)
    x = _peer(x, mods[0], norm_g[0, 1].reshape(1, D), peer_w_query[0],
              peer_sub_keys[0], peer_u[0], peer_v[0], fg, npr, ls, final_norm=False)

    u = _s5_in(x, mods[1], norm_g[1, 0].reshape(1, D), s5_w_in[0].astype(BF16), npr, ls)
    n = npr + nsr
    cs = S5_CHUNK
    ug = u.reshape(n // cs, cs, S5_GROUPS, S5_CH).transpose(2, 0, 1, 3).reshape(S5_GROUPS, n // cs, cs * S5_CH)
    mats = _s5_matrices(s5_a_re[0], s5_a_im[0], s5_log_dt[0], s5_b_re[0], s5_b_im[0],
                        s5_c_re[0], s5_c_im[0], s5_d[0])
    h0 = state_s5[:, 0]
    h0 = h0.transpose(2, 0, 1, 4, 3).reshape(S5_GROUPS, bs, 2, 1, 2 * S5_P)
    y_p, sf = _s5_core(ug, mats, 0, npr // cs, lp // cs, emit_states=True)
    (y_s,) = _s5_core(ug, mats, npr // cs, nsr // cs, ls // cs, h0=h0)
    yg = jnp.concatenate([y_p, y_s], axis=1)
    y = yg.reshape(S5_GROUPS, n // cs, cs, S5_CH).transpose(1, 2, 0, 3).reshape(n, D)
    x = _s5_out(x, mods[1], y, s5_w_out[0].astype(BF16), npr, ls)
    x = _peer(x, mods[1], norm_g[1, 1].reshape(1, D), peer_w_query[1],
              peer_sub_keys[1], peer_u[1], peer_v[1], fg, npr, ls, final_norm=True)

    y_prompt = x[:npr].reshape(bp, lp, D)
    y_sample = x[npr:].reshape(bs, ls, D)
    new_k = k32[:npr].reshape(bp, 1, lp, 2 * DA_HEADS, DA_DIM)
    new_v = v32[:npr].reshape(bp, 1, lp, DA_HEADS, 2 * DA_DIM)
    new_gla = st_p.reshape(bp, 2, GLA_DV, GLA_HEADS, GLA_DK).transpose(0, 1, 3, 4, 2)[:, None]
    kc = lp // cs
    sf = sf.reshape(S5_GROUPS, 2, bp, kc, 2, S5_P)
    fin = jnp.stack([sf[:, 0, :, kc - 1], sf[:, 1, :, 0]], axis=1)
    new_s5 = fin.transpose(2, 1, 0, 4, 3)[:, None]
    return (y_prompt, y_sample, new_k, new_v, new_gla, new_s5)
```

```python
import functools
import math

import jax
import jax.numpy as jnp
from jax import lax
from jax.experimental import pallas as pl
from jax.experimental.pallas import tpu as pltpu

F32 = jnp.float32
BF16 = jnp.bfloat16
HI = lax.Precision.HIGHEST

D = 1024
EPS = 1e-6
ROPE_BASE = 10000.0
GRID_W = 64
DA_HEADS = 4
DA_DIM = 64
GLA_HEADS = 4
GLA_DK = 64
GLA_DV = 128
GLA_RANK = 16
GLA_TAU = 16.0
GLA_CHUNK = 64
S5_CH = 16
S5_GROUPS = D // S5_CH
S5_P = 64
S5_CHUNK = 16
S5_GROUP_BLOCK = 4
PEER_HEADS = 8
PEER_NKEYS = 128
PEER_TOPK = 16
IN0_PAD = 3200

ROW_TILE = 256
PEER_TOK = 512
PEER_EXP = 1024
PEER_SUB = 256
VMEM_LIMIT = 48 * 1024 * 1024

NT = (((1,), (1,)), ((), ()))
TN = (((0,), (0,)), ((), ()))


def _cp(*sem, flags=None):
    return pltpu.CompilerParams(dimension_semantics=sem, vmem_limit_bytes=VMEM_LIMIT, flags=flags)


def _norm_mod(x, g, sc, sh):
    ms = jnp.mean(x * x, axis=-1, keepdims=True)
    return x * lax.rsqrt(ms + EPS) * g * (1.0 + sc) + sh


_GELU_C2 = -2.0 * 0.7978845608028654 * 1.4426950408889634
_GELU_C1 = _GELU_C2 * 0.044715


def _gelu(x):
    return x / (1.0 + jnp.exp2(x * (_GELU_C1 * (x * x) + _GELU_C2)))


def _mod_row_map(np_blocks, seq_blocks):
    def f(i):
        return jnp.where(i < np_blocks, 0, 1 + (i - np_blocks) // seq_blocks)
    return f


def _ada_body(c_ref, w_ref, b_ref, o_ref):
    c = c_ref[...]
    s = c / (1.0 + jnp.exp(-c))
    o_ref[0] = jnp.dot(s, w_ref[0], precision=HI, preferred_element_type=F32) + b_ref[0]


def _ada_mod(cond, ada_w, ada_b):
    depth, _, n = ada_w.shape
    rows = cond.shape[0]
    tn = 1536
    return pl.pallas_call(
        _ada_body,
        grid=(depth, n // tn),
        in_specs=[pl.BlockSpec((rows, D), lambda l, j: (0, 0)),
                  pl.BlockSpec((1, D, tn), lambda l, j: (l, 0, j)),
                  pl.BlockSpec((1, 1, tn), lambda l, j: (l, 0, j))],
        out_specs=pl.BlockSpec((1, rows, tn), lambda l, j: (l, 0, j)),
        out_shape=jax.ShapeDtypeStruct((depth, rows, n), F32),
        compiler_params=_cp("arbitrary", "arbitrary"),
        name="ada_mod",
    )(cond, ada_w, ada_b.reshape(depth, 1, n))


def _inproj0_body(x_ref, mod_ref, g_ref, w_ref, cos_ref, sin_ref,
                  q_o, kb_o, vb_o, k32_o, v32_o, gqk_o, gv_o, gr_o, low_o):
    mod = mod_ref[0]
    h = _norm_mod(x_ref[...], g_ref[...], mod[:, D:2 * D], mod[:, 0:D]).astype(BF16)
    y = jnp.dot(h, w_ref[...], preferred_element_type=F32)
    cos = cos_ref[...]
    sin = sin_ref[...]
    lane = lax.broadcasted_iota(jnp.int32, cos.shape, 1)
    first = (lane & 31) < 16

    def rope(t):
        rot = jnp.where(first, -pltpu.roll(t, 512 - 16, 1), pltpu.roll(t, 16, 1))
        return t * cos + rot * sin

    q = rope(y[:, 0:512])
    k = rope(y[:, 512:1024])
    v = y[:, 1024:1536]
    q_o[...] = q.astype(BF16)
    kb_o[...] = k.astype(BF16)
    vb_o[...] = v.astype(BF16)
    k32_o[...] = k
    v32_o[...] = v
    gqk_o[...] = y[:, 1536:2048].astype(BF16)
    gv_o[...] = y[:, 2048:2560].astype(BF16)
    gr_o[...] = y[:, 2560:3072].astype(BF16)
    low_o[...] = y[:, 3072:3200]


def _inproj0(x, mod, norm_g, w, cos_t, sin_t, np_rows, seq_rows):
    n = x.shape[0]
    tm = ROW_TILE
    npb, sqb = np_rows // tm, seq_rows // tm
    rowmap = _mod_row_map(npb, sqb)
    posmap = lambda i: (jnp.where(i < npb, 0, 1 + (i - npb) % sqb), 0)
    tok = lambda w_: pl.BlockSpec((tm, w_), lambda i: (i, 0))
    outs = [(512, BF16), (512, BF16), (512, BF16), (512, F32), (512, F32),
            (512, BF16), (512, BF16), (512, BF16), (128, F32)]
    return pl.pallas_call(
        _inproj0_body,
        grid=(n // tm,),
        in_specs=[tok(D),
                  pl.BlockSpec((1, 1, 6 * D), lambda i: (rowmap(i), 0, 0)),
                  pl.BlockSpec((1, D), lambda i: (0, 0)),
                  pl.BlockSpec((D, IN0_PAD), lambda i: (0, 0)),
                  pl.BlockSpec((tm, 512), posmap),
                  pl.BlockSpec((tm, 512), posmap)],
        out_specs=[tok(w_) for w_, _ in outs],
        out_shape=[jax.ShapeDtypeStruct((n, w_), dt) for w_, dt in outs],
        compiler_params=_cp("arbitrary"),
        name="inproj0",
    )(x, mod, norm_g, w, cos_t, sin_t)


def _attn_body(lam_ref, sg_ref, q_ref, k_ref, v_ref, *rest, lam_init, has_cache):
    if has_cache:
        kc_ref, vc_ref, o_ref = rest
    else:
        (o_ref,) = rest
    lf = lam_ref[...]
    lam = (jnp.exp(jnp.sum(lf[0:1] * lf[1:2], axis=-1, keepdims=True))
           - jnp.exp(jnp.sum(lf[2:3] * lf[3:4], axis=-1, keepdims=True)) + lam_init)
    q = q_ref[...] * (DA_DIM ** -0.5)
    lane = lax.broadcasted_iota(jnp.int32, q.shape, 1)
    zero = jnp.zeros_like(q)
    qm = (jnp.where(lane < DA_DIM, q, zero), jnp.where(lane >= DA_DIM, q, zero))
    k = k_ref[...]
    s = [lax.dot_general(qm[m], k, NT, preferred_element_type=F32) for m in range(2)]
    mx = [jnp.max(s[m], axis=-1, keepdims=True) for m in range(2)]
    if has_cache:
        kc = kc_ref[...].astype(BF16)
        sc = [lax.dot_general(qm[m], kc, NT, preferred_element_type=F32) for m in range(2)]
        mx = [jnp.maximum(mx[m], jnp.max(sc[m], axis=-1, keepdims=True)) for m in range(2)]
    e = [jnp.exp(s[m] - mx[m]) for m in range(2)]
    z = [jnp.sum(e[m], axis=-1, keepdims=True) for m in range(2)]
    if has_cache:
        ec = [jnp.exp(sc[m] - mx[m]) for m in range(2)]
        z = [z[m] + jnp.sum(ec[m], axis=-1, keepdims=True) for m in range(2)]
    w0 = 1.0 / z[0]
    w1 = lam / z[1]
    o = jnp.dot((e[0] * w0 - e[1] * w1).astype(BF16), v_ref[...], preferred_element_type=F32)
    if has_cache:
        o = o + jnp.dot((ec[0] * w0 - ec[1] * w1).astype(BF16), vc_ref[...].astype(BF16),
                        preferred_element_type=F32)
    ms = jnp.mean(o * o, axis=-1, keepdims=True)
    o_ref[...] = (o * lax.rsqrt(ms + EPS) * sg_ref[...] * (1.0 - lam_init)).astype(o_ref.dtype)


def _diff_attention(q, k, v, da_lam, subln_g, lam_init, row0, batch, seq, cache=None):
    tq = 256
    qb0, kb0 = row0 // tq, row0 // seq
    in_specs = [pl.BlockSpec((4, DA_DIM), lambda b, h, i: (0, 0)),
                pl.BlockSpec((1, 128), lambda b, h, i: (0, 0)),
                pl.BlockSpec((tq, 128), lambda b, h, i: (qb0 + b * (seq // tq) + i, h)),
                pl.BlockSpec((seq, 128), lambda b, h, i: (kb0 + b, h)),
                pl.BlockSpec((seq, 128), lambda b, h, i: (kb0 + b, h))]
    args = [da_lam, subln_g, q, k, v]
    if cache is not None:
        past = cache[0].shape[1]
        in_specs += [pl.BlockSpec((None, past, 128), lambda b, h, i: (b, 0, h))] * 2
        args += list(cache)
    return pl.pallas_call(
        functools.partial(_attn_body, lam_init=lam_init, has_cache=cache is not None),
        grid=(batch, DA_HEADS, seq // tq),
        in_specs=in_specs,
        out_specs=pl.BlockSpec((tq, 128), lambda b, h, i: (b * (seq // tq) + i, h)),
        out_shape=jax.ShapeDtypeStruct((batch * seq, 512), BF16),
        compiler_params=_cp("arbitrary", "arbitrary", "arbitrary"),
        name="diff_attn_ctx" if cache is not None else "diff_attn",
    )(*args)


def _gla_body(qk_ref, v_ref, r_ref, low_ref, gup_ref, gb_ref, ng_ref, *rest, seq, has_init):
    if has_init:
        s0_ref, o_ref, sf_ref, st_scr, of_scr, ob_scr, la_scr = rest
    else:
        o_ref, sf_ref, st_scr, of_scr, ob_scr, la_scr = rest
    c = GLA_CHUNK
    n = seq // c
    hk = GLA_HEADS * GLA_DK
    low = low_ref[...]
    for d in range(2):
        z = jnp.dot(low, gup_ref[d], precision=HI, preferred_element_type=F32) + gb_ref[d]
        la_scr[d] = (jnp.minimum(z, 0.0) - jnp.log(1.0 + jnp.exp(-jnp.abs(z)))) * (1.0 / GLA_TAU)
    if has_init:
        st_scr[...] = s0_ref[0]
    else:
        st_scr[...] = jnp.zeros(st_scr.shape, F32)
    row = lax.broadcasted_iota(jnp.int32, (c, c), 0)
    col = lax.broadcasted_iota(jnp.int32, (c, c), 1)
    keep = (col <= row, col >= row)
    lane = lax.broadcasted_iota(jnp.int32, (c, 128), 1)
    low_half = lane < GLA_DK

    def chunk(d, r0):
        rows = pl.ds(r0, c)
        qk = qk_ref[rows, :].astype(F32)
        q = qk[:, 0:hk] * (GLA_DK ** -0.5)
        k = qk[:, hk:2 * hk]
        v = v_ref[rows, :]
        g = la_scr[d, rows, :]
        b = jnp.dot(keep[d].astype(F32), g, precision=HI, preferred_element_type=F32)
        b_tot = b[c - 1:c] if d == 0 else b[0:1]
        q_dec = q * jnp.exp(b)
        k_inv = k * jnp.exp(-b)
        k_end = k * jnp.exp(b_tot - b)
        st = st_scr[d]
        st_b = st.astype(BF16)
        outs, news = [], []
        for pair in range(GLA_HEADS // 2):
            ps = slice(pair * 128, (pair + 1) * 128)
            kin = k_inv[:, ps].astype(BF16)
            new = None
            for sub in range(2):
                hh = pair * 2 + sub
                vs = slice(hh * GLA_DV, (hh + 1) * GLA_DV)
                sel = low_half if sub == 0 else jnp.logical_not(low_half)
                qd = jnp.where(sel, q_dec[:, ps], 0.0).astype(BF16)
                ke = jnp.where(sel, k_end[:, ps], 0.0).astype(BF16)
                inter = lax.dot_general(qd, st_b[:, ps], NT, preferred_element_type=F32)
                att = lax.dot_general(qd, kin, NT, preferred_element_type=F32)
                att = jnp.where(keep[d], att, 0.0).astype(BF16)
                outs.append(inter + jnp.dot(att, v[:, vs], preferred_element_type=F32))
                upd = lax.dot_general(v[:, vs], ke, TN, preferred_element_type=F32)
                new = upd if new is None else new + upd
            news.append(new)
        st_scr[d] = st * jnp.exp(b_tot) + jnp.concatenate(news, axis=1)
        return jnp.concatenate(outs, axis=1)

    def step(i, carry):
        rf = pl.multiple_of(i * c, c)
        of_scr[pl.ds(rf, c), :] = chunk(0, rf)
        rb = pl.multiple_of((n - 1 - i) * c, c)
        ob_scr[pl.ds(rb, c), :] = chunk(1, rb)
        return carry

    lax.fori_loop(0, n, step, 0, unroll=2)
    sf_ref[0] = st_scr[...]

    fin = 256

    def finish(i, carry):
        rows = pl.ds(pl.multiple_of(i * fin, fin), fin)
        o = of_scr[rows, :] + ob_scr[rows, :]
        r = r_ref[rows, :].astype(F32)
        gate = r / (1.0 + jnp.exp(-r))
        ng = ng_ref[...]
        for hh in range(GLA_HEADS):
            vs = slice(hh * GLA_DV, (hh + 1) * GLA_DV)
            oh = o[:, vs]
            ms = jnp.mean(oh * oh, axis=-1, keepdims=True)
            o_ref[rows, vs] = (oh * lax.rsqrt(ms + EPS) * ng * gate[:, vs]).astype(BF16)
        return carry

    lax.fori_loop(0, seq // fin, finish, 0)


def _gla(gqk, gv, gr, low, gup, gb, ng, row0, batch, seq, s0=None):
    b0 = row0 // seq
    tokmap = lambda b: (b0 + b, 0)
    hk = GLA_HEADS * GLA_DK
    in_specs = [pl.BlockSpec((seq, 512), tokmap), pl.BlockSpec((seq, 512), tokmap),
                pl.BlockSpec((seq, 512), tokmap), pl.BlockSpec((seq, 128), tokmap),
                pl.BlockSpec((2, 128, hk), lambda b: (0, 0, 0)),
                pl.BlockSpec((2, 1, hk), lambda b: (0, 0, 0)),
                pl.BlockSpec((1, GLA_DV), lambda b: (0, 0))]
    args = [gqk, gv, gr, low, gup, gb, ng]
    if s0 is not None:
        in_specs.append(pl.BlockSpec((1, 2, GLA_DV, hk), lambda b: (b, 0, 0, 0)))
        args.append(s0)
    return pl.pallas_call(
        functools.partial(_gla_body, seq=seq, has_init=s0 is not None),
        grid=(batch,),
        in_specs=in_specs,
        out_specs=[pl.BlockSpec((seq, 512), lambda b: (b, 0)),
                   pl.BlockSpec((1, 2, GLA_DV, hk), lambda b: (b, 0, 0, 0))],
        out_shape=[jax.ShapeDtypeStruct((batch * seq, 512), BF16),
                   jax.ShapeDtypeStruct((batch, 2, GLA_DV, hk), F32)],
        scratch_shapes=[pltpu.VMEM((2, GLA_DV, hk), F32),
                        pltpu.VMEM((seq, 512), F32), pltpu.VMEM((seq, 512), F32),
                        pltpu.VMEM((2, seq, hk), F32)],
        compiler_params=_cp("arbitrary"),
        name="gla_ctx" if s0 is not None else "gla",
    )(*args)


def _outproj0_body(x_ref, mod_ref, oda_ref, og_ref, w_ref, o_ref):
    m = (jnp.dot(oda_ref[...], w_ref[0:512, :], preferred_element_type=F32)
         + jnp.dot(og_ref[...], w_ref[512:1024, :], preferred_element_type=F32))
    o_ref[...] = x_ref[...] + mod_ref[0][:, 2 * D:3 * D] * m


def _outproj0(x, mod, oda, og, w, np_rows, seq_rows):
    n = x.shape[0]
    tm = ROW_TILE
    rowmap = _mod_row_map(np_rows // tm, seq_rows // tm)
    return pl.pallas_call(
        _outproj0_body,
        grid=(n // tm,),
        in_specs=[pl.BlockSpec((tm, D), lambda i: (i, 0)),
                  pl.BlockSpec((1, 1, 6 * D), lambda i: (rowmap(i), 0, 0)),
                  pl.BlockSpec((tm, 512), lambda i: (i, 0)),
                  pl.BlockSpec((tm, 512), lambda i: (i, 0)),
                  pl.BlockSpec((D, D), lambda i: (0, 0))],
        out_specs=pl.BlockSpec((tm, D), lambda i: (i, 0)),
        out_shape=jax.ShapeDtypeStruct((n, D), F32),
        compiler_params=_cp("arbitrary"),
        name="outproj0",
    )(x, mod, oda, og, w)


def _route_body(x_ref, mod_ref, g_ref, wq_ref, sk_ref, ht_o, nsel_o, e1_o, rank_o, e2_o, top_scr, s_scr):
    mod = mod_ref[0]
    h = _norm_mod(x_ref[...], g_ref[...], mod[:, 4 * D:5 * D], mod[:, 3 * D:4 * D])
    ht_o[...] = h.T.astype(BF16)
    q = jnp.dot(h.astype(BF16), wq_ref[...], preferred_element_type=F32).astype(BF16)
    tm = q.shape[0]
    neg = -jnp.inf
    k = PEER_TOPK
    row8 = lax.broadcasted_iota(jnp.int32, (8, tm), 0)
    tiles = [slice(c * 128, (c + 1) * 128) for c in range(tm // 128)]
    for hh in range(PEER_HEADS):
        hs = slice(hh * PEER_NKEYS, (hh + 1) * PEER_NKEYS)
        for t in range(2):
            c0 = (hh * 2 + t) * PEER_NKEYS
            s_scr[t] = lax.dot_general(sk_ref[hh, t], q[:, c0:c0 + PEER_NKEYS], NT,
                                       preferred_element_type=F32)
            for cs in tiles:
                cur = s_scr[t, :, cs]
                rank = jnp.full((PEER_NKEYS, 128), float(k), F32)
                for j in range(k):
                    m = jnp.max(cur, axis=0, keepdims=True)
                    top_scr[t, j:j + 1, cs] = m
                    hit = cur == m
                    if t == 1:
                        rank = jnp.where(hit, float(j), rank)
                    cur = jnp.where(hit, neg, cur)
                if t == 1:
                    rank_o[hs, cs] = rank.astype(BF16)
        t1 = top_scr[0]
        t2 = top_scr[1]
        slabs = [t1[0:1] + t2, t1[1:2] + t2[0:8], t1[2:3] + t2[0:8], t1[3:4] + t2[0:8],
                 t2[0:1] + t1[8:16]]
        for j in range(3):
            slabs.append(jnp.where(row8 >= 4, t2[j:j + 1] + t1[0:8], neg))
        cand = jnp.concatenate(slabs, axis=0)
        top = t1[0:1] + t2[0:1]
        zsum = jnp.zeros_like(top)
        kth = top
        for j in range(k):
            kth = jnp.max(cand, axis=0, keepdims=True)
            zsum = zsum + jnp.exp(kth - top)
            cand = jnp.where(cand == kth, neg, cand)
        zinv = 1.0 / zsum
        for c, cs in enumerate(tiles):
            s1 = s_scr[0, :, cs]
            nsel = jnp.zeros((PEER_NKEYS, 128), F32)
            for j in range(k):
                nsel = jnp.where(s1 + t2[j:j + 1, cs] >= kth[:, cs], float(j + 1), nsel)
            nsel_o[c, hs, :] = nsel
            e1_o[c, hs, :] = jnp.exp(s1 - t1[0:1, cs]) * zinv[:, cs]
            e2_o[hs, cs] = jnp.exp(s_scr[1, :, cs] - t2[0:1, cs]).astype(BF16)


def _route(x, mod, norm_g, wq, sk, layer, np_rows, seq_rows):
    n = x.shape[0]
    tm = ROW_TILE
    rowmap = _mod_row_map(np_rows // tm, seq_rows // tm)
    rt = lambda dt: jax.ShapeDtypeStruct((PEER_HEADS * PEER_NKEYS, n), dt)
    rspec = pl.BlockSpec((PEER_HEADS * PEER_NKEYS, tm), lambda i: (0, i))
    st_ = jax.ShapeDtypeStruct((n // 128, PEER_HEADS * PEER_NKEYS, 128), F32)
    sspec = pl.BlockSpec((tm // 128, PEER_HEADS * PEER_NKEYS, 128), lambda i: (i, 0, 0))
    return pl.pallas_call(
        _route_body,
        grid=(n // tm,),
        in_specs=[pl.BlockSpec((tm, D), lambda i: (i, 0)),
                  pl.BlockSpec((1, 1, 6 * D), lambda i: (rowmap(i), 0, 0)),
                  pl.BlockSpec((1, D), lambda i: (0, 0)),
                  pl.BlockSpec((None, D, 2 * PEER_HEADS * PEER_NKEYS), lambda i: (layer, 0, 0)),
                  pl.BlockSpec((None, PEER_HEADS, 2, PEER_NKEYS, 128), lambda i: (layer, 0, 0, 0, 0))],
        out_specs=[pl.BlockSpec((D, tm), lambda i: (0, i)), sspec, sspec, rspec, rspec],
        out_shape=[jax.ShapeDtypeStruct((D, n), BF16), st_, st_, rt(BF16), rt(BF16)],
        scratch_shapes=[pltpu.VMEM((2, PEER_TOPK, tm), F32), pltpu.VMEM((2, PEER_NKEYS, tm), F32)],
        compiler_params=_cp("arbitrary"),
        name="peer_route",
    )(x, mod, norm_g, wq, sk)


def _peer_body(ht_ref, u_ref, vt_ref, nsel_ref, e1_ref, rank_ref, e2_ref, x_ref, mod_ref, fg_ref,
               o_ref, acc_ref, p_ref, *, final_norm):
    j = pl.program_id(1)

    @pl.when(j == 0)
    def _():
        acc_ref[...] = jnp.zeros(acc_ref.shape, F32)

    nk = PEER_NKEYS
    tok = ht_ref.shape[1]
    na = PEER_EXP // nk
    a0 = pl.multiple_of(j * na, na)
    half = PEER_EXP // 2
    rg = 8
    zero = jnp.zeros((rg, PEER_SUB), BF16)

    def row_bcast(ref, hh, r, ts):
        parts = [jnp.broadcast_to(ref[c, pl.ds(hh * nk + a0, na), :][r:r + 1], (rg, 128))
                 for c in range(ts.start // 128, ts.stop // 128)]
        return jnp.concatenate(parts, axis=1).astype(BF16)

    for t0 in range(0, tok, PEER_SUB):
        ts = slice(t0, t0 + PEER_SUB)
        act = [jnp.dot(u_ref[hf * half:(hf + 1) * half, :], ht_ref[:, ts], preferred_element_type=F32)
               for hf in range(2)]
        for r in range(na):
            gate = [zero] * (nk // rg)
            for hh in range(PEER_HEADS):
                ns = row_bcast(nsel_ref, hh, r, ts)
                e1 = row_bcast(e1_ref, hh, r, ts)
                for g in range(nk // rg):
                    bs = slice(hh * nk + g * rg, hh * nk + (g + 1) * rg)
                    gate[g] = gate[g] + jnp.where(rank_ref[bs, ts] < ns, e2_ref[bs, ts], zero) * e1
            for g in range(nk // rg):
                lo = r * nk + g * rg
                a = act[lo // half][lo % half:lo % half + rg, :]
                p_ref[lo:lo + rg, ts] = _gelu(a).astype(BF16) * gate[g]
        for hf in range(2):
            ds = slice(hf * (D // 2), (hf + 1) * (D // 2))
            acc_ref[ds, ts] += jnp.dot(vt_ref[ds, :], p_ref[:, ts], preferred_element_type=F32)

    @pl.when(j == pl.num_programs(1) - 1)
    def _():
        y = x_ref[...] + mod_ref[0][:, 5 * D:6 * D] * acc_ref[...].T
        if final_norm:
            ms = jnp.mean(y * y, axis=-1, keepdims=True)
            y = y * lax.rsqrt(ms + EPS) * fg_ref[...]
        o_ref[...] = y


def _peer_dense(x, mod, ht, u, vt, layer, nsel, e1, rank, e2, fg, np_rows, seq_rows, final_norm):
    n = x.shape[0]
    tk, ex = PEER_TOK, PEER_EXP
    rowmap = _mod_row_map(np_rows // tk, seq_rows // tk)
    rspec = pl.BlockSpec((PEER_HEADS * PEER_NKEYS, tk), lambda i, j: (0, i))
    sspec = pl.BlockSpec((tk // 128, PEER_HEADS * PEER_NKEYS, 128), lambda i, j: (i, 0, 0))
    return pl.pallas_call(
        functools.partial(_peer_body, final_norm=final_norm),
        grid=(n // tk, u.shape[1] // ex),
        in_specs=[pl.BlockSpec((D, tk), lambda i, j: (0, i)),
                  pl.BlockSpec((None, ex, D), lambda i, j: (layer, j, 0)),
                  pl.BlockSpec((None, D, ex), lambda i, j: (layer, 0, j)),
                  sspec, sspec, rspec, rspec,
                  pl.BlockSpec((tk, D), lambda i, j: (i, 0)),
                  pl.BlockSpec((1, 1, 6 * D), lambda i, j: (rowmap(i), 0, 0)),
                  pl.BlockSpec((1, D), lambda i, j: (0, 0))],
        out_specs=pl.BlockSpec((tk, D), lambda i, j: (i, 0)),
        out_shape=jax.ShapeDtypeStruct((n, D), F32),
        scratch_shapes=[pltpu.VMEM((D, tk), F32), pltpu.VMEM((ex, tk), BF16)],
        compiler_params=_cp("arbitrary", "arbitrary"),
        name="peer_dense",
    )(ht, u, vt, nsel, e1, rank, e2, x, mod, fg)


def _peer_tables(w_query, sub_keys, u, v):
    return w_query.astype(BF16), sub_keys.astype(BF16), u.astype(BF16), v.transpose(0, 2, 1).astype(BF16)


def _peer(x, mod, norm_g, tables, layer, fg, np_rows, seq_rows, final_norm):
    wq, sk, u, vt = tables
    ht, nsel, e1, rank, e2 = _route(x, mod, norm_g, wq, sk, layer, np_rows, seq_rows)
    return _peer_dense(x, mod, ht, u, vt, layer, nsel, e1, rank, e2, fg, np_rows, seq_rows, final_norm)


def _window_of(lane, i):
    return (lane >> 4) == i


def _s5_in_body(x_ref, mod_ref, g_ref, w_ref, o_ref, u_scr):
    mod = mod_ref[0]
    h = _norm_mod(x_ref[...], g_ref[...], mod[:, D:2 * D], mod[:, 0:D]).astype(BF16)
    u = jnp.dot(h, w_ref[...], preferred_element_type=F32)
    cs, per = S5_CHUNK, 128 // S5_CH
    nck = u.shape[0] // cs
    for j in range(D // 128):
        u_scr[j] = u[:, j * 128:(j + 1) * 128]
    lane = lax.broadcasted_iota(jnp.int32, (nck, 128), 1)
    for j in range(D // 128):
        xs = [u_scr[j, pl.ds(s, nck, stride=cs), :] for s in range(cs)]
        for gp in range(per):
            for half in range(cs // per):
                acc = None
                for s8 in range(per):
                    piece = xs[half * per + s8]
                    shift = ((s8 - gp) * S5_CH) % 128
                    if shift:
                        piece = pltpu.roll(piece, shift, 1)
                    acc = piece if acc is None else jnp.where(_window_of(lane, s8), piece, acc)
                o_ref[j * per + gp, :, half * 128:(half + 1) * 128] = acc.astype(BF16)


def _s5_in(x, mod, norm_g, w, np_rows, seq_rows):
    n = x.shape[0]
    tm = ROW_TILE
    rowmap = _mod_row_map(np_rows // tm, seq_rows // tm)
    nck = tm // S5_CHUNK
    return pl.pallas_call(
        _s5_in_body,
        grid=(n // tm,),
        in_specs=[pl.BlockSpec((tm, D), lambda i: (i, 0)),
                  pl.BlockSpec((1, 1, 6 * D), lambda i: (rowmap(i), 0, 0)),
                  pl.BlockSpec((1, D), lambda i: (0, 0)),
                  pl.BlockSpec((D, D), lambda i: (0, 0))],
        out_specs=pl.BlockSpec((S5_GROUPS, nck, S5_CHUNK * S5_CH), lambda i: (0, i, 0)),
        out_shape=jax.ShapeDtypeStruct((S5_GROUPS, n // S5_CHUNK, S5_CHUNK * S5_CH), BF16),
        scratch_shapes=[pltpu.VMEM((D // 128, tm, 128), F32)],
        compiler_params=_cp("arbitrary"),
        name="s5_in",
    )(x, mod, norm_g, w)


def _s5_core_body(u_ref, m_ref, win_ref, cout_ref, pw_ref, *rest, kseq, has_init, emit_states):
    rest = list(rest)
    h0_ref = rest.pop(0) if has_init else None
    y_ref = rest.pop(0)
    sf_ref = rest.pop(0) if emit_states else None
    kb = u_ref.shape[1]
    row = lax.broadcasted_iota(jnp.int32, (kb, 128), 0)
    kk = row & (kseq - 1)

    def cmul(a1, a2, x):
        return a1 * x + a2 * pltpu.roll(x, 64, 1)

    for gi in range(u_ref.shape[0]):
        u = u_ref[gi]
        y = jnp.dot(u, m_ref[gi], preferred_element_type=F32)
        for d in range(2):
            s = jnp.dot(u, win_ref[gi, d], preferred_element_type=F32)
            edge = (kk == 0) if d == 0 else (kk == kseq - 1)
            if has_init:
                h0 = h0_ref[gi, d]
                s = s + jnp.where(edge, cmul(pw_ref[gi, d, 0, 0:1], pw_ref[gi, d, 0, 1:2], h0), 0.0)
            step, lvl = 1, 0
            while step < kseq:
                if d == 0:
                    sh = pltpu.roll(s, step, 0)
                    ok = kk >= step
                else:
                    sh = pltpu.roll(s, kb - step, 0)
                    ok = kk < kseq - step
                s = s + jnp.where(ok, cmul(pw_ref[gi, d, lvl, 0:1], pw_ref[gi, d, lvl, 1:2], sh), 0.0)
                step, lvl = step * 2, lvl + 1
            if emit_states:
                sf_ref[gi, d] = s
            hin = pltpu.roll(s, 1, 0) if d == 0 else pltpu.roll(s, kb - 1, 0)
            if has_init:
                hin = jnp.where(edge, h0, hin)
            else:
                hin = jnp.where(edge, 0.0, hin)
            y = y + jnp.dot(hin.astype(BF16), cout_ref[gi, d], preferred_element_type=F32)
        y_ref[gi] = _gelu(y).astype(BF16)


def _s5_core(ug, mats, row0, rows, kseq, h0=None, emit_states=False):
    m, win, cout, pw = mats
    kb = 128
    b0 = row0 // kb
    nlv = pw.shape[2]
    gb = S5_GROUP_BLOCK
    in_specs = [pl.BlockSpec((gb, kb, 256), lambda g, i: (g, b0 + i, 0)),
                pl.BlockSpec((gb, 256, 256), lambda g, i: (g, 0, 0)),
                pl.BlockSpec((gb, 2, 256, 128), lambda g, i: (g, 0, 0, 0)),
                pl.BlockSpec((gb, 2, 128, 256), lambda g, i: (g, 0, 0, 0)),
                pl.BlockSpec((gb, 2, nlv, 2, 128), lambda g, i: (g, 0, 0, 0, 0))]
    args = [ug, m, win, cout, pw]
    if h0 is not None:
        assert kseq == kb
        in_specs.append(pl.BlockSpec((gb, None, 2, 1, 128), lambda g, i: (g, i, 0, 0, 0)))
        args.append(h0)
    out_specs = [pl.BlockSpec((gb, kb, 256), lambda g, i: (g, i, 0))]
    out_shape = [jax.ShapeDtypeStruct((S5_GROUPS, rows, 256), BF16)]
    if emit_states:
        out_specs.append(pl.BlockSpec((gb, 2, kb, 128), lambda g, i: (g, 0, i, 0)))
        out_shape.append(jax.ShapeDtypeStruct((S5_GROUPS, 2, rows, 128), F32))
    return pl.pallas_call(
        functools.partial(_s5_core_body, kseq=kseq, has_init=h0 is not None, emit_states=emit_states),
        grid=(S5_GROUPS // gb, rows // kb),
        in_specs=in_specs,
        out_specs=out_specs,
        out_shape=out_shape,
        compiler_params=_cp("arbitrary", "arbitrary"),
        name="s5_core_ctx" if h0 is not None else "s5_core",
    )(*args)


def _s5_out_body(x_ref, mod_ref, y_ref, w_ref, o_ref, y_scr):
    cs, per = S5_CHUNK, 128 // S5_CH
    nck = y_ref.shape[1]
    lane = lax.broadcasted_iota(jnp.int32, (nck, 128), 1)
    for j in range(D // 128):
        src = [[y_ref[j * per + gp, :, half * 128:(half + 1) * 128].astype(F32) for half in range(cs // per)]
               for gp in range(per)]
        for t in range(cs):
            half, t8 = divmod(t, per)
            acc = None
            for gp in range(per):
                piece = src[gp][half]
                shift = ((gp - t8) * S5_CH) % 128
                if shift:
                    piece = pltpu.roll(piece, shift, 1)
                acc = piece if acc is None else jnp.where(_window_of(lane, gp), piece, acc)
            y_scr[j, pl.ds(t, nck, stride=cs), :] = acc
    y = jnp.concatenate([y_scr[j] for j in range(D // 128)], axis=1).astype(BF16)
    zz = jnp.dot(y, w_ref[...], preferred_element_type=F32)
    za = zz[:, 0:D]
    zb = zz[:, D:2 * D]
    o_ref[...] = x_ref[...] + mod_ref[0][:, 2 * D:3 * D] * (za / (1.0 + jnp.exp(-zb)))


def _s5_out(x, mod, y, w, np_rows, seq_rows):
    n = x.shape[0]
    tm = ROW_TILE
    rowmap = _mod_row_map(np_rows // tm, seq_rows // tm)
    return pl.pallas_call(
        _s5_out_body,
        grid=(n // tm,),
        in_specs=[pl.BlockSpec((tm, D), lambda i: (i, 0)),
                  pl.BlockSpec((1, 1, 6 * D), lambda i: (rowmap(i), 0, 0)),
                  pl.BlockSpec((S5_GROUPS, tm // S5_CHUNK, S5_CHUNK * S5_CH), lambda i: (0, i, 0)),
                  pl.BlockSpec((D, 2 * D), lambda i: (0, 0))],
        out_specs=pl.BlockSpec((tm, D), lambda i: (i, 0)),
        out_shape=jax.ShapeDtypeStruct((n, D), F32),
        scratch_shapes=[pltpu.VMEM((D // 128, tm, 128), F32)],
        compiler_params=_cp("arbitrary"),
        name="s5_out",
    )(x, mod, y, w)


def _s5_matrices(a_re, a_im, log_dt, b_re, b_im, c_re, c_im, d_skip):
    cs = S5_CHUNK
    dt = jnp.exp(log_dt)[..., None]
    lr, li = a_re * dt, a_im * dt

    def lam_pow(tau):
        mag = jnp.exp(lr[:, :, None, :] * tau[:, None])
        ang = li[:, :, None, :] * tau[:, None]
        return mag * jnp.cos(ang), mag * jnp.sin(ang)

    l1r, l1i = jnp.exp(lr) * jnp.cos(li), jnp.exp(lr) * jnp.sin(li)
    den = a_re * a_re + a_im * a_im
    cr = ((l1r - 1.0) * a_re + l1i * a_im) / den
    ci = (l1i * a_re - (l1r - 1.0) * a_im) / den
    bt_re = b_re.transpose(0, 1, 3, 2)
    bt_im = b_im.transpose(0, 1, 3, 2)
    bbr = cr[:, :, None, :] * bt_re - ci[:, :, None, :] * bt_im
    bbi = cr[:, :, None, :] * bt_im + ci[:, :, None, :] * bt_re

    pr, pi = lam_pow(jnp.arange(cs + 1, dtype=F32))
    clr = c_re[:, :, None] * pr[:, :, :, None, :] - c_im[:, :, None] * pi[:, :, :, None, :]
    cli = c_re[:, :, None] * pi[:, :, :, None, :] + c_im[:, :, None] * pr[:, :, :, None, :]
    cl = jnp.concatenate([clr[:, :, :cs], -cli[:, :, :cs]], axis=-1).reshape(2, S5_GROUPS, cs * S5_CH, 2 * S5_P)
    bb = jnp.concatenate([bbr, bbi], axis=-1)
    kern = jnp.einsum('dgeq,dgjq->dgej', bb, cl, precision=HI)
    s_idx = jnp.arange(cs)[:, None]
    j_idx = jnp.arange(cs * S5_CH)[None, :]
    src_f = j_idx - S5_CH * s_idx
    src_b = j_idx + S5_CH * (cs - 1 - s_idx)
    kern_b = kern[1].reshape(S5_GROUPS, S5_CH, cs, S5_CH)[:, :, ::-1].reshape(S5_GROUPS, S5_CH, cs * S5_CH)
    mf = jnp.where(src_f >= 0, jnp.take(kern[0], jnp.clip(src_f, 0, cs * S5_CH - 1), axis=2), 0.0)
    mb = jnp.where(src_b < cs * S5_CH, jnp.take(kern_b, jnp.clip(src_b, 0, cs * S5_CH - 1), axis=2), 0.0)
    m = (mf + mb).transpose(0, 2, 1, 3).reshape(S5_GROUPS, cs * S5_CH, cs * S5_CH)
    m = m + jnp.eye(cs * S5_CH, dtype=F32) * jnp.tile(d_skip.reshape(S5_GROUPS, 1, S5_CH), (1, 1, cs))

    def win_dir(d, powers):
        qr_, qi_ = pr[d][:, powers][:, :, None, :], pi[d][:, powers][:, :, None, :]
        wr = qr_ * bbr[d][:, None] - qi_ * bbi[d][:, None]
        wi = qr_ * bbi[d][:, None] + qi_ * bbr[d][:, None]
        return jnp.concatenate([wr, wi], axis=-1).reshape(S5_GROUPS, cs * S5_CH, 2 * S5_P)

    win = jnp.stack([win_dir(0, jnp.arange(cs - 1, -1, -1)), win_dir(1, jnp.arange(cs))], axis=1)

    def cout_dir(d, powers):
        z = jnp.concatenate([clr[d][:, powers], -cli[d][:, powers]], axis=-1)
        return z.reshape(S5_GROUPS, cs * S5_CH, 2 * S5_P).transpose(0, 2, 1)

    cout = jnp.stack([cout_dir(0, jnp.arange(1, cs + 1)), cout_dir(1, jnp.arange(cs, 0, -1))], axis=1)

    qr, qi = lam_pow(cs * (2.0 ** jnp.arange(7, dtype=F32)))
    a1 = jnp.concatenate([qr, qr], axis=-1)
    a2 = jnp.concatenate([-qi, qi], axis=-1)
    pw = jnp.stack([a1, a2], axis=3).transpose(1, 0, 2, 3, 4)
    return m.astype(BF16), win.astype(BF16), cout.astype(BF16), pw


def _rope_tables(dec_seq):
    rows = dec_seq // GRID_W
    row_id = jnp.repeat(jnp.arange(rows), GRID_W).astype(F32)
    col_id = jnp.tile(jnp.arange(GRID_W), rows).astype(F32)
    quarter = DA_DIM // 4
    inv = ROPE_BASE ** (-jnp.arange(quarter, dtype=F32) / quarter)
    ang_r = row_id[:, None] * inv
    ang_c = col_id[:, None] * inv
    ang = jnp.concatenate([ang_r, ang_r, ang_c, ang_c], axis=-1)
    ang = jnp.tile(ang, (1, 2 * DA_HEADS))
    pad = jnp.zeros((ROW_TILE, ang.shape[1]), F32)
    ang = jnp.concatenate([pad, ang], axis=0)
    return jnp.cos(ang), jnp.sin(ang)


def kernel(x_prompt, x_sample, cache_da_k, cache_da_v, state_gla, state_s5, c, c_ctx,
           ada_w, ada_b, norm_g, final_norm_g, mix0_w_in, mix0_w_out, da_lam, da_subln_g,
           gla_gate_up, gla_gate_bias, gla_norm_g, s5_w_in, s5_a_re, s5_a_im, s5_log_dt,
           s5_b_re, s5_b_im, s5_c_re, s5_c_im, s5_d, s5_w_out,
           peer_w_query, peer_sub_keys, peer_u, peer_v):
    bp, lp, _ = x_prompt.shape
    bs, ls, _ = x_sample.shape
    npr, nsr = bp * lp, bs * ls
    assert npr % ls == 0 and ls % ROW_TILE == 0 and lp % ROW_TILE == 0
    x = jnp.concatenate([x_prompt.reshape(npr, D), x_sample.reshape(nsr, D)], axis=0)

    cond = jnp.concatenate([c_ctx[None], c, jnp.zeros((15 - bs, D), F32)], axis=0)
    mod_all = _ada_mod(cond, ada_w, ada_b)
    mods = [mod_all[l].reshape(16, 1, 6 * D) for l in range(mod_all.shape[0])]
    fg = final_norm_g.reshape(1, D)

    lam_init = 0.8 - 0.6 * math.exp(-0.3 * 0)
    cos_t, sin_t = _rope_tables(ls)
    w_in = jnp.pad(mix0_w_in[0], ((0, 0), (0, IN0_PAD - mix0_w_in.shape[2]))).astype(BF16)
    q, kb, vb, k32, v32, gqk, gv, gr, low = _inproj0(
        x, mods[0], norm_g[0, 0].reshape(1, D), w_in, cos_t, sin_t, npr, ls)
    sub_g = da_subln_g[0].reshape(1, 2 * DA_DIM)
    past = cache_da_k.shape[2]
    o_da = jnp.concatenate([
        _diff_attention(q, kb, vb, da_lam[0], sub_g, lam_init, 0, bp, lp),
        _diff_attention(q, kb, vb, da_lam[0], sub_g, lam_init, npr, bs, ls,
                        cache=(cache_da_k[:, 0].reshape(bs, past, 512), cache_da_v[:, 0].reshape(bs, past, 512))),
    ], axis=0)
    hk = GLA_HEADS * GLA_DK
    gup = jnp.zeros((2, 128, hk), F32)
    for d in range(2):
        gup = gup.at[d, d * GLA_RANK:(d + 1) * GLA_RANK].set(gla_gate_up[0, d])
    gbias = gla_gate_bias[0].reshape(2, 1, hk)
    ng = gla_norm_g[0].reshape(1, GLA_DV)
    s0 = state_gla[:, 0].transpose(0, 1, 4, 2, 3).reshape(bs, 2, GLA_DV, hk)
    og_p, st_p = _gla(gqk, gv, gr, low, gup, gbias, ng, 0, bp, lp)
    og_s, _ = _gla(gqk, gv, gr, low, gup, gbias, ng, npr, bs, ls, s0=s0)
    o_g = jnp.concatenate([og_p, og_s], axis=0)
    x = _outproj0(x, mods[0], o_da, o_g, mix0_w_out[0].astype(BF16), npr, ls)
    tables = _peer_tables(peer_w_query, peer_sub_keys, peer_u, peer_v)
    x = _peer(x, mods[0], norm_g[0, 1].reshape(1, D), tables, 0, fg, npr, ls, final_norm=False)

    ug = _s5_in(x, mods[1], norm_g[1, 0].reshape(1, D), s5_w_in[0].astype(BF16), npr, ls)
    cs = S5_CHUNK
    mats = _s5_matrices(s5_a_re[0], s5_a_im[0], s5_log_dt[0], s5_b_re[0], s5_b_im[0],
                        s5_c_re[0], s5_c_im[0], s5_d[0])
    h0 = state_s5[:, 0]
    h0 = h0.transpose(2, 0, 1, 4, 3).reshape(S5_GROUPS, bs, 2, 1, 2 * S5_P)
    y_p, sf = _s5_core(ug, mats, 0, npr // cs, lp // cs, emit_states=True)
    (y_s,) = _s5_core(ug, mats, npr // cs, nsr // cs, ls // cs, h0=h0)
    yg = jnp.concatenate([y_p, y_s], axis=1)
    x = _s5_out(x, mods[1], yg, s5_w_out[0].astype(BF16), npr, ls)
    x = _peer(x, mods[1], norm_g[1, 1].reshape(1, D), tables, 1, fg, npr, ls, final_norm=True)

    y_prompt = x[:npr].reshape(bp, lp, D)
    y_sample = x[npr:].reshape(bs, ls, D)
    new_k = k32[:npr].reshape(bp, 1, lp, 2 * DA_HEADS, DA_DIM)
    new_v = v32[:npr].reshape(bp, 1, lp, DA_HEADS, 2 * DA_DIM)
    new_gla = st_p.reshape(bp, 2, GLA_DV, GLA_HEADS, GLA_DK).transpose(0, 1, 3, 4, 2)[:, None]
    kc = lp // cs
    sf = sf.reshape(S5_GROUPS, 2, bp, kc, 2, S5_P)
    fin = jnp.stack([sf[:, 0, :, kc - 1], sf[:, 1, :, 0]], axis=1)
    new_s5 = fin.transpose(2, 1, 0, 4, 3)[:, None]
    return (y_prompt, y_sample, new_k, new_v, new_gla, new_s5)
```

```python
import functools
import math

import jax
import jax.numpy as jnp
from jax import lax
from jax.experimental import pallas as pl
from jax.experimental.pallas import tpu as pltpu

F32 = jnp.float32
BF16 = jnp.bfloat16
HI = lax.Precision.HIGHEST

D = 1024
EPS = 1e-6
ROPE_BASE = 10000.0
GRID_W = 64
DA_HEADS = 4
DA_DIM = 64
GLA_HEADS = 4
GLA_DK = 64
GLA_DV = 128
GLA_RANK = 16
GLA_TAU = 16.0
GLA_CHUNK = 64
S5_CH = 16
S5_GROUPS = D // S5_CH
S5_P = 64
S5_CHUNK = 16
S5_GROUP_BLOCK = 4
PEER_HEADS = 8
PEER_NKEYS = 128
PEER_TOPK = 16
IN0_PAD = 3200

ROW_TILE = 256
PEER_TOK = 512
PEER_EXP = 1024
PEER_SUB = 256
VMEM_LIMIT = 48 * 1024 * 1024

NT = (((1,), (1,)), ((), ()))
TN = (((0,), (0,)), ((), ()))


def _cp(*sem, flags=None):
    return pltpu.CompilerParams(dimension_semantics=sem, vmem_limit_bytes=VMEM_LIMIT, flags=flags)


def _norm_mod(x, g, sc, sh):
    ms = jnp.mean(x * x, axis=-1, keepdims=True)
    return x * lax.rsqrt(ms + EPS) * g * (1.0 + sc) + sh


_GELU_C2 = -2.0 * 0.7978845608028654 * 1.4426950408889634
_GELU_C1 = _GELU_C2 * 0.044715


def _gelu(x):
    return x / (1.0 + jnp.exp2(x * (_GELU_C1 * (x * x) + _GELU_C2)))


def _mod_row_map(np_blocks, seq_blocks):
    def f(i):
        return jnp.where(i < np_blocks, 0, 1 + (i - np_blocks) // seq_blocks)
    return f


def _ada_body(c_ref, w_ref, b_ref, o_ref):
    c = c_ref[...]
    s = c / (1.0 + jnp.exp(-c))
    o_ref[0] = jnp.dot(s, w_ref[0], precision=HI, preferred_element_type=F32) + b_ref[0]


def _ada_mod(cond, ada_w, ada_b):
    depth, _, n = ada_w.shape
    rows = cond.shape[0]
    tn = 1536
    return pl.pallas_call(
        _ada_body,
        grid=(depth, n // tn),
        in_specs=[pl.BlockSpec((rows, D), lambda l, j: (0, 0)),
                  pl.BlockSpec((1, D, tn), lambda l, j: (l, 0, j)),
                  pl.BlockSpec((1, 1, tn), lambda l, j: (l, 0, j))],
        out_specs=pl.BlockSpec((1, rows, tn), lambda l, j: (l, 0, j)),
        out_shape=jax.ShapeDtypeStruct((depth, rows, n), F32),
        compiler_params=_cp("arbitrary", "arbitrary"),
        name="ada_mod",
    )(cond, ada_w, ada_b.reshape(depth, 1, n))


def _inproj0_body(x_ref, mod_ref, g_ref, w_ref, cos_ref, sin_ref,
                  q_o, kb_o, vb_o, k32_o, v32_o, gqk_o, gv_o, gr_o, low_o):
    mod = mod_ref[0]
    h = _norm_mod(x_ref[...], g_ref[...], mod[:, D:2 * D], mod[:, 0:D]).astype(BF16)
    y = jnp.dot(h, w_ref[...], preferred_element_type=F32)
    cos = cos_ref[...]
    sin = sin_ref[...]
    lane = lax.broadcasted_iota(jnp.int32, cos.shape, 1)
    first = (lane & 31) < 16

    def rope(t):
        rot = jnp.where(first, -pltpu.roll(t, 512 - 16, 1), pltpu.roll(t, 16, 1))
        return t * cos + rot * sin

    q = rope(y[:, 0:512])
    k = rope(y[:, 512:1024])
    v = y[:, 1024:1536]
    q_o[...] = q.astype(BF16)
    kb_o[...] = k.astype(BF16)
    vb_o[...] = v.astype(BF16)
    k32_o[...] = k
    v32_o[...] = v
    gqk_o[...] = y[:, 1536:2048].astype(BF16)
    gv_o[...] = y[:, 2048:2560].astype(BF16)
    gr_o[...] = y[:, 2560:3072].astype(BF16)
    low_o[...] = y[:, 3072:3200]


def _inproj0(x, mod, norm_g, w, cos_t, sin_t, np_rows, seq_rows):
    n = x.shape[0]
    tm = ROW_TILE
    npb, sqb = np_rows // tm, seq_rows // tm
    rowmap = _mod_row_map(npb, sqb)
    posmap = lambda i: (jnp.where(i < npb, 0, 1 + (i - npb) % sqb), 0)
    tok = lambda w_: pl.BlockSpec((tm, w_), lambda i: (i, 0))
    outs = [(512, BF16), (512, BF16), (512, BF16), (512, F32), (512, F32),
            (512, BF16), (512, BF16), (512, BF16), (128, F32)]
    return pl.pallas_call(
        _inproj0_body,
        grid=(n // tm,),
        in_specs=[tok(D),
                  pl.BlockSpec((1, 1, 6 * D), lambda i: (rowmap(i), 0, 0)),
                  pl.BlockSpec((1, D), lambda i: (0, 0)),
                  pl.BlockSpec((D, IN0_PAD), lambda i: (0, 0)),
                  pl.BlockSpec((tm, 512), posmap),
                  pl.BlockSpec((tm, 512), posmap)],
        out_specs=[tok(w_) for w_, _ in outs],
        out_shape=[jax.ShapeDtypeStruct((n, w_), dt) for w_, dt in outs],
        compiler_params=_cp("arbitrary"),
        name="inproj0",
    )(x, mod, norm_g, w, cos_t, sin_t)


def _attn_body(lam_ref, sg_ref, q_ref, k_ref, v_ref, *rest, lam_init, has_cache):
    if has_cache:
        kc_ref, vc_ref, o_ref = rest
    else:
        (o_ref,) = rest
    lf = lam_ref[...]
    lam = (jnp.exp(jnp.sum(lf[0:1] * lf[1:2], axis=-1, keepdims=True))
           - jnp.exp(jnp.sum(lf[2:3] * lf[3:4], axis=-1, keepdims=True)) + lam_init)
    q = q_ref[...] * (DA_DIM ** -0.5)
    lane = lax.broadcasted_iota(jnp.int32, q.shape, 1)
    zero = jnp.zeros_like(q)
    qm = (jnp.where(lane < DA_DIM, q, zero), jnp.where(lane >= DA_DIM, q, zero))
    k = k_ref[...]
    s = [lax.dot_general(qm[m], k, NT, preferred_element_type=F32) for m in range(2)]
    mx = [jnp.max(s[m], axis=-1, keepdims=True) for m in range(2)]
    if has_cache:
        kc = kc_ref[...].astype(BF16)
        sc = [lax.dot_general(qm[m], kc, NT, preferred_element_type=F32) for m in range(2)]
        mx = [jnp.maximum(mx[m], jnp.max(sc[m], axis=-1, keepdims=True)) for m in range(2)]
    e = [jnp.exp(s[m] - mx[m]) for m in range(2)]
    z = [jnp.sum(e[m], axis=-1, keepdims=True) for m in range(2)]
    if has_cache:
        ec = [jnp.exp(sc[m] - mx[m]) for m in range(2)]
        z = [z[m] + jnp.sum(ec[m], axis=-1, keepdims=True) for m in range(2)]
    w0 = 1.0 / z[0]
    w1 = lam / z[1]
    o = jnp.dot((e[0] * w0 - e[1] * w1).astype(BF16), v_ref[...], preferred_element_type=F32)
    if has_cache:
        o = o + jnp.dot((ec[0] * w0 - ec[1] * w1).astype(BF16), vc_ref[...].astype(BF16),
                        preferred_element_type=F32)
    ms = jnp.mean(o * o, axis=-1, keepdims=True)
    o_ref[...] = (o * lax.rsqrt(ms + EPS) * sg_ref[...] * (1.0 - lam_init)).astype(o_ref.dtype)


def _diff_attention(q, k, v, da_lam, subln_g, lam_init, row0, batch, seq, cache=None):
    tq = 256
    qb0, kb0 = row0 // tq, row0 // seq
    in_specs = [pl.BlockSpec((4, DA_DIM), lambda b, h, i: (0, 0)),
                pl.BlockSpec((1, 128), lambda b, h, i: (0, 0)),
                pl.BlockSpec((tq, 128), lambda b, h, i: (qb0 + b * (seq // tq) + i, h)),
                pl.BlockSpec((seq, 128), lambda b, h, i: (kb0 + b, h)),
                pl.BlockSpec((seq, 128), lambda b, h, i: (kb0 + b, h))]
    args = [da_lam, subln_g, q, k, v]
    if cache is not None:
        past = cache[0].shape[1]
        in_specs += [pl.BlockSpec((None, past, 128), lambda b, h, i: (b, 0, h))] * 2
        args += list(cache)
    return pl.pallas_call(
        functools.partial(_attn_body, lam_init=lam_init, has_cache=cache is not None),
        grid=(batch, DA_HEADS, seq // tq),
        in_specs=in_specs,
        out_specs=pl.BlockSpec((tq, 128), lambda b, h, i: (b * (seq // tq) + i, h)),
        out_shape=jax.ShapeDtypeStruct((batch * seq, 512), BF16),
        compiler_params=_cp("arbitrary", "arbitrary", "arbitrary"),
        name="diff_attn_ctx" if cache is not None else "diff_attn",
    )(*args)


def _gla_body(qk_ref, v_ref, r_ref, low_ref, gup_ref, gb_ref, ng_ref, *rest, seq, has_init):
    if has_init:
        s0_ref, o_ref, sf_ref, st_scr, of_scr, ob_scr, la_scr = rest
    else:
        o_ref, sf_ref, st_scr, of_scr, ob_scr, la_scr = rest
    c = GLA_CHUNK
    n = seq // c
    hk = GLA_HEADS * GLA_DK
    low = low_ref[...]
    for d in range(2):
        z = jnp.dot(low, gup_ref[d], precision=HI, preferred_element_type=F32) + gb_ref[d]
        la_scr[d] = (jnp.minimum(z, 0.0) - jnp.log(1.0 + jnp.exp(-jnp.abs(z)))) * (1.0 / GLA_TAU)
    if has_init:
        st_scr[...] = s0_ref[0]
    else:
        st_scr[...] = jnp.zeros(st_scr.shape, F32)
    row = lax.broadcasted_iota(jnp.int32, (c, c), 0)
    col = lax.broadcasted_iota(jnp.int32, (c, c), 1)
    keep = (col <= row, col >= row)
    lane = lax.broadcasted_iota(jnp.int32, (c, 128), 1)
    low_half = lane < GLA_DK

    def chunk(d, r0):
        rows = pl.ds(r0, c)
        qk = qk_ref[rows, :].astype(F32)
        q = qk[:, 0:hk] * (GLA_DK ** -0.5)
        k = qk[:, hk:2 * hk]
        v = v_ref[rows, :]
        g = la_scr[d, rows, :]
        b = jnp.dot(keep[d].astype(F32), g, precision=HI, preferred_element_type=F32)
        b_tot = b[c - 1:c] if d == 0 else b[0:1]
        q_dec = q * jnp.exp(b)
        k_inv = k * jnp.exp(-b)
        k_end = k * jnp.exp(b_tot - b)
        st = st_scr[d]
        st_b = st.astype(BF16)
        outs, news = [], []
        for pair in range(GLA_HEADS // 2):
            ps = slice(pair * 128, (pair + 1) * 128)
            kin = k_inv[:, ps].astype(BF16)
            new = None
            for sub in range(2):
                hh = pair * 2 + sub
                vs = slice(hh * GLA_DV, (hh + 1) * GLA_DV)
                sel = low_half if sub == 0 else jnp.logical_not(low_half)
                qd = jnp.where(sel, q_dec[:, ps], 0.0).astype(BF16)
                ke = jnp.where(sel, k_end[:, ps], 0.0).astype(BF16)
                inter = lax.dot_general(qd, st_b[:, ps], NT, preferred_element_type=F32)
                att = lax.dot_general(qd, kin, NT, preferred_element_type=F32)
                att = jnp.where(keep[d], att, 0.0).astype(BF16)
                outs.append(inter + jnp.dot(att, v[:, vs], preferred_element_type=F32))
                upd = lax.dot_general(v[:, vs], ke, TN, preferred_element_type=F32)
                new = upd if new is None else new + upd
            news.append(new)
        st_scr[d] = st * jnp.exp(b_tot) + jnp.concatenate(news, axis=1)
        return jnp.concatenate(outs, axis=1)

    def step(i, carry):
        rf = pl.multiple_of(i * c, c)
        of_scr[pl.ds(rf, c), :] = chunk(0, rf)
        rb = pl.multiple_of((n - 1 - i) * c, c)
        ob_scr[pl.ds(rb, c), :] = chunk(1, rb)
        return carry

    lax.fori_loop(0, n, step, 0, unroll=2)
    sf_ref[0] = st_scr[...]

    fin = 256

    def finish(i, carry):
        rows = pl.ds(pl.multiple_of(i * fin, fin), fin)
        o = of_scr[rows, :] + ob_scr[rows, :]
        r = r_ref[rows, :].astype(F32)
        gate = r / (1.0 + jnp.exp(-r))
        ng = ng_ref[...]
        for hh in range(GLA_HEADS):
            vs = slice(hh * GLA_DV, (hh + 1) * GLA_DV)
            oh = o[:, vs]
            ms = jnp.mean(oh * oh, axis=-1, keepdims=True)
            o_ref[rows, vs] = (oh * lax.rsqrt(ms + EPS) * ng * gate[:, vs]).astype(BF16)
        return carry

    lax.fori_loop(0, seq // fin, finish, 0)


def _gla(gqk, gv, gr, low, gup, gb, ng, row0, batch, seq, s0=None):
    b0 = row0 // seq
    tokmap = lambda b: (b0 + b, 0)
    hk = GLA_HEADS * GLA_DK
    in_specs = [pl.BlockSpec((seq, 512), tokmap), pl.BlockSpec((seq, 512), tokmap),
                pl.BlockSpec((seq, 512), tokmap), pl.BlockSpec((seq, 128), tokmap),
                pl.BlockSpec((2, 128, hk), lambda b: (0, 0, 0)),
                pl.BlockSpec((2, 1, hk), lambda b: (0, 0, 0)),
                pl.BlockSpec((1, GLA_DV), lambda b: (0, 0))]
    args = [gqk, gv, gr, low, gup, gb, ng]
    if s0 is not None:
        in_specs.append(pl.BlockSpec((1, 2, GLA_DV, hk), lambda b: (b, 0, 0, 0)))
        args.append(s0)
    return pl.pallas_call(
        functools.partial(_gla_body, seq=seq, has_init=s0 is not None),
        grid=(batch,),
        in_specs=in_specs,
        out_specs=[pl.BlockSpec((seq, 512), lambda b: (b, 0)),
                   pl.BlockSpec((1, 2, GLA_DV, hk), lambda b: (b, 0, 0, 0))],
        out_shape=[jax.ShapeDtypeStruct((batch * seq, 512), BF16),
                   jax.ShapeDtypeStruct((batch, 2, GLA_DV, hk), F32)],
        scratch_shapes=[pltpu.VMEM((2, GLA_DV, hk), F32),
                        pltpu.VMEM((seq, 512), F32), pltpu.VMEM((seq, 512), F32),
                        pltpu.VMEM((2, seq, hk), F32)],
        compiler_params=_cp("arbitrary"),
        name="gla_ctx" if s0 is not None else "gla",
    )(*args)


def _outproj0_body(x_ref, mod_ref, oda_ref, og_ref, w_ref, o_ref):
    m = (jnp.dot(oda_ref[...], w_ref[0:512, :], preferred_element_type=F32)
         + jnp.dot(og_ref[...], w_ref[512:1024, :], preferred_element_type=F32))
    o_ref[...] = x_ref[...] + mod_ref[0][:, 2 * D:3 * D] * m


def _outproj0(x, mod, oda, og, w, np_rows, seq_rows):
    n = x.shape[0]
    tm = ROW_TILE
    rowmap = _mod_row_map(np_rows // tm, seq_rows // tm)
    return pl.pallas_call(
        _outproj0_body,
        grid=(n // tm,),
        in_specs=[pl.BlockSpec((tm, D), lambda i: (i, 0)),
                  pl.BlockSpec((1, 1, 6 * D), lambda i: (rowmap(i), 0, 0)),
                  pl.BlockSpec((tm, 512), lambda i: (i, 0)),
                  pl.BlockSpec((tm, 512), lambda i: (i, 0)),
                  pl.BlockSpec((D, D), lambda i: (0, 0))],
        out_specs=pl.BlockSpec((tm, D), lambda i: (i, 0)),
        out_shape=jax.ShapeDtypeStruct((n, D), F32),
        compiler_params=_cp("arbitrary"),
        name="outproj0",
    )(x, mod, oda, og, w)


def _route_body(x_ref, mod_ref, g_ref, wq_ref, sk_ref, ht_o, nsel_o, e1_o, rank_o, e2_o, top_scr, s_scr):
    mod = mod_ref[0]
    h = _norm_mod(x_ref[...], g_ref[...], mod[:, 4 * D:5 * D], mod[:, 3 * D:4 * D])
    ht_o[...] = h.T.astype(BF16)
    q = jnp.dot(h.astype(BF16), wq_ref[...], preferred_element_type=F32).astype(BF16)
    tm = q.shape[0]
    neg = -jnp.inf
    k = PEER_TOPK
    row8 = lax.broadcasted_iota(jnp.int32, (8, tm), 0)
    tiles = [slice(c * 128, (c + 1) * 128) for c in range(tm // 128)]
    for hh in range(PEER_HEADS):
        hs = slice(hh * PEER_NKEYS, (hh + 1) * PEER_NKEYS)
        for t in range(2):
            c0 = (hh * 2 + t) * PEER_NKEYS
            s_scr[t] = lax.dot_general(sk_ref[hh, t], q[:, c0:c0 + PEER_NKEYS], NT,
                                       preferred_element_type=F32)
            for cs in tiles:
                cur = s_scr[t, :, cs]
                rank = jnp.full((PEER_NKEYS, 128), float(k), F32)
                for j in range(k):
                    m = jnp.max(cur, axis=0, keepdims=True)
                    top_scr[t, j:j + 1, cs] = m
                    hit = cur == m
                    if t == 1:
                        rank = jnp.where(hit, float(j), rank)
                    cur = jnp.where(hit, neg, cur)
                if t == 1:
                    rank_o[hs, cs] = rank.astype(BF16)
        t1 = top_scr[0]
        t2 = top_scr[1]
        slabs = [t1[0:1] + t2, t1[1:2] + t2[0:8], t1[2:3] + t2[0:8], t1[3:4] + t2[0:8],
                 t2[0:1] + t1[8:16]]
        for j in range(3):
            slabs.append(jnp.where(row8 >= 4, t2[j:j + 1] + t1[0:8], neg))
        cand = jnp.concatenate(slabs, axis=0)
        top = t1[0:1] + t2[0:1]
        zsum = jnp.zeros_like(top)
        kth = top
        for j in range(k):
            kth = jnp.max(cand, axis=0, keepdims=True)
            zsum = zsum + jnp.exp(kth - top)
            cand = jnp.where(cand == kth, neg, cand)
        zinv = 1.0 / zsum
        for c, cs in enumerate(tiles):
            s1 = s_scr[0, :, cs]
            nsel = jnp.zeros((PEER_NKEYS, 128), F32)
            for j in range(k):
                nsel = jnp.where(s1 + t2[j:j + 1, cs] >= kth[:, cs], float(j + 1), nsel)
            nsel_o[c, hs, :] = nsel
            e1_o[c, hs, :] = jnp.exp(s1 - t1[0:1, cs]) * zinv[:, cs]
            e2_o[hs, cs] = jnp.exp(s_scr[1, :, cs] - t2[0:1, cs]).astype(BF16)


def _route(x, mod, norm_g, wq, sk, layer, np_rows, seq_rows):
    n = x.shape[0]
    tm = ROW_TILE
    rowmap = _mod_row_map(np_rows // tm, seq_rows // tm)
    rt = lambda dt: jax.ShapeDtypeStruct((PEER_HEADS * PEER_NKEYS, n), dt)
    rspec = pl.BlockSpec((PEER_HEADS * PEER_NKEYS, tm), lambda i: (0, i))
    st_ = jax.ShapeDtypeStruct((n // 128, PEER_HEADS * PEER_NKEYS, 128), F32)
    sspec = pl.BlockSpec((tm // 128, PEER_HEADS * PEER_NKEYS, 128), lambda i: (i, 0, 0))
    return pl.pallas_call(
        _route_body,
        grid=(n // tm,),
        in_specs=[pl.BlockSpec((tm, D), lambda i: (i, 0)),
                  pl.BlockSpec((1, 1, 6 * D), lambda i: (rowmap(i), 0, 0)),
                  pl.BlockSpec((1, D), lambda i: (0, 0)),
                  pl.BlockSpec((None, D, 2 * PEER_HEADS * PEER_NKEYS), lambda i: (layer, 0, 0)),
                  pl.BlockSpec((None, PEER_HEADS, 2, PEER_NKEYS, 128), lambda i: (layer, 0, 0, 0, 0))],
        out_specs=[pl.BlockSpec((D, tm), lambda i: (0, i)), sspec, sspec, rspec, rspec],
        out_shape=[jax.ShapeDtypeStruct((D, n), BF16), st_, st_, rt(BF16), rt(BF16)],
        scratch_shapes=[pltpu.VMEM((2, PEER_TOPK, tm), F32), pltpu.VMEM((2, PEER_NKEYS, tm), F32)],
        compiler_params=_cp("arbitrary"),
        name="peer_route",
    )(x, mod, norm_g, wq, sk)


def _peer_body(ht_ref, u_ref, vt_ref, nsel_ref, e1_ref, rank_ref, e2_ref, x_ref, mod_ref, fg_ref,
               o_ref, acc_ref, p_ref, *, final_norm):
    j = pl.program_id(1)

    @pl.when(j == 0)
    def _():
        acc_ref[...] = jnp.zeros(acc_ref.shape, F32)

    nk = PEER_NKEYS
    tok = ht_ref.shape[1]
    na = PEER_EXP // nk
    a0 = pl.multiple_of(j * na, na)
    half = PEER_EXP // 2
    rg = 8
    zero = jnp.zeros((rg, PEER_SUB), BF16)

    def row_bcast(ref, hh, r, ts):
        parts = [jnp.broadcast_to(ref[c, pl.ds(hh * nk + a0, na), :][r:r + 1], (rg, 128))
                 for c in range(ts.start // 128, ts.stop // 128)]
        return jnp.concatenate(parts, axis=1).astype(BF16)

    for t0 in range(0, tok, PEER_SUB):
        ts = slice(t0, t0 + PEER_SUB)
        act = [jnp.dot(u_ref[hf * half:(hf + 1) * half, :], ht_ref[:, ts], preferred_element_type=F32)
               for hf in range(2)]
        for r in range(na):
            gate = [zero] * (nk // rg)
            for hh in range(PEER_HEADS):
                ns = row_bcast(nsel_ref, hh, r, ts)
                e1 = row_bcast(e1_ref, hh, r, ts)
                for g in range(nk // rg):
                    bs = slice(hh * nk + g * rg, hh * nk + (g + 1) * rg)
                    gate[g] = gate[g] + jnp.where(rank_ref[bs, ts] < ns, e2_ref[bs, ts], zero) * e1
            for g in range(nk // rg):
                lo = r * nk + g * rg
                a = act[lo // half][lo % half:lo % half + rg, :]
                p_ref[lo:lo + rg, ts] = _gelu(a).astype(BF16) * gate[g]
        for hf in range(2):
            ds = slice(hf * (D // 2), (hf + 1) * (D // 2))
            acc_ref[ds, ts] += jnp.dot(vt_ref[ds, :], p_ref[:, ts], preferred_element_type=F32)

    @pl.when(j == pl.num_programs(1) - 1)
    def _():
        y = x_ref[...] + mod_ref[0][:, 5 * D:6 * D] * acc_ref[...].T
        if final_norm:
            ms = jnp.mean(y * y, axis=-1, keepdims=True)
            y = y * lax.rsqrt(ms + EPS) * fg_ref[...]
        o_ref[...] = y


def _peer_dense(x, mod, ht, u, vt, layer, nsel, e1, rank, e2, fg, np_rows, seq_rows, final_norm):
    n = x.shape[0]
    tk, ex = PEER_TOK, PEER_EXP
    rowmap = _mod_row_map(np_rows // tk, seq_rows // tk)
    rspec = pl.BlockSpec((PEER_HEADS * PEER_NKEYS, tk), lambda i, j: (0, i))
    sspec = pl.BlockSpec((tk // 128, PEER_HEADS * PEER_NKEYS, 128), lambda i, j: (i, 0, 0))
    return pl.pallas_call(
        functools.partial(_peer_body, final_norm=final_norm),
        grid=(n // tk, u.shape[1] // ex),
        in_specs=[pl.BlockSpec((D, tk), lambda i, j: (0, i)),
                  pl.BlockSpec((None, ex, D), lambda i, j: (layer, j, 0)),
                  pl.BlockSpec((None, D, ex), lambda i, j: (layer, 0, j)),
                  sspec, sspec, rspec, rspec,
                  pl.BlockSpec((tk, D), lambda i, j: (i, 0)),
                  pl.BlockSpec((1, 1, 6 * D), lambda i, j: (rowmap(i), 0, 0)),
                  pl.BlockSpec((1, D), lambda i, j: (0, 0))],
        out_specs=pl.BlockSpec((tk, D), lambda i, j: (i, 0)),
        out_shape=jax.ShapeDtypeStruct((n, D), F32),
        scratch_shapes=[pltpu.VMEM((D, tk), F32), pltpu.VMEM((ex, tk), BF16)],
        compiler_params=_cp("arbitrary", "arbitrary"),
        name="peer_dense",
    )(ht, u, vt, nsel, e1, rank, e2, x, mod, fg)


def _peer_tables(w_query, sub_keys, u, v):
    return w_query.astype(BF16), sub_keys.astype(BF16), u.astype(BF16), v.transpose(0, 2, 1).astype(BF16)


def _peer(x, mod, norm_g, tables, layer, fg, np_rows, seq_rows, final_norm):
    wq, sk, u, vt = tables
    ht, nsel, e1, rank, e2 = _route(x, mod, norm_g, wq, sk, layer, np_rows, seq_rows)
    return _peer_dense(x, mod, ht, u, vt, layer, nsel, e1, rank, e2, fg, np_rows, seq_rows, final_norm)


def _window_of(lane, i):
    return (lane >> 4) == i


def _s5_in_body(x_ref, mod_ref, g_ref, w_ref, o_ref, u_scr):
    mod = mod_ref[0]
    h = _norm_mod(x_ref[...], g_ref[...], mod[:, D:2 * D], mod[:, 0:D]).astype(BF16)
    u = jnp.dot(h, w_ref[...], preferred_element_type=F32)
    cs, per = S5_CHUNK, 128 // S5_CH
    nck = u.shape[0] // cs
    for j in range(D // 128):
        u_scr[j] = u[:, j * 128:(j + 1) * 128]
    lane = lax.broadcasted_iota(jnp.int32, (nck, 128), 1)
    for j in range(D // 128):
        xs = [u_scr[j, pl.ds(s, nck, stride=cs), :] for s in range(cs)]
        for gp in range(per):
            for half in range(cs // per):
                acc = None
                for s8 in range(per):
                    piece = xs[half * per + s8]
                    shift = ((s8 - gp) * S5_CH) % 128
                    if shift:
                        piece = pltpu.roll(piece, shift, 1)
                    acc = piece if acc is None else jnp.where(_window_of(lane, s8), piece, acc)
                o_ref[j * per + gp, :, half * 128:(half + 1) * 128] = acc.astype(BF16)


def _s5_in(x, mod, norm_g, w, np_rows, seq_rows):
    n = x.shape[0]
    tm = ROW_TILE
    rowmap = _mod_row_map(np_rows // tm, seq_rows // tm)
    nck = tm // S5_CHUNK
    return pl.pallas_call(
        _s5_in_body,
        grid=(n // tm,),
        in_specs=[pl.BlockSpec((tm, D), lambda i: (i, 0)),
                  pl.BlockSpec((1, 1, 6 * D), lambda i: (rowmap(i), 0, 0)),
                  pl.BlockSpec((1, D), lambda i: (0, 0)),
                  pl.BlockSpec((D, D), lambda i: (0, 0))],
        out_specs=pl.BlockSpec((S5_GROUPS, nck, S5_CHUNK * S5_CH), lambda i: (0, i, 0)),
        out_shape=jax.ShapeDtypeStruct((S5_GROUPS, n // S5_CHUNK, S5_CHUNK * S5_CH), BF16),
        scratch_shapes=[pltpu.VMEM((D // 128, tm, 128), F32)],
        compiler_params=_cp("arbitrary"),
        name="s5_in",
    )(x, mod, norm_g, w)


def _s5_core_body(u_ref, m_ref, win_ref, cout_ref, pw_ref, *rest, kseq, has_init, emit_states):
    rest = list(rest)
    h0_ref = rest.pop(0) if has_init else None
    y_ref = rest.pop(0)
    sf_ref = rest.pop(0) if emit_states else None
    kb = u_ref.shape[1]
    row = lax.broadcasted_iota(jnp.int32, (kb, 128), 0)
    kk = row & (kseq - 1)

    def cmul(a1, a2, x):
        return a1 * x + a2 * pltpu.roll(x, 64, 1)

    for gi in range(u_ref.shape[0]):
        u = u_ref[gi]
        y = jnp.dot(u, m_ref[gi], preferred_element_type=F32)
        for d in range(2):
            s = jnp.dot(u, win_ref[gi, d], preferred_element_type=F32)
            edge = (kk == 0) if d == 0 else (kk == kseq - 1)
            if has_init:
                h0 = h0_ref[gi, d]
                s = s + jnp.where(edge, cmul(pw_ref[gi, d, 0, 0:1], pw_ref[gi, d, 0, 1:2], h0), 0.0)
            sw = pltpu.roll(s, 64, 1)
            step, lvl = 1, 0
            while step < kseq:
                shift = step if d == 0 else kb - step
                ok = (kk >= step) if d == 0 else (kk < kseq - step)
                a1, a2 = pw_ref[gi, d, lvl, 0:1], pw_ref[gi, d, lvl, 1:2]
                sh, swh = pltpu.roll(s, shift, 0), pltpu.roll(sw, shift, 0)
                s = s + jnp.where(ok, a1 * sh + a2 * swh, 0.0)
                sw = sw + jnp.where(ok, a1 * swh - a2 * sh, 0.0)
                step, lvl = step * 2, lvl + 1
            if emit_states:
                sf_ref[gi, d] = s
            hin = pltpu.roll(s, 1, 0) if d == 0 else pltpu.roll(s, kb - 1, 0)
            if has_init:
                hin = jnp.where(edge, h0, hin)
            else:
                hin = jnp.where(edge, 0.0, hin)
            y = y + jnp.dot(hin.astype(BF16), cout_ref[gi, d], preferred_element_type=F32)
        y_ref[gi] = _gelu(y).astype(BF16)


def _s5_core(ug, mats, row0, rows, kseq, h0=None, emit_states=False):
    m, win, cout, pw = mats
    kb = 128
    b0 = row0 // kb
    nlv = pw.shape[2]
    gb = S5_GROUP_BLOCK
    in_specs = [pl.BlockSpec((gb, kb, 256), lambda g, i: (g, b0 + i, 0)),
                pl.BlockSpec((gb, 256, 256), lambda g, i: (g, 0, 0)),
                pl.BlockSpec((gb, 2, 256, 128), lambda g, i: (g, 0, 0, 0)),
                pl.BlockSpec((gb, 2, 128, 256), lambda g, i: (g, 0, 0, 0)),
                pl.BlockSpec((gb, 2, nlv, 2, 128), lambda g, i: (g, 0, 0, 0, 0))]
    args = [ug, m, win, cout, pw]
    if h0 is not None:
        assert kseq == kb
        in_specs.append(pl.BlockSpec((gb, None, 2, 1, 128), lambda g, i: (g, i, 0, 0, 0)))
        args.append(h0)
    out_specs = [pl.BlockSpec((gb, kb, 256), lambda g, i: (g, i, 0))]
    out_shape = [jax.ShapeDtypeStruct((S5_GROUPS, rows, 256), BF16)]
    if emit_states:
        out_specs.append(pl.BlockSpec((gb, 2, kb, 128), lambda g, i: (g, 0, i, 0)))
        out_shape.append(jax.ShapeDtypeStruct((S5_GROUPS, 2, rows, 128), F32))
    return pl.pallas_call(
        functools.partial(_s5_core_body, kseq=kseq, has_init=h0 is not None, emit_states=emit_states),
        grid=(S5_GROUPS // gb, rows // kb),
        in_specs=in_specs,
        out_specs=out_specs,
        out_shape=out_shape,
        compiler_params=_cp("arbitrary", "arbitrary"),
        name="s5_core_ctx" if h0 is not None else "s5_core",
    )(*args)


def _s5_out_body(x_ref, mod_ref, y_ref, w_ref, o_ref, y_scr):
    cs, per = S5_CHUNK, 128 // S5_CH
    nck = y_ref.shape[1]
    lane = lax.broadcasted_iota(jnp.int32, (nck, 128), 1)
    for j in range(D // 128):
        src = [[y_ref[j * per + gp, :, half * 128:(half + 1) * 128].astype(F32) for half in range(cs // per)]
               for gp in range(per)]
        for t in range(cs):
            half, t8 = divmod(t, per)
            acc = None
            for gp in range(per):
                piece = src[gp][half]
                shift = ((gp - t8) * S5_CH) % 128
                if shift:
                    piece = pltpu.roll(piece, shift, 1)
                acc = piece if acc is None else jnp.where(_window_of(lane, gp), piece, acc)
            y_scr[j, pl.ds(t, nck, stride=cs), :] = acc
    y = jnp.concatenate([y_scr[j] for j in range(D // 128)], axis=1).astype(BF16)
    zz = jnp.dot(y, w_ref[...], preferred_element_type=F32)
    za = zz[:, 0:D]
    zb = zz[:, D:2 * D]
    o_ref[...] = x_ref[...] + mod_ref[0][:, 2 * D:3 * D] * (za / (1.0 + jnp.exp(-zb)))


def _s5_out(x, mod, y, w, np_rows, seq_rows):
    n = x.shape[0]
    tm = ROW_TILE
    rowmap = _mod_row_map(np_rows // tm, seq_rows // tm)
    return pl.pallas_call(
        _s5_out_body,
        grid=(n // tm,),
        in_specs=[pl.BlockSpec((tm, D), lambda i: (i, 0)),
                  pl.BlockSpec((1, 1, 6 * D), lambda i: (rowmap(i), 0, 0)),
                  pl.BlockSpec((S5_GROUPS, tm // S5_CHUNK, S5_CHUNK * S5_CH), lambda i: (0, i, 0)),
                  pl.BlockSpec((D, 2 * D), lambda i: (0, 0))],
        out_specs=pl.BlockSpec((tm, D), lambda i: (i, 0)),
        out_shape=jax.ShapeDtypeStruct((n, D), F32),
        scratch_shapes=[pltpu.VMEM((D // 128, tm, 128), F32)],
        compiler_params=_cp("arbitrary"),
        name="s5_out",
    )(x, mod, y, w)


def _s5_matrices(a_re, a_im, log_dt, b_re, b_im, c_re, c_im, d_skip):
    cs = S5_CHUNK
    dt = jnp.exp(log_dt)[..., None]
    lr, li = a_re * dt, a_im * dt

    def lam_pow(tau):
        mag = jnp.exp(lr[:, :, None, :] * tau[:, None])
        ang = li[:, :, None, :] * tau[:, None]
        return mag * jnp.cos(ang), mag * jnp.sin(ang)

    l1r, l1i = jnp.exp(lr) * jnp.cos(li), jnp.exp(lr) * jnp.sin(li)
    den = a_re * a_re + a_im * a_im
    cr = ((l1r - 1.0) * a_re + l1i * a_im) / den
    ci = (l1i * a_re - (l1r - 1.0) * a_im) / den
    bt_re = b_re.transpose(0, 1, 3, 2)
    bt_im = b_im.transpose(0, 1, 3, 2)
    bbr = cr[:, :, None, :] * bt_re - ci[:, :, None, :] * bt_im
    bbi = cr[:, :, None, :] * bt_im + ci[:, :, None, :] * bt_re

    pr, pi = lam_pow(jnp.arange(cs + 1, dtype=F32))
    clr = c_re[:, :, None] * pr[:, :, :, None, :] - c_im[:, :, None] * pi[:, :, :, None, :]
    cli = c_re[:, :, None] * pi[:, :, :, None, :] + c_im[:, :, None] * pr[:, :, :, None, :]
    cl = jnp.concatenate([clr[:, :, :cs], -cli[:, :, :cs]], axis=-1).reshape(2, S5_GROUPS, cs * S5_CH, 2 * S5_P)
    bb = jnp.concatenate([bbr, bbi], axis=-1)
    kern = jnp.einsum('dgeq,dgjq->dgej', bb, cl, precision=HI)
    w = cs * S5_CH
    kern_b = kern[1].reshape(S5_GROUPS, S5_CH, cs, S5_CH)[:, :, ::-1].reshape(S5_GROUPS, S5_CH, w)
    zpad = lambda a, lo, hi: jnp.pad(a, ((0, 0), (0, 0), (lo, hi)))
    mf = jnp.stack([zpad(kern[0][:, :, :w - S5_CH * s], S5_CH * s, 0) for s in range(cs)], axis=1)
    mb = jnp.stack([zpad(kern_b[:, :, S5_CH * (cs - 1 - s):], 0, S5_CH * (cs - 1 - s)) for s in range(cs)], axis=1)
    m = (mf + mb).reshape(S5_GROUPS, w, w)
    m = m + jnp.eye(cs * S5_CH, dtype=F32) * jnp.tile(d_skip.reshape(S5_GROUPS, 1, S5_CH), (1, 1, cs))

    def win_dir(d, powers):
        qr_, qi_ = pr[d][:, powers][:, :, None, :], pi[d][:, powers][:, :, None, :]
        wr = qr_ * bbr[d][:, None] - qi_ * bbi[d][:, None]
        wi = qr_ * bbi[d][:, None] + qi_ * bbr[d][:, None]
        return jnp.concatenate([wr, wi], axis=-1).reshape(S5_GROUPS, cs * S5_CH, 2 * S5_P)

    win = jnp.stack([win_dir(0, jnp.arange(cs - 1, -1, -1)), win_dir(1, jnp.arange(cs))], axis=1)

    def cout_dir(d, powers):
        z = jnp.concatenate([clr[d][:, powers], -cli[d][:, powers]], axis=-1)
        return z.reshape(S5_GROUPS, cs * S5_CH, 2 * S5_P).transpose(0, 2, 1)

    cout = jnp.stack([cout_dir(0, jnp.arange(1, cs + 1)), cout_dir(1, jnp.arange(cs, 0, -1))], axis=1)

    qr, qi = lam_pow(cs * (2.0 ** jnp.arange(7, dtype=F32)))
    a1 = jnp.concatenate([qr, qr], axis=-1)
    a2 = jnp.concatenate([-qi, qi], axis=-1)
    pw = jnp.stack([a1, a2], axis=3).transpose(1, 0, 2, 3, 4)
    return m.astype(BF16), win.astype(BF16), cout.astype(BF16), pw


def _rope_tables(dec_seq):
    rows = dec_seq // GRID_W
    row_id = jnp.repeat(jnp.arange(rows), GRID_W).astype(F32)
    col_id = jnp.tile(jnp.arange(GRID_W), rows).astype(F32)
    quarter = DA_DIM // 4
    inv = ROPE_BASE ** (-jnp.arange(quarter, dtype=F32) / quarter)
    ang_r = row_id[:, None] * inv
    ang_c = col_id[:, None] * inv
    ang = jnp.concatenate([ang_r, ang_r, ang_c, ang_c], axis=-1)
    ang = jnp.tile(ang, (1, 2 * DA_HEADS))
    pad = jnp.zeros((ROW_TILE, ang.shape[1]), F32)
    ang = jnp.concatenate([pad, ang], axis=0)
    return jnp.cos(ang), jnp.sin(ang)


def kernel(x_prompt, x_sample, cache_da_k, cache_da_v, state_gla, state_s5, c, c_ctx,
           ada_w, ada_b, norm_g, final_norm_g, mix0_w_in, mix0_w_out, da_lam, da_subln_g,
           gla_gate_up, gla_gate_bias, gla_norm_g, s5_w_in, s5_a_re, s5_a_im, s5_log_dt,
           s5_b_re, s5_b_im, s5_c_re, s5_c_im, s5_d, s5_w_out,
           peer_w_query, peer_sub_keys, peer_u, peer_v):
    bp, lp, _ = x_prompt.shape
    bs, ls, _ = x_sample.shape
    npr, nsr = bp * lp, bs * ls
    assert npr % ls == 0 and ls % ROW_TILE == 0 and lp % ROW_TILE == 0
    x = jnp.concatenate([x_prompt.reshape(npr, D), x_sample.reshape(nsr, D)], axis=0)

    cond = jnp.concatenate([c_ctx[None], c, jnp.zeros((15 - bs, D), F32)], axis=0)
    mod_all = _ada_mod(cond, ada_w, ada_b)
    mods = [mod_all[l].reshape(16, 1, 6 * D) for l in range(mod_all.shape[0])]
    fg = final_norm_g.reshape(1, D)

    lam_init = 0.8 - 0.6 * math.exp(-0.3 * 0)
    cos_t, sin_t = _rope_tables(ls)
    w_in = jnp.pad(mix0_w_in[0], ((0, 0), (0, IN0_PAD - mix0_w_in.shape[2]))).astype(BF16)
    q, kb, vb, k32, v32, gqk, gv, gr, low = _inproj0(
        x, mods[0], norm_g[0, 0].reshape(1, D), w_in, cos_t, sin_t, npr, ls)
    sub_g = da_subln_g[0].reshape(1, 2 * DA_DIM)
    past = cache_da_k.shape[2]
    o_da = jnp.concatenate([
        _diff_attention(q, kb, vb, da_lam[0], sub_g, lam_init, 0, bp, lp),
        _diff_attention(q, kb, vb, da_lam[0], sub_g, lam_init, npr, bs, ls,
                        cache=(cache_da_k[:, 0].reshape(bs, past, 512), cache_da_v[:, 0].reshape(bs, past, 512))),
    ], axis=0)
    hk = GLA_HEADS * GLA_DK
    gup = jnp.zeros((2, 128, hk), F32)
    for d in range(2):
        gup = gup.at[d, d * GLA_RANK:(d + 1) * GLA_RANK].set(gla_gate_up[0, d])
    gbias = gla_gate_bias[0].reshape(2, 1, hk)
    ng = gla_norm_g[0].reshape(1, GLA_DV)
    s0 = state_gla[:, 0].transpose(0, 1, 4, 2, 3).reshape(bs, 2, GLA_DV, hk)
    og_p, st_p = _gla(gqk, gv, gr, low, gup, gbias, ng, 0, bp, lp)
    og_s, _ = _gla(gqk, gv, gr, low, gup, gbias, ng, npr, bs, ls, s0=s0)
    o_g = jnp.concatenate([og_p, og_s], axis=0)
    x = _outproj0(x, mods[0], o_da, o_g, mix0_w_out[0].astype(BF16), npr, ls)
    tables = _peer_tables(peer_w_query, peer_sub_keys, peer_u, peer_v)
    x = _peer(x, mods[0], norm_g[0, 1].reshape(1, D), tables, 0, fg, npr, ls, final_norm=False)

    ug = _s5_in(x, mods[1], norm_g[1, 0].reshape(1, D), s5_w_in[0].astype(BF16), npr, ls)
    cs = S5_CHUNK
    mats = _s5_matrices(s5_a_re[0], s5_a_im[0], s5_log_dt[0], s5_b_re[0], s5_b_im[0],
                        s5_c_re[0], s5_c_im[0], s5_d[0])
    h0 = state_s5[:, 0]
    h0 = h0.transpose(2, 0, 1, 4, 3).reshape(S5_GROUPS, bs, 2, 1, 2 * S5_P)
    y_p, sf = _s5_core(ug, mats, 0, npr // cs, lp // cs, emit_states=True)
    (y_s,) = _s5_core(ug, mats, npr // cs, nsr // cs, ls // cs, h0=h0)
    yg = jnp.concatenate([y_p, y_s], axis=1)
    x = _s5_out(x, mods[1], yg, s5_w_out[0].astype(BF16), npr, ls)
    x = _peer(x, mods[1], norm_g[1, 1].reshape(1, D), tables, 1, fg, npr, ls, final_norm=True)

    y_prompt = x[:npr].reshape(bp, lp, D)
    y_sample = x[npr:].reshape(bs, ls, D)
    new_k = k32[:npr].reshape(bp, 1, lp, 2 * DA_HEADS, DA_DIM)
    new_v = v32[:npr].reshape(bp, 1, lp, DA_HEADS, 2 * DA_DIM)
    new_gla = st_p.reshape(bp, 2, GLA_DV, GLA_HEADS, GLA_DK).transpose(0, 1, 3, 4, 2)[:, None]
    kc = lp // cs
    sf = sf.reshape(S5_GROUPS, 2, bp, kc, 2, S5_P)
    fin = jnp.stack([sf[:, 0, :, kc - 1], sf[:, 1, :, 0]], axis=1)
    new_s5 = fin.transpose(2, 1, 0, 4, 3)[:, None]
    return (y_prompt, y_sample, new_k, new_v, new_gla, new_s5)
```

```python
import functools
import math

import jax
import jax.numpy as jnp
from jax import lax
from jax.experimental import pallas as pl
from jax.experimental.pallas import tpu as pltpu

F32 = jnp.float32
BF16 = jnp.bfloat16
HI = lax.Precision.HIGHEST

D = 1024
EPS = 1e-6
ROPE_BASE = 10000.0
GRID_W = 64
DA_HEADS = 4
DA_DIM = 64
GLA_HEADS = 4
GLA_DK = 64
GLA_DV = 128
GLA_RANK = 16
GLA_TAU = 16.0
GLA_CHUNK = 64
S5_CH = 16
S5_GROUPS = D // S5_CH
S5_P = 64
S5_CHUNK = 16
S5_GROUP_BLOCK = 4
PEER_HEADS = 8
PEER_NKEYS = 128
PEER_TOPK = 16
IN0_PAD = 3200

ROW_TILE = 256
PEER_TOK = 512
PEER_EXP = 1024
PEER_SUB = 256
VMEM_LIMIT = 48 * 1024 * 1024

NT = (((1,), (1,)), ((), ()))
TN = (((0,), (0,)), ((), ()))


def _cp(*sem, flags=None):
    return pltpu.CompilerParams(dimension_semantics=sem, vmem_limit_bytes=VMEM_LIMIT, flags=flags)


def _norm_mod(x, g, sc, sh):
    ms = jnp.mean(x * x, axis=-1, keepdims=True)
    return x * lax.rsqrt(ms + EPS) * g * (1.0 + sc) + sh


_GELU_C2 = -2.0 * 0.7978845608028654 * 1.4426950408889634
_GELU_C1 = _GELU_C2 * 0.044715


def _gelu(x):
    return x / (1.0 + jnp.exp2(x * (_GELU_C1 * (x * x) + _GELU_C2)))


def _mod_row_map(np_blocks, seq_blocks):
    def f(i):
        return jnp.where(i < np_blocks, 0, 1 + (i - np_blocks) // seq_blocks)
    return f


def _ada_body(c_ref, w_ref, b_ref, o_ref):
    c = c_ref[...]
    s = c / (1.0 + jnp.exp(-c))
    o_ref[0] = jnp.dot(s, w_ref[0], precision=HI, preferred_element_type=F32) + b_ref[0]


def _ada_mod(cond, ada_w, ada_b):
    depth, _, n = ada_w.shape
    rows = cond.shape[0]
    tn = 1536
    return pl.pallas_call(
        _ada_body,
        grid=(depth, n // tn),
        in_specs=[pl.BlockSpec((rows, D), lambda l, j: (0, 0)),
                  pl.BlockSpec((1, D, tn), lambda l, j: (l, 0, j)),
                  pl.BlockSpec((1, 1, tn), lambda l, j: (l, 0, j))],
        out_specs=pl.BlockSpec((1, rows, tn), lambda l, j: (l, 0, j)),
        out_shape=jax.ShapeDtypeStruct((depth, rows, n), F32),
        compiler_params=_cp("arbitrary", "arbitrary"),
        name="ada_mod",
    )(cond, ada_w, ada_b.reshape(depth, 1, n))


def _inproj0_body(x_ref, mod_ref, g_ref, w_ref, cos_ref, sin_ref,
                  q_o, kb_o, vb_o, k32_o, v32_o, gqk_o, gv_o, gr_o, low_o):
    mod = mod_ref[0]
    h = _norm_mod(x_ref[...], g_ref[...], mod[:, D:2 * D], mod[:, 0:D]).astype(BF16)
    y = jnp.dot(h, w_ref[...], preferred_element_type=F32)
    cos = cos_ref[...]
    sin = sin_ref[...]
    lane = lax.broadcasted_iota(jnp.int32, cos.shape, 1)
    first = (lane & 31) < 16

    def rope(t):
        rot = jnp.where(first, -pltpu.roll(t, 512 - 16, 1), pltpu.roll(t, 16, 1))
        return t * cos + rot * sin

    q = rope(y[:, 0:512])
    k = rope(y[:, 512:1024])
    v = y[:, 1024:1536]
    q_o[...] = q.astype(BF16)
    kb_o[...] = k.astype(BF16)
    vb_o[...] = v.astype(BF16)
    k32_o[...] = k
    v32_o[...] = v
    gqk_o[...] = y[:, 1536:2048].astype(BF16)
    gv_o[...] = y[:, 2048:2560].astype(BF16)
    gr_o[...] = y[:, 2560:3072].astype(BF16)
    low_o[...] = y[:, 3072:3200]


def _inproj0(x, mod, norm_g, w, cos_t, sin_t, np_rows, seq_rows):
    n = x.shape[0]
    tm = ROW_TILE
    npb, sqb = np_rows // tm, seq_rows // tm
    rowmap = _mod_row_map(npb, sqb)
    posmap = lambda i: (jnp.where(i < npb, 0, 1 + (i - npb) % sqb), 0)
    tok = lambda w_: pl.BlockSpec((tm, w_), lambda i: (i, 0))
    outs = [(512, BF16), (512, BF16), (512, BF16), (512, F32), (512, F32),
            (512, BF16), (512, BF16), (512, BF16), (128, F32)]
    return pl.pallas_call(
        _inproj0_body,
        grid=(n // tm,),
        in_specs=[tok(D),
                  pl.BlockSpec((1, 1, 6 * D), lambda i: (rowmap(i), 0, 0)),
                  pl.BlockSpec((1, D), lambda i: (0, 0)),
                  pl.BlockSpec((D, IN0_PAD), lambda i: (0, 0)),
                  pl.BlockSpec((tm, 512), posmap),
                  pl.BlockSpec((tm, 512), posmap)],
        out_specs=[tok(w_) for w_, _ in outs],
        out_shape=[jax.ShapeDtypeStruct((n, w_), dt) for w_, dt in outs],
        compiler_params=_cp("arbitrary"),
        name="inproj0",
    )(x, mod, norm_g, w, cos_t, sin_t)


def _attn_body(lam_ref, sg_ref, q_ref, k_ref, v_ref, *rest, lam_init, has_cache):
    if has_cache:
        kc_ref, vc_ref, o_ref = rest
    else:
        (o_ref,) = rest
    lf = lam_ref[...]
    lam = (jnp.exp(jnp.sum(lf[0:1] * lf[1:2], axis=-1, keepdims=True))
           - jnp.exp(jnp.sum(lf[2:3] * lf[3:4], axis=-1, keepdims=True)) + lam_init)
    q = q_ref[...] * (DA_DIM ** -0.5)
    lane = lax.broadcasted_iota(jnp.int32, q.shape, 1)
    zero = jnp.zeros_like(q)
    qm = (jnp.where(lane < DA_DIM, q, zero), jnp.where(lane >= DA_DIM, q, zero))
    k = k_ref[...]
    s = [lax.dot_general(qm[m], k, NT, preferred_element_type=F32) for m in range(2)]
    mx = [jnp.max(s[m], axis=-1, keepdims=True) for m in range(2)]
    if has_cache:
        kc = kc_ref[...].astype(BF16)
        sc = [lax.dot_general(qm[m], kc, NT, preferred_element_type=F32) for m in range(2)]
        mx = [jnp.maximum(mx[m], jnp.max(sc[m], axis=-1, keepdims=True)) for m in range(2)]
    e = [jnp.exp(s[m] - mx[m]) for m in range(2)]
    z = [jnp.sum(e[m], axis=-1, keepdims=True) for m in range(2)]
    if has_cache:
        ec = [jnp.exp(sc[m] - mx[m]) for m in range(2)]
        z = [z[m] + jnp.sum(ec[m], axis=-1, keepdims=True) for m in range(2)]
    w0 = 1.0 / z[0]
    w1 = lam / z[1]
    o = jnp.dot((e[0] * w0 - e[1] * w1).astype(BF16), v_ref[...], preferred_element_type=F32)
    if has_cache:
        o = o + jnp.dot((ec[0] * w0 - ec[1] * w1).astype(BF16), vc_ref[...].astype(BF16),
                        preferred_element_type=F32)
    ms = jnp.mean(o * o, axis=-1, keepdims=True)
    o_ref[...] = (o * lax.rsqrt(ms + EPS) * sg_ref[...] * (1.0 - lam_init)).astype(o_ref.dtype)


def _diff_attention(q, k, v, da_lam, subln_g, lam_init, row0, batch, seq, cache=None):
    tq = 256
    qb0, kb0 = row0 // tq, row0 // seq
    in_specs = [pl.BlockSpec((4, DA_DIM), lambda b, h, i: (0, 0)),
                pl.BlockSpec((1, 128), lambda b, h, i: (0, 0)),
                pl.BlockSpec((tq, 128), lambda b, h, i: (qb0 + b * (seq // tq) + i, h)),
                pl.BlockSpec((seq, 128), lambda b, h, i: (kb0 + b, h)),
                pl.BlockSpec((seq, 128), lambda b, h, i: (kb0 + b, h))]
    args = [da_lam, subln_g, q, k, v]
    if cache is not None:
        past = cache[0].shape[1]
        in_specs += [pl.BlockSpec((None, past, 128), lambda b, h, i: (b, 0, h))] * 2
        args += list(cache)
    return pl.pallas_call(
        functools.partial(_attn_body, lam_init=lam_init, has_cache=cache is not None),
        grid=(batch, DA_HEADS, seq // tq),
        in_specs=in_specs,
        out_specs=pl.BlockSpec((tq, 128), lambda b, h, i: (b * (seq // tq) + i, h)),
        out_shape=jax.ShapeDtypeStruct((batch * seq, 512), BF16),
        compiler_params=_cp("arbitrary", "arbitrary", "arbitrary"),
        name="diff_attn_ctx" if cache is not None else "diff_attn",
    )(*args)


def _gla_body(qk_ref, v_ref, r_ref, low_ref, gup_ref, gb_ref, ng_ref, *rest, seq, has_init):
    if has_init:
        s0_ref, o_ref, sf_ref, st_scr, of_scr, ob_scr, la_scr = rest
    else:
        o_ref, sf_ref, st_scr, of_scr, ob_scr, la_scr = rest
    c = GLA_CHUNK
    n = seq // c
    hk = GLA_HEADS * GLA_DK
    low = low_ref[...]
    for d in range(2):
        z = jnp.dot(low, gup_ref[d], precision=HI, preferred_element_type=F32) + gb_ref[d]
        la_scr[d] = (jnp.minimum(z, 0.0) - jnp.log(1.0 + jnp.exp(-jnp.abs(z)))) * (1.0 / GLA_TAU)
    if has_init:
        st_scr[...] = s0_ref[0]
    else:
        st_scr[...] = jnp.zeros(st_scr.shape, F32)
    row = lax.broadcasted_iota(jnp.int32, (2 * c, 2 * c), 0)
    col = lax.broadcasted_iota(jnp.int32, (2 * c, 2 * c), 1)
    fwd_blk = jnp.logical_and(row < c, col <= row)
    bwd_blk = jnp.logical_and(jnp.logical_and(row >= c, col >= c), col >= row)
    keep = jnp.logical_or(fwd_blk, bwd_blk)
    keep_f = keep.astype(F32)
    lane = lax.broadcasted_iota(jnp.int32, (2 * c, 128), 1)
    low_half = lane < GLA_DK
    top_rows = lax.broadcasted_iota(jnp.int32, (2 * c, hk), 0) < c

    def step(i, carry):
        rf = pl.ds(pl.multiple_of(i * c, c), c)
        rb = pl.ds(pl.multiple_of((n - 1 - i) * c, c), c)
        qk = jnp.concatenate([qk_ref[rf, :], qk_ref[rb, :]], axis=0).astype(F32)
        q = qk[:, 0:hk] * (GLA_DK ** -0.5)
        k = qk[:, hk:2 * hk]
        v = jnp.concatenate([v_ref[rf, :], v_ref[rb, :]], axis=0)
        g = jnp.concatenate([la_scr[0, rf, :], la_scr[1, rb, :]], axis=0)
        b = jnp.dot(keep_f, g, precision=HI, preferred_element_type=F32)
        b_tot = (b[c - 1:c], b[c:c + 1])
        q_dec = q * jnp.exp(b)
        k_inv = k * jnp.exp(-b)
        k_end = k * jnp.exp(jnp.where(top_rows, b_tot[0], b_tot[1]) - b)
        st = (st_scr[0], st_scr[1])
        st_b = (st[0].astype(BF16), st[1].astype(BF16))
        outs, news = [], ([], [])
        for pair in range(GLA_HEADS // 2):
            ps = slice(pair * 128, (pair + 1) * 128)
            kin = k_inv[:, ps].astype(BF16)
            new = [None, None]
            for sub in range(2):
                hh = pair * 2 + sub
                vs = slice(hh * GLA_DV, (hh + 1) * GLA_DV)
                sel = low_half if sub == 0 else jnp.logical_not(low_half)
                qd = jnp.where(sel, q_dec[:, ps], 0.0).astype(BF16)
                ke = jnp.where(sel, k_end[:, ps], 0.0).astype(BF16)
                att = lax.dot_general(qd, kin, NT, preferred_element_type=F32)
                att = jnp.where(keep, att, 0.0).astype(BF16)
                intra = jnp.dot(att, v[:, vs], preferred_element_type=F32)
                inter = jnp.concatenate(
                    [lax.dot_general(qd[d * c:(d + 1) * c], st_b[d][:, ps], NT, preferred_element_type=F32)
                     for d in range(2)], axis=0)
                outs.append(inter + intra)
                for d in range(2):
                    upd = lax.dot_general(v[d * c:(d + 1) * c, vs], ke[d * c:(d + 1) * c], TN,
                                          preferred_element_type=F32)
                    new[d] = upd if new[d] is None else new[d] + upd
            for d in range(2):
                news[d].append(new[d])
        for d in range(2):
            st_scr[d] = st[d] * jnp.exp(b_tot[d]) + jnp.concatenate(news[d], axis=1)
        o = jnp.concatenate(outs, axis=1)
        of_scr[rf, :] = o[0:c]
        ob_scr[rb, :] = o[c:2 * c]
        return carry

    lax.fori_loop(0, n, step, 0)
    sf_ref[0] = st_scr[...]

    fin = 256

    def finish(i, carry):
        rows = pl.ds(pl.multiple_of(i * fin, fin), fin)
        o = of_scr[rows, :] + ob_scr[rows, :]
        r = r_ref[rows, :].astype(F32)
        gate = r / (1.0 + jnp.exp(-r))
        ng = ng_ref[...]
        for hh in range(GLA_HEADS):
            vs = slice(hh * GLA_DV, (hh + 1) * GLA_DV)
            oh = o[:, vs]
            ms = jnp.mean(oh * oh, axis=-1, keepdims=True)
            o_ref[rows, vs] = (oh * lax.rsqrt(ms + EPS) * ng * gate[:, vs]).astype(BF16)
        return carry

    lax.fori_loop(0, seq // fin, finish, 0)


def _gla(gqk, gv, gr, low, gup, gb, ng, row0, batch, seq, s0=None):
    b0 = row0 // seq
    tokmap = lambda b: (b0 + b, 0)
    hk = GLA_HEADS * GLA_DK
    in_specs = [pl.BlockSpec((seq, 512), tokmap), pl.BlockSpec((seq, 512), tokmap),
                pl.BlockSpec((seq, 512), tokmap), pl.BlockSpec((seq, 128), tokmap),
                pl.BlockSpec((2, 128, hk), lambda b: (0, 0, 0)),
                pl.BlockSpec((2, 1, hk), lambda b: (0, 0, 0)),
                pl.BlockSpec((1, GLA_DV), lambda b: (0, 0))]
    args = [gqk, gv, gr, low, gup, gb, ng]
    if s0 is not None:
        in_specs.append(pl.BlockSpec((1, 2, GLA_DV, hk), lambda b: (b, 0, 0, 0)))
        args.append(s0)
    return pl.pallas_call(
        functools.partial(_gla_body, seq=seq, has_init=s0 is not None),
        grid=(batch,),
        in_specs=in_specs,
        out_specs=[pl.BlockSpec((seq, 512), lambda b: (b, 0)),
                   pl.BlockSpec((1, 2, GLA_DV, hk), lambda b: (b, 0, 0, 0))],
        out_shape=[jax.ShapeDtypeStruct((batch * seq, 512), BF16),
                   jax.ShapeDtypeStruct((batch, 2, GLA_DV, hk), F32)],
        scratch_shapes=[pltpu.VMEM((2, GLA_DV, hk), F32),
                        pltpu.VMEM((seq, 512), F32), pltpu.VMEM((seq, 512), F32),
                        pltpu.VMEM((2, seq, hk), F32)],
        compiler_params=_cp("arbitrary"),
        name="gla_ctx" if s0 is not None else "gla",
    )(*args)


def _outproj0_body(x_ref, mod_ref, odap_ref, odas_ref, ogp_ref, ogs_ref, w_ref, o_ref, *, np_blocks):
    ctx = pl.program_id(0) < np_blocks
    oda = jnp.where(ctx, odap_ref[...], odas_ref[...])
    og = jnp.where(ctx, ogp_ref[...], ogs_ref[...])
    m = (jnp.dot(oda, w_ref[0:512, :], preferred_element_type=F32)
         + jnp.dot(og, w_ref[512:1024, :], preferred_element_type=F32))
    o_ref[...] = x_ref[...] + mod_ref[0][:, 2 * D:3 * D] * m


def _split_specs(block, np_blocks, axis):
    def at(k):
        return tuple(k if a == axis else 0 for a in range(len(block)))
    return (pl.BlockSpec(block, lambda i: at(jnp.minimum(i, np_blocks - 1))),
            pl.BlockSpec(block, lambda i: at(jnp.maximum(i - np_blocks, 0))))


def _outproj0(x, mod, oda_p, oda_s, og_p, og_s, w, np_rows, seq_rows):
    n = x.shape[0]
    tm = ROW_TILE
    npb = np_rows // tm
    rowmap = _mod_row_map(npb, seq_rows // tm)
    sp, ss = _split_specs((tm, 512), npb, 0)
    return pl.pallas_call(
        functools.partial(_outproj0_body, np_blocks=npb),
        grid=(n // tm,),
        in_specs=[pl.BlockSpec((tm, D), lambda i: (i, 0)),
                  pl.BlockSpec((1, 1, 6 * D), lambda i: (rowmap(i), 0, 0)),
                  sp, ss, sp, ss,
                  pl.BlockSpec((D, D), lambda i: (0, 0))],
        out_specs=pl.BlockSpec((tm, D), lambda i: (i, 0)),
        out_shape=jax.ShapeDtypeStruct((n, D), F32),
        compiler_params=_cp("arbitrary"),
        name="outproj0",
    )(x, mod, oda_p, oda_s, og_p, og_s, w)


def _route_body(x_ref, mod_ref, g_ref, wq_ref, sk_ref, ht_o, nsel_o, e1_o, rank_o, e2_o, top_scr, s_scr):
    mod = mod_ref[0]
    h = _norm_mod(x_ref[...], g_ref[...], mod[:, 4 * D:5 * D], mod[:, 3 * D:4 * D])
    ht_o[...] = h.T.astype(BF16)
    q = jnp.dot(h.astype(BF16), wq_ref[...], preferred_element_type=F32).astype(BF16)
    tm = q.shape[0]
    neg = -jnp.inf
    k = PEER_TOPK
    row8 = lax.broadcasted_iota(jnp.int32, (8, tm), 0)
    tiles = [slice(c * 128, (c + 1) * 128) for c in range(tm // 128)]
    for hh in range(PEER_HEADS):
        hs = slice(hh * PEER_NKEYS, (hh + 1) * PEER_NKEYS)
        for t in range(2):
            c0 = (hh * 2 + t) * PEER_NKEYS
            s_scr[t] = lax.dot_general(sk_ref[hh, t], q[:, c0:c0 + PEER_NKEYS], NT,
                                       preferred_element_type=F32)
            for cs in tiles:
                cur = s_scr[t, :, cs]
                rank = jnp.full((PEER_NKEYS, 128), float(k), F32)
                for j in range(k):
                    m = jnp.max(cur, axis=0, keepdims=True)
                    top_scr[t, j:j + 1, cs] = m
                    hit = cur == m
                    if t == 1:
                        rank = jnp.where(hit, float(j), rank)
                    cur = jnp.where(hit, neg, cur)
                if t == 1:
                    rank_o[hs, cs] = rank.astype(BF16)
        t1 = top_scr[0]
        t2 = top_scr[1]
        slabs = [t1[0:1] + t2, t1[1:2] + t2[0:8], t1[2:3] + t2[0:8], t1[3:4] + t2[0:8],
                 t2[0:1] + t1[8:16]]
        for j in range(3):
            slabs.append(jnp.where(row8 >= 4, t2[j:j + 1] + t1[0:8], neg))
        cand = jnp.concatenate(slabs, axis=0)
        top = t1[0:1] + t2[0:1]
        zsum = jnp.zeros_like(top)
        kth = top
        for j in range(k):
            kth = jnp.max(cand, axis=0, keepdims=True)
            zsum = zsum + jnp.exp(kth - top)
            cand = jnp.where(cand == kth, neg, cand)
        zinv = 1.0 / zsum
        for c, cs in enumerate(tiles):
            s1 = s_scr[0, :, cs]
            nsel = jnp.zeros((PEER_NKEYS, 128), F32)
            for j in range(k):
                nsel = jnp.where(s1 + t2[j:j + 1, cs] >= kth[:, cs], float(j + 1), nsel)
            nsel_o[c, hs, :] = nsel
            e1_o[c, hs, :] = jnp.exp(s1 - t1[0:1, cs]) * zinv[:, cs]
            e2_o[hs, cs] = jnp.exp(s_scr[1, :, cs] - t2[0:1, cs]).astype(BF16)


def _route(x, mod, norm_g, wq, sk, layer, np_rows, seq_rows):
    n = x.shape[0]
    tm = ROW_TILE
    rowmap = _mod_row_map(np_rows // tm, seq_rows // tm)
    rt = lambda dt: jax.ShapeDtypeStruct((PEER_HEADS * PEER_NKEYS, n), dt)
    rspec = pl.BlockSpec((PEER_HEADS * PEER_NKEYS, tm), lambda i: (0, i))
    st_ = jax.ShapeDtypeStruct((n // 128, PEER_HEADS * PEER_NKEYS, 128), F32)
    sspec = pl.BlockSpec((tm // 128, PEER_HEADS * PEER_NKEYS, 128), lambda i: (i, 0, 0))
    return pl.pallas_call(
        _route_body,
        grid=(n // tm,),
        in_specs=[pl.BlockSpec((tm, D), lambda i: (i, 0)),
                  pl.BlockSpec((1, 1, 6 * D), lambda i: (rowmap(i), 0, 0)),
                  pl.BlockSpec((1, D), lambda i: (0, 0)),
                  pl.BlockSpec((None, D, 2 * PEER_HEADS * PEER_NKEYS), lambda i: (layer, 0, 0)),
                  pl.BlockSpec((None, PEER_HEADS, 2, PEER_NKEYS, 128), lambda i: (layer, 0, 0, 0, 0))],
        out_specs=[pl.BlockSpec((D, tm), lambda i: (0, i)), sspec, sspec, rspec, rspec],
        out_shape=[jax.ShapeDtypeStruct((D, n), BF16), st_, st_, rt(BF16), rt(BF16)],
        scratch_shapes=[pltpu.VMEM((2, PEER_TOPK, tm), F32), pltpu.VMEM((2, PEER_NKEYS, tm), F32)],
        compiler_params=_cp("arbitrary"),
        name="peer_route",
    )(x, mod, norm_g, wq, sk)


def _peer_body(ht_ref, u_ref, vt_ref, nsel_ref, e1_ref, rank_ref, e2_ref, x_ref, mod_ref, fg_ref,
               o_ref, acc_ref, p_ref, *, final_norm):
    j = pl.program_id(1)

    @pl.when(j == 0)
    def _():
        acc_ref[...] = jnp.zeros(acc_ref.shape, F32)

    nk = PEER_NKEYS
    tok = ht_ref.shape[1]
    na = PEER_EXP // nk
    a0 = pl.multiple_of(j * na, na)
    half = PEER_EXP // 2
    rg = 8
    zero = jnp.zeros((rg, PEER_SUB), BF16)

    def row_bcast(ref, hh, r, ts):
        parts = [jnp.broadcast_to(ref[c, pl.ds(hh * nk + a0, na), :][r:r + 1], (rg, 128))
                 for c in range(ts.start // 128, ts.stop // 128)]
        return jnp.concatenate(parts, axis=1).astype(BF16)

    for t0 in range(0, tok, PEER_SUB):
        ts = slice(t0, t0 + PEER_SUB)
        act = [jnp.dot(u_ref[hf * half:(hf + 1) * half, :], ht_ref[:, ts], preferred_element_type=F32)
               for hf in range(2)]
        for r in range(na):
            gate = [zero] * (nk // rg)
            for hh in range(PEER_HEADS):
                ns = row_bcast(nsel_ref, hh, r, ts)
                e1 = row_bcast(e1_ref, hh, r, ts)
                for g in range(nk // rg):
                    bs = slice(hh * nk + g * rg, hh * nk + (g + 1) * rg)
                    gate[g] = gate[g] + jnp.where(rank_ref[bs, ts] < ns, e2_ref[bs, ts], zero) * e1
            for g in range(nk // rg):
                lo = r * nk + g * rg
                a = act[lo // half][lo % half:lo % half + rg, :]
                p_ref[lo:lo + rg, ts] = _gelu(a).astype(BF16) * gate[g]
        for hf in range(2):
            ds = slice(hf * (D // 2), (hf + 1) * (D // 2))
            acc_ref[ds, ts] += jnp.dot(vt_ref[ds, :], p_ref[:, ts], preferred_element_type=F32)

    @pl.when(j == pl.num_programs(1) - 1)
    def _():
        y = x_ref[...] + mod_ref[0][:, 5 * D:6 * D] * acc_ref[...].T
        if final_norm:
            ms = jnp.mean(y * y, axis=-1, keepdims=True)
            y = y * lax.rsqrt(ms + EPS) * fg_ref[...]
        o_ref[...] = y


def _peer_dense(x, mod, ht, u, vt, layer, nsel, e1, rank, e2, fg, np_rows, seq_rows, final_norm):
    n = x.shape[0]
    tk, ex = PEER_TOK, PEER_EXP
    rowmap = _mod_row_map(np_rows // tk, seq_rows // tk)
    rspec = pl.BlockSpec((PEER_HEADS * PEER_NKEYS, tk), lambda i, j: (0, i))
    sspec = pl.BlockSpec((tk // 128, PEER_HEADS * PEER_NKEYS, 128), lambda i, j: (i, 0, 0))
    return pl.pallas_call(
        functools.partial(_peer_body, final_norm=final_norm),
        grid=(n // tk, u.shape[1] // ex),
        in_specs=[pl.BlockSpec((D, tk), lambda i, j: (0, i)),
                  pl.BlockSpec((None, ex, D), lambda i, j: (layer, j, 0)),
                  pl.BlockSpec((None, D, ex), lambda i, j: (layer, 0, j)),
                  sspec, sspec, rspec, rspec,
                  pl.BlockSpec((tk, D), lambda i, j: (i, 0)),
                  pl.BlockSpec((1, 1, 6 * D), lambda i, j: (rowmap(i), 0, 0)),
                  pl.BlockSpec((1, D), lambda i, j: (0, 0))],
        out_specs=pl.BlockSpec((tk, D), lambda i, j: (i, 0)),
        out_shape=jax.ShapeDtypeStruct((n, D), F32),
        scratch_shapes=[pltpu.VMEM((D, tk), F32), pltpu.VMEM((ex, tk), BF16)],
        compiler_params=_cp("arbitrary", "arbitrary"),
        name="peer_dense",
    )(ht, u, vt, nsel, e1, rank, e2, x, mod, fg)


def _peer_tables(w_query, sub_keys, u, v):
    return w_query.astype(BF16), sub_keys.astype(BF16), u.astype(BF16), v.transpose(0, 2, 1).astype(BF16)


def _peer(x, mod, norm_g, tables, layer, fg, np_rows, seq_rows, final_norm):
    wq, sk, u, vt = tables
    ht, nsel, e1, rank, e2 = _route(x, mod, norm_g, wq, sk, layer, np_rows, seq_rows)
    return _peer_dense(x, mod, ht, u, vt, layer, nsel, e1, rank, e2, fg, np_rows, seq_rows, final_norm)


def _window_of(lane, i):
    return (lane >> 4) == i


def _s5_in_body(x_ref, mod_ref, g_ref, w_ref, o_ref, u_scr):
    mod = mod_ref[0]
    h = _norm_mod(x_ref[...], g_ref[...], mod[:, D:2 * D], mod[:, 0:D]).astype(BF16)
    u = jnp.dot(h, w_ref[...], preferred_element_type=F32)
    cs, per = S5_CHUNK, 128 // S5_CH
    nck = u.shape[0] // cs
    for j in range(D // 128):
        u_scr[j] = u[:, j * 128:(j + 1) * 128]
    lane = lax.broadcasted_iota(jnp.int32, (nck, 128), 1)
    for j in range(D // 128):
        xs = [u_scr[j, pl.ds(s, nck, stride=cs), :] for s in range(cs)]
        for gp in range(per):
            for half in range(cs // per):
                acc = None
                for s8 in range(per):
                    piece = xs[half * per + s8]
                    shift = ((s8 - gp) * S5_CH) % 128
                    if shift:
                        piece = pltpu.roll(piece, shift, 1)
                    acc = piece if acc is None else jnp.where(_window_of(lane, s8), piece, acc)
                o_ref[j * per + gp, :, half * 128:(half + 1) * 128] = acc.astype(BF16)


def _s5_in(x, mod, norm_g, w, np_rows, seq_rows):
    n = x.shape[0]
    tm = ROW_TILE
    rowmap = _mod_row_map(np_rows // tm, seq_rows // tm)
    nck = tm // S5_CHUNK
    return pl.pallas_call(
        _s5_in_body,
        grid=(n // tm,),
        in_specs=[pl.BlockSpec((tm, D), lambda i: (i, 0)),
                  pl.BlockSpec((1, 1, 6 * D), lambda i: (rowmap(i), 0, 0)),
                  pl.BlockSpec((1, D), lambda i: (0, 0)),
                  pl.BlockSpec((D, D), lambda i: (0, 0))],
        out_specs=pl.BlockSpec((S5_GROUPS, nck, S5_CHUNK * S5_CH), lambda i: (0, i, 0)),
        out_shape=jax.ShapeDtypeStruct((S5_GROUPS, n // S5_CHUNK, S5_CHUNK * S5_CH), BF16),
        scratch_shapes=[pltpu.VMEM((D // 128, tm, 128), F32)],
        compiler_params=_cp("arbitrary"),
        name="s5_in",
    )(x, mod, norm_g, w)


def _s5_core_body(u_ref, m_ref, win_ref, cout_ref, pw_ref, *rest, kseq, has_init, emit_states):
    rest = list(rest)
    h0_ref = rest.pop(0) if has_init else None
    y_ref = rest.pop(0)
    sf_ref = rest.pop(0) if emit_states else None
    kb = u_ref.shape[1]
    row = lax.broadcasted_iota(jnp.int32, (kb, 128), 0)
    kk = row & (kseq - 1)

    def cmul(a1, a2, x):
        return a1 * x + a2 * pltpu.roll(x, 64, 1)

    for gi in range(u_ref.shape[0]):
        u = u_ref[gi]
        y = jnp.dot(u, m_ref[gi], preferred_element_type=F32)
        for d in range(2):
            s = jnp.dot(u, win_ref[gi, d], preferred_element_type=F32)
            edge = (kk == 0) if d == 0 else (kk == kseq - 1)
            if has_init:
                h0 = h0_ref[gi, d]
                s = s + jnp.where(edge, cmul(pw_ref[gi, d, 0, 0:1], pw_ref[gi, d, 0, 1:2], h0), 0.0)
            sw = pltpu.roll(s, 64, 1)
            step, lvl = 1, 0
            while step < kseq:
                shift = step if d == 0 else kb - step
                ok = (kk >= step) if d == 0 else (kk < kseq - step)
                a1, a2 = pw_ref[gi, d, lvl, 0:1], pw_ref[gi, d, lvl, 1:2]
                sh, swh = pltpu.roll(s, shift, 0), pltpu.roll(sw, shift, 0)
                s = s + jnp.where(ok, a1 * sh + a2 * swh, 0.0)
                sw = sw + jnp.where(ok, a1 * swh - a2 * sh, 0.0)
                step, lvl = step * 2, lvl + 1
            if emit_states:
                sf_ref[gi, d] = s
            hin = pltpu.roll(s, 1, 0) if d == 0 else pltpu.roll(s, kb - 1, 0)
            if has_init:
                hin = jnp.where(edge, h0, hin)
            else:
                hin = jnp.where(edge, 0.0, hin)
            y = y + jnp.dot(hin.astype(BF16), cout_ref[gi, d], preferred_element_type=F32)
        y_ref[gi] = _gelu(y).astype(BF16)


def _s5_core(ug, mats, row0, rows, kseq, h0=None, emit_states=False):
    m, win, cout, pw = mats
    kb = 128
    b0 = row0 // kb
    nlv = pw.shape[2]
    gb = S5_GROUP_BLOCK
    in_specs = [pl.BlockSpec((gb, kb, 256), lambda g, i: (g, b0 + i, 0)),
                pl.BlockSpec((gb, 256, 256), lambda g, i: (g, 0, 0)),
                pl.BlockSpec((gb, 2, 256, 128), lambda g, i: (g, 0, 0, 0)),
                pl.BlockSpec((gb, 2, 128, 256), lambda g, i: (g, 0, 0, 0)),
                pl.BlockSpec((gb, 2, nlv, 2, 128), lambda g, i: (g, 0, 0, 0, 0))]
    args = [ug, m, win, cout, pw]
    if h0 is not None:
        assert kseq == kb
        in_specs.append(pl.BlockSpec((gb, None, 2, 1, 128), lambda g, i: (g, i, 0, 0, 0)))
        args.append(h0)
    out_specs = [pl.BlockSpec((gb, kb, 256), lambda g, i: (g, i, 0))]
    out_shape = [jax.ShapeDtypeStruct((S5_GROUPS, rows, 256), BF16)]
    if emit_states:
        out_specs.append(pl.BlockSpec((gb, 2, kb, 128), lambda g, i: (g, 0, i, 0)))
        out_shape.append(jax.ShapeDtypeStruct((S5_GROUPS, 2, rows, 128), F32))
    return pl.pallas_call(
        functools.partial(_s5_core_body, kseq=kseq, has_init=h0 is not None, emit_states=emit_states),
        grid=(S5_GROUPS // gb, rows // kb),
        in_specs=in_specs,
        out_specs=out_specs,
        out_shape=out_shape,
        compiler_params=_cp("arbitrary", "arbitrary"),
        name="s5_core_ctx" if h0 is not None else "s5_core",
    )(*args)


def _s5_out_body(x_ref, mod_ref, yp_ref, ys_ref, w_ref, o_ref, y_scr, *, np_blocks):
    cs, per = S5_CHUNK, 128 // S5_CH
    nck = yp_ref.shape[1]
    lane = lax.broadcasted_iota(jnp.int32, (nck, 128), 1)
    ctx = pl.program_id(0) < np_blocks
    for j in range(D // 128):
        gs = slice(j * per, (j + 1) * per)
        yj = jnp.where(ctx, yp_ref[gs], ys_ref[gs])
        src = [[yj[gp, :, half * 128:(half + 1) * 128].astype(F32) for half in range(cs // per)]
               for gp in range(per)]
        for t in range(cs):
            half, t8 = divmod(t, per)
            acc = None
            for gp in range(per):
                piece = src[gp][half]
                shift = ((gp - t8) * S5_CH) % 128
                if shift:
                    piece = pltpu.roll(piece, shift, 1)
                acc = piece if acc is None else jnp.where(_window_of(lane, gp), piece, acc)
            y_scr[j, pl.ds(t, nck, stride=cs), :] = acc
    y = jnp.concatenate([y_scr[j] for j in range(D // 128)], axis=1).astype(BF16)
    zz = jnp.dot(y, w_ref[...], preferred_element_type=F32)
    za = zz[:, 0:D]
    zb = zz[:, D:2 * D]
    o_ref[...] = x_ref[...] + mod_ref[0][:, 2 * D:3 * D] * (za / (1.0 + jnp.exp(-zb)))


def _s5_out(x, mod, y_p, y_s, w, np_rows, seq_rows):
    n = x.shape[0]
    tm = ROW_TILE
    npb = np_rows // tm
    rowmap = _mod_row_map(npb, seq_rows // tm)
    sp, ss = _split_specs((S5_GROUPS, tm // S5_CHUNK, S5_CHUNK * S5_CH), npb, 1)
    return pl.pallas_call(
        functools.partial(_s5_out_body, np_blocks=npb),
        grid=(n // tm,),
        in_specs=[pl.BlockSpec((tm, D), lambda i: (i, 0)),
                  pl.BlockSpec((1, 1, 6 * D), lambda i: (rowmap(i), 0, 0)),
                  sp, ss,
                  pl.BlockSpec((D, 2 * D), lambda i: (0, 0))],
        out_specs=pl.BlockSpec((tm, D), lambda i: (i, 0)),
        out_shape=jax.ShapeDtypeStruct((n, D), F32),
        scratch_shapes=[pltpu.VMEM((D // 128, tm, 128), F32)],
        compiler_params=_cp("arbitrary"),
        name="s5_out",
    )(x, mod, y_p, y_s, w)


def _s5_matrices(a_re, a_im, log_dt, b_re, b_im, c_re, c_im, d_skip):
    cs = S5_CHUNK
    dt = jnp.exp(log_dt)[..., None]
    lr, li = a_re * dt, a_im * dt

    def lam_pow(tau):
        mag = jnp.exp(lr[:, :, None, :] * tau[:, None])
        ang = li[:, :, None, :] * tau[:, None]
        return mag * jnp.cos(ang), mag * jnp.sin(ang)

    l1r, l1i = jnp.exp(lr) * jnp.cos(li), jnp.exp(lr) * jnp.sin(li)
    den = a_re * a_re + a_im * a_im
    cr = ((l1r - 1.0) * a_re + l1i * a_im) / den
    ci = (l1i * a_re - (l1r - 1.0) * a_im) / den
    bt_re = b_re.transpose(0, 1, 3, 2)
    bt_im = b_im.transpose(0, 1, 3, 2)
    bbr = cr[:, :, None, :] * bt_re - ci[:, :, None, :] * bt_im
    bbi = cr[:, :, None, :] * bt_im + ci[:, :, None, :] * bt_re

    pr, pi = lam_pow(jnp.arange(cs + 1, dtype=F32))
    clr = c_re[:, :, None] * pr[:, :, :, None, :] - c_im[:, :, None] * pi[:, :, :, None, :]
    cli = c_re[:, :, None] * pi[:, :, :, None, :] + c_im[:, :, None] * pr[:, :, :, None, :]
    cl = jnp.concatenate([clr[:, :, :cs], -cli[:, :, :cs]], axis=-1).reshape(2, S5_GROUPS, cs * S5_CH, 2 * S5_P)
    bb = jnp.concatenate([bbr, bbi], axis=-1)
    kern = jnp.einsum('dgeq,dgjq->dgej', bb, cl, precision=HI)
    w = cs * S5_CH
    kern_b = kern[1].reshape(S5_GROUPS, S5_CH, cs, S5_CH)[:, :, ::-1].reshape(S5_GROUPS, S5_CH, w)
    zpad = lambda a, lo, hi: jnp.pad(a, ((0, 0), (0, 0), (lo, hi)))
    mf = jnp.stack([zpad(kern[0][:, :, :w - S5_CH * s], S5_CH * s, 0) for s in range(cs)], axis=1)
    mb = jnp.stack([zpad(kern_b[:, :, S5_CH * (cs - 1 - s):], 0, S5_CH * (cs - 1 - s)) for s in range(cs)], axis=1)
    m = (mf + mb).reshape(S5_GROUPS, w, w)
    m = m + jnp.eye(cs * S5_CH, dtype=F32) * jnp.tile(d_skip.reshape(S5_GROUPS, 1, S5_CH), (1, 1, cs))

    def win_dir(d, powers):
        qr_, qi_ = pr[d][:, powers][:, :, None, :], pi[d][:, powers][:, :, None, :]
        wr = qr_ * bbr[d][:, None] - qi_ * bbi[d][:, None]
        wi = qr_ * bbi[d][:, None] + qi_ * bbr[d][:, None]
        return jnp.concatenate([wr, wi], axis=-1).reshape(S5_GROUPS, cs * S5_CH, 2 * S5_P)

    win = jnp.stack([win_dir(0, jnp.arange(cs - 1, -1, -1)), win_dir(1, jnp.arange(cs))], axis=1)

    def cout_dir(d, powers):
        z = jnp.concatenate([clr[d][:, powers], -cli[d][:, powers]], axis=-1)
        return z.reshape(S5_GROUPS, cs * S5_CH, 2 * S5_P).transpose(0, 2, 1)

    cout = jnp.stack([cout_dir(0, jnp.arange(1, cs + 1)), cout_dir(1, jnp.arange(cs, 0, -1))], axis=1)

    qr, qi = lam_pow(cs * (2.0 ** jnp.arange(7, dtype=F32)))
    a1 = jnp.concatenate([qr, qr], axis=-1)
    a2 = jnp.concatenate([-qi, qi], axis=-1)
    pw = jnp.stack([a1, a2], axis=3).transpose(1, 0, 2, 3, 4)
    return m.astype(BF16), win.astype(BF16), cout.astype(BF16), pw


def _rope_tables(dec_seq):
    rows = dec_seq // GRID_W
    row_id = jnp.repeat(jnp.arange(rows), GRID_W).astype(F32)
    col_id = jnp.tile(jnp.arange(GRID_W), rows).astype(F32)
    quarter = DA_DIM // 4
    inv = ROPE_BASE ** (-jnp.arange(quarter, dtype=F32) / quarter)
    ang_r = row_id[:, None] * inv
    ang_c = col_id[:, None] * inv
    ang = jnp.concatenate([ang_r, ang_r, ang_c, ang_c], axis=-1)
    ang = jnp.tile(ang, (1, 2 * DA_HEADS))
    pad = jnp.zeros((ROW_TILE, ang.shape[1]), F32)
    ang = jnp.concatenate([pad, ang], axis=0)
    return jnp.cos(ang), jnp.sin(ang)


def kernel(x_prompt, x_sample, cache_da_k, cache_da_v, state_gla, state_s5, c, c_ctx,
           ada_w, ada_b, norm_g, final_norm_g, mix0_w_in, mix0_w_out, da_lam, da_subln_g,
           gla_gate_up, gla_gate_bias, gla_norm_g, s5_w_in, s5_a_re, s5_a_im, s5_log_dt,
           s5_b_re, s5_b_im, s5_c_re, s5_c_im, s5_d, s5_w_out,
           peer_w_query, peer_sub_keys, peer_u, peer_v):
    bp, lp, _ = x_prompt.shape
    bs, ls, _ = x_sample.shape
    npr, nsr = bp * lp, bs * ls
    assert npr % ls == 0 and ls % ROW_TILE == 0 and lp % ROW_TILE == 0
    x = jnp.concatenate([x_prompt.reshape(npr, D), x_sample.reshape(nsr, D)], axis=0)

    cond = jnp.concatenate([c_ctx[None], c, jnp.zeros((15 - bs, D), F32)], axis=0)
    mod_all = _ada_mod(cond, ada_w, ada_b)
    mods = [mod_all[l].reshape(16, 1, 6 * D) for l in range(mod_all.shape[0])]
    fg = final_norm_g.reshape(1, D)

    lam_init = 0.8 - 0.6 * math.exp(-0.3 * 0)
    cos_t, sin_t = _rope_tables(ls)
    w_in = jnp.pad(mix0_w_in[0], ((0, 0), (0, IN0_PAD - mix0_w_in.shape[2]))).astype(BF16)
    q, kb, vb, k32, v32, gqk, gv, gr, low = _inproj0(
        x, mods[0], norm_g[0, 0].reshape(1, D), w_in, cos_t, sin_t, npr, ls)
    sub_g = da_subln_g[0].reshape(1, 2 * DA_DIM)
    past = cache_da_k.shape[2]
    oda_p = _diff_attention(q, kb, vb, da_lam[0], sub_g, lam_init, 0, bp, lp)
    oda_s = _diff_attention(q, kb, vb, da_lam[0], sub_g, lam_init, npr, bs, ls,
                            cache=(cache_da_k[:, 0].reshape(bs, past, 512), cache_da_v[:, 0].reshape(bs, past, 512)))
    hk = GLA_HEADS * GLA_DK
    gup = jnp.zeros((2, 128, hk), F32)
    for d in range(2):
        gup = gup.at[d, d * GLA_RANK:(d + 1) * GLA_RANK].set(gla_gate_up[0, d])
    gbias = gla_gate_bias[0].reshape(2, 1, hk)
    ng = gla_norm_g[0].reshape(1, GLA_DV)
    s0 = state_gla[:, 0].transpose(0, 1, 4, 2, 3).reshape(bs, 2, GLA_DV, hk)
    og_p, st_p = _gla(gqk, gv, gr, low, gup, gbias, ng, 0, bp, lp)
    og_s, _ = _gla(gqk, gv, gr, low, gup, gbias, ng, npr, bs, ls, s0=s0)
    x = _outproj0(x, mods[0], oda_p, oda_s, og_p, og_s, mix0_w_out[0].astype(BF16), npr, ls)
    tables = _peer_tables(peer_w_query, peer_sub_keys, peer_u, peer_v)
    x = _peer(x, mods[0], norm_g[0, 1].reshape(1, D), tables, 0, fg, npr, ls, final_norm=False)

    ug = _s5_in(x, mods[1], norm_g[1, 0].reshape(1, D), s5_w_in[0].astype(BF16), npr, ls)
    cs = S5_CHUNK
    mats = _s5_matrices(s5_a_re[0], s5_a_im[0], s5_log_dt[0], s5_b_re[0], s5_b_im[0],
                        s5_c_re[0], s5_c_im[0], s5_d[0])
    h0 = state_s5[:, 0]
    h0 = h0.transpose(2, 0, 1, 4, 3).reshape(S5_GROUPS, bs, 2, 1, 2 * S5_P)
    y_p, sf = _s5_core(ug, mats, 0, npr // cs, lp // cs, emit_states=True)
    (y_s,) = _s5_core(ug, mats, npr // cs, nsr // cs, ls // cs, h0=h0)
    x = _s5_out(x, mods[1], y_p, y_s, s5_w_out[0].astype(BF16), npr, ls)
    x = _peer(x, mods[1], norm_g[1, 1].reshape(1, D), tables, 1, fg, npr, ls, final_norm=True)

    y_prompt = x[:npr].reshape(bp, lp, D)
    y_sample = x[npr:].reshape(bs, ls, D)
    new_k = k32[:npr].reshape(bp, 1, lp, 2 * DA_HEADS, DA_DIM)
    new_v = v32[:npr].reshape(bp, 1, lp, DA_HEADS, 2 * DA_DIM)
    new_gla = st_p.reshape(bp, 2, GLA_DV, GLA_HEADS, GLA_DK).transpose(0, 1, 3, 4, 2)[:, None]
    kc = lp // cs
    fin = jnp.stack([sf[:, 0, kc - 1::kc], sf[:, 1, 0::kc]], axis=1)
    new_s5 = fin.reshape(S5_GROUPS, 2, bp, 2, S5_P).transpose(2, 1, 0, 4, 3)[:, None]
    return (y_prompt, y_sample, new_k, new_v, new_gla, new_s5)
```

```python
import functools
import math

import jax
import jax.numpy as jnp
from jax import lax
from jax.experimental import pallas as pl
from jax.experimental.pallas import tpu as pltpu

F32 = jnp.float32
BF16 = jnp.bfloat16
HI = lax.Precision.HIGHEST

D = 1024
EPS = 1e-6
ROPE_BASE = 10000.0
GRID_W = 64
DA_HEADS = 4
DA_DIM = 64
GLA_HEADS = 4
GLA_DK = 64
GLA_DV = 128
GLA_RANK = 16
GLA_TAU = 16.0
GLA_CHUNK = 64
S5_CH = 16
S5_GROUPS = D // S5_CH
S5_P = 64
S5_CHUNK = 16
S5_GROUP_BLOCK = 4
PEER_HEADS = 8
PEER_NKEYS = 128
PEER_TOPK = 16
IN0_PAD = 3200

ROW_TILE = 256
PEER_TOK = 512
PEER_EXP = 2048
PEER_SUB = 256
VMEM_LIMIT = 48 * 1024 * 1024

NT = (((1,), (1,)), ((), ()))
TN = (((0,), (0,)), ((), ()))


def _cp(*sem, flags=None):
    return pltpu.CompilerParams(dimension_semantics=sem, vmem_limit_bytes=VMEM_LIMIT, flags=flags)


def _norm_mod(x, g, sc, sh):
    ms = jnp.mean(x * x, axis=-1, keepdims=True)
    return x * lax.rsqrt(ms + EPS) * g * (1.0 + sc) + sh


_GELU_C2 = -2.0 * 0.7978845608028654 * 1.4426950408889634
_GELU_C1 = _GELU_C2 * 0.044715


def _gelu(x):
    return x / (1.0 + jnp.exp2(x * (_GELU_C1 * (x * x) + _GELU_C2)))


def _mod_row_map(np_blocks, seq_blocks):
    def f(i):
        return jnp.where(i < np_blocks, 0, 1 + (i - np_blocks) // seq_blocks)
    return f


def _ada_body(c_ref, w_ref, b_ref, o_ref):
    c = c_ref[...]
    s = c / (1.0 + jnp.exp(-c))
    o_ref[0] = jnp.dot(s, w_ref[0], precision=HI, preferred_element_type=F32) + b_ref[0]


def _ada_mod(cond, ada_w, ada_b):
    depth, _, n = ada_w.shape
    rows = cond.shape[0]
    tn = 1536
    return pl.pallas_call(
        _ada_body,
        grid=(depth, n // tn),
        in_specs=[pl.BlockSpec((rows, D), lambda l, j: (0, 0)),
                  pl.BlockSpec((1, D, tn), lambda l, j: (l, 0, j)),
                  pl.BlockSpec((1, 1, tn), lambda l, j: (l, 0, j))],
        out_specs=pl.BlockSpec((1, rows, tn), lambda l, j: (l, 0, j)),
        out_shape=jax.ShapeDtypeStruct((depth, rows, n), F32),
        compiler_params=_cp("arbitrary", "arbitrary"),
        name="ada_mod",
    )(cond, ada_w, ada_b.reshape(depth, 1, n))


def _inproj0_body(x_ref, mod_ref, g_ref, w_ref, cos_ref, sin_ref,
                  q_o, kb_o, vb_o, k32_o, v32_o, gqk_o, gv_o, gr_o, low_o):
    mod = mod_ref[0]
    h = _norm_mod(x_ref[...], g_ref[...], mod[:, D:2 * D], mod[:, 0:D]).astype(BF16)
    y = jnp.dot(h, w_ref[...], preferred_element_type=F32)
    cos = cos_ref[...]
    sin = sin_ref[...]
    lane = lax.broadcasted_iota(jnp.int32, cos.shape, 1)
    first = (lane & 31) < 16

    def rope(t):
        rot = jnp.where(first, -pltpu.roll(t, 512 - 16, 1), pltpu.roll(t, 16, 1))
        return t * cos + rot * sin

    q = rope(y[:, 0:512])
    k = rope(y[:, 512:1024])
    v = y[:, 1024:1536]
    q_o[...] = q.astype(BF16)
    kb_o[...] = k.astype(BF16)
    vb_o[...] = v.astype(BF16)
    k32_o[...] = k
    v32_o[...] = v
    gqk_o[...] = y[:, 1536:2048].astype(BF16)
    gv_o[...] = y[:, 2048:2560].astype(BF16)
    gr_o[...] = y[:, 2560:3072].astype(BF16)
    low_o[...] = y[:, 3072:3200]


def _inproj0(x, mod, norm_g, w, cos_t, sin_t, np_rows, seq_rows):
    n = x.shape[0]
    tm = ROW_TILE
    npb, sqb = np_rows // tm, seq_rows // tm
    rowmap = _mod_row_map(npb, sqb)
    posmap = lambda i: (jnp.where(i < npb, 0, 1 + (i - npb) % sqb), 0)
    tok = lambda w_: pl.BlockSpec((tm, w_), lambda i: (i, 0))
    outs = [(512, BF16), (512, BF16), (512, BF16), (512, F32), (512, F32),
            (512, BF16), (512, BF16), (512, BF16), (128, F32)]
    return pl.pallas_call(
        _inproj0_body,
        grid=(n // tm,),
        in_specs=[tok(D),
                  pl.BlockSpec((1, 1, 6 * D), lambda i: (rowmap(i), 0, 0)),
                  pl.BlockSpec((1, D), lambda i: (0, 0)),
                  pl.BlockSpec((D, IN0_PAD), lambda i: (0, 0)),
                  pl.BlockSpec((tm, 512), posmap),
                  pl.BlockSpec((tm, 512), posmap)],
        out_specs=[tok(w_) for w_, _ in outs],
        out_shape=[jax.ShapeDtypeStruct((n, w_), dt) for w_, dt in outs],
        compiler_params=_cp("arbitrary"),
        name="inproj0",
    )(x, mod, norm_g, w, cos_t, sin_t)


def _attn_body(lam_ref, sg_ref, q_ref, k_ref, v_ref, *rest, lam_init, has_cache):
    if has_cache:
        kc_ref, vc_ref, o_ref = rest
    else:
        (o_ref,) = rest
    lf = lam_ref[...]
    lam = (jnp.exp(jnp.sum(lf[0:1] * lf[1:2], axis=-1, keepdims=True))
           - jnp.exp(jnp.sum(lf[2:3] * lf[3:4], axis=-1, keepdims=True)) + lam_init)
    q = q_ref[...] * (DA_DIM ** -0.5)
    lane = lax.broadcasted_iota(jnp.int32, q.shape, 1)
    zero = jnp.zeros_like(q)
    qm = (jnp.where(lane < DA_DIM, q, zero), jnp.where(lane >= DA_DIM, q, zero))
    k = k_ref[...]
    s = [lax.dot_general(qm[m], k, NT, preferred_element_type=F32) for m in range(2)]
    mx = [jnp.max(s[m], axis=-1, keepdims=True) for m in range(2)]
    if has_cache:
        kc = kc_ref[...].astype(BF16)
        sc = [lax.dot_general(qm[m], kc, NT, preferred_element_type=F32) for m in range(2)]
        mx = [jnp.maximum(mx[m], jnp.max(sc[m], axis=-1, keepdims=True)) for m in range(2)]
    e = [jnp.exp(s[m] - mx[m]) for m in range(2)]
    z = [jnp.sum(e[m], axis=-1, keepdims=True) for m in range(2)]
    if has_cache:
        ec = [jnp.exp(sc[m] - mx[m]) for m in range(2)]
        z = [z[m] + jnp.sum(ec[m], axis=-1, keepdims=True) for m in range(2)]
    w0 = 1.0 / z[0]
    w1 = lam / z[1]
    o = jnp.dot((e[0] * w0 - e[1] * w1).astype(BF16), v_ref[...], preferred_element_type=F32)
    if has_cache:
        o = o + jnp.dot((ec[0] * w0 - ec[1] * w1).astype(BF16), vc_ref[...].astype(BF16),
                        preferred_element_type=F32)
    ms = jnp.mean(o * o, axis=-1, keepdims=True)
    o_ref[...] = (o * lax.rsqrt(ms + EPS) * sg_ref[...] * (1.0 - lam_init)).astype(o_ref.dtype)


def _diff_attention(q, k, v, da_lam, subln_g, lam_init, row0, batch, seq, cache=None):
    tq = 256
    qb0, kb0 = row0 // tq, row0 // seq
    in_specs = [pl.BlockSpec((4, DA_DIM), lambda b, h, i: (0, 0)),
                pl.BlockSpec((1, 128), lambda b, h, i: (0, 0)),
                pl.BlockSpec((tq, 128), lambda b, h, i: (qb0 + b * (seq // tq) + i, h)),
                pl.BlockSpec((seq, 128), lambda b, h, i: (kb0 + b, h)),
                pl.BlockSpec((seq, 128), lambda b, h, i: (kb0 + b, h))]
    args = [da_lam, subln_g, q, k, v]
    if cache is not None:
        past = cache[0].shape[1]
        in_specs += [pl.BlockSpec((None, past, 128), lambda b, h, i: (b, 0, h))] * 2
        args += list(cache)
    return pl.pallas_call(
        functools.partial(_attn_body, lam_init=lam_init, has_cache=cache is not None),
        grid=(batch, DA_HEADS, seq // tq),
        in_specs=in_specs,
        out_specs=pl.BlockSpec((tq, 128), lambda b, h, i: (b * (seq // tq) + i, h)),
        out_shape=jax.ShapeDtypeStruct((batch * seq, 512), BF16),
        compiler_params=_cp("arbitrary", "arbitrary", "arbitrary"),
        name="diff_attn_ctx" if cache is not None else "diff_attn",
    )(*args)


def _gla_body(qk_ref, v_ref, r_ref, low_ref, gup_ref, gb_ref, ng_ref, *rest, seq, has_init):
    if has_init:
        s0_ref, o_ref, sf_ref, st_scr, of_scr, ob_scr, la_scr = rest
    else:
        o_ref, sf_ref, st_scr, of_scr, ob_scr, la_scr = rest
    c = GLA_CHUNK
    n = seq // c
    hk = GLA_HEADS * GLA_DK
    low = low_ref[...]
    for d in range(2):
        z = jnp.dot(low, gup_ref[d], precision=HI, preferred_element_type=F32) + gb_ref[d]
        la_scr[d] = (jnp.minimum(z, 0.0) - jnp.log(1.0 + jnp.exp(-jnp.abs(z)))) * (1.0 / GLA_TAU)
    if has_init:
        st_scr[...] = s0_ref[0]
    else:
        st_scr[...] = jnp.zeros(st_scr.shape, F32)
    row = lax.broadcasted_iota(jnp.int32, (2 * c, 2 * c), 0)
    col = lax.broadcasted_iota(jnp.int32, (2 * c, 2 * c), 1)
    fwd_blk = jnp.logical_and(row < c, col <= row)
    bwd_blk = jnp.logical_and(jnp.logical_and(row >= c, col >= c), col >= row)
    keep = jnp.logical_or(fwd_blk, bwd_blk)
    keep_f = keep.astype(F32)
    lane = lax.broadcasted_iota(jnp.int32, (2 * c, 128), 1)
    low_half = lane < GLA_DK
    top_rows = lax.broadcasted_iota(jnp.int32, (2 * c, hk), 0) < c

    def step(i, carry):
        rf = pl.ds(pl.multiple_of(i * c, c), c)
        rb = pl.ds(pl.multiple_of((n - 1 - i) * c, c), c)
        qk = jnp.concatenate([qk_ref[rf, :], qk_ref[rb, :]], axis=0).astype(F32)
        q = qk[:, 0:hk] * (GLA_DK ** -0.5)
        k = qk[:, hk:2 * hk]
        v = jnp.concatenate([v_ref[rf, :], v_ref[rb, :]], axis=0)
        g = jnp.concatenate([la_scr[0, rf, :], la_scr[1, rb, :]], axis=0)
        b = jnp.dot(keep_f, g, precision=HI, preferred_element_type=F32)
        b_tot = (b[c - 1:c], b[c:c + 1])
        q_dec = q * jnp.exp(b)
        k_inv = k * jnp.exp(-b)
        k_end = k * jnp.exp(jnp.where(top_rows, b_tot[0], b_tot[1]) - b)
        st = (st_scr[0], st_scr[1])
        st_b = (st[0].astype(BF16), st[1].astype(BF16))
        outs, news = [], ([], [])
        for pair in range(GLA_HEADS // 2):
            ps = slice(pair * 128, (pair + 1) * 128)
            kin = k_inv[:, ps].astype(BF16)
            new = [None, None]
            for sub in range(2):
                hh = pair * 2 + sub
                vs = slice(hh * GLA_DV, (hh + 1) * GLA_DV)
                sel = low_half if sub == 0 else jnp.logical_not(low_half)
                qd = jnp.where(sel, q_dec[:, ps], 0.0).astype(BF16)
                ke = jnp.where(sel, k_end[:, ps], 0.0).astype(BF16)
                att = lax.dot_general(qd, kin, NT, preferred_element_type=F32)
                att = jnp.where(keep, att, 0.0).astype(BF16)
                intra = jnp.dot(att, v[:, vs], preferred_element_type=F32)
                inter = jnp.concatenate(
                    [lax.dot_general(qd[d * c:(d + 1) * c], st_b[d][:, ps], NT, preferred_element_type=F32)
                     for d in range(2)], axis=0)
                outs.append(inter + intra)
                for d in range(2):
                    upd = lax.dot_general(v[d * c:(d + 1) * c, vs], ke[d * c:(d + 1) * c], TN,
                                          preferred_element_type=F32)
                    new[d] = upd if new[d] is None else new[d] + upd
            for d in range(2):
                news[d].append(new[d])
        for d in range(2):
            st_scr[d] = st[d] * jnp.exp(b_tot[d]) + jnp.concatenate(news[d], axis=1)
        o = jnp.concatenate(outs, axis=1)
        of_scr[rf, :] = o[0:c]
        ob_scr[rb, :] = o[c:2 * c]
        return carry

    lax.fori_loop(0, n, step, 0)
    sf_ref[0] = st_scr[...]

    fin = 256

    def finish(i, carry):
        rows = pl.ds(pl.multiple_of(i * fin, fin), fin)
        o = of_scr[rows, :] + ob_scr[rows, :]
        r = r_ref[rows, :].astype(F32)
        gate = r / (1.0 + jnp.exp(-r))
        ng = ng_ref[...]
        for hh in range(GLA_HEADS):
            vs = slice(hh * GLA_DV, (hh + 1) * GLA_DV)
            oh = o[:, vs]
            ms = jnp.mean(oh * oh, axis=-1, keepdims=True)
            o_ref[rows, vs] = (oh * lax.rsqrt(ms + EPS) * ng * gate[:, vs]).astype(BF16)
        return carry

    lax.fori_loop(0, seq // fin, finish, 0)


def _gla(gqk, gv, gr, low, gup, gb, ng, row0, batch, seq, s0=None):
    b0 = row0 // seq
    tokmap = lambda b: (b0 + b, 0)
    hk = GLA_HEADS * GLA_DK
    in_specs = [pl.BlockSpec((seq, 512), tokmap), pl.BlockSpec((seq, 512), tokmap),
                pl.BlockSpec((seq, 512), tokmap), pl.BlockSpec((seq, 128), tokmap),
                pl.BlockSpec((2, 128, hk), lambda b: (0, 0, 0)),
                pl.BlockSpec((2, 1, hk), lambda b: (0, 0, 0)),
                pl.BlockSpec((1, GLA_DV), lambda b: (0, 0))]
    args = [gqk, gv, gr, low, gup, gb, ng]
    if s0 is not None:
        in_specs.append(pl.BlockSpec((1, 2, GLA_DV, hk), lambda b: (b, 0, 0, 0)))
        args.append(s0)
    return pl.pallas_call(
        functools.partial(_gla_body, seq=seq, has_init=s0 is not None),
        grid=(batch,),
        in_specs=in_specs,
        out_specs=[pl.BlockSpec((seq, 512), lambda b: (b, 0)),
                   pl.BlockSpec((1, 2, GLA_DV, hk), lambda b: (b, 0, 0, 0))],
        out_shape=[jax.ShapeDtypeStruct((batch * seq, 512), BF16),
                   jax.ShapeDtypeStruct((batch, 2, GLA_DV, hk), F32)],
        scratch_shapes=[pltpu.VMEM((2, GLA_DV, hk), F32),
                        pltpu.VMEM((seq, 512), F32), pltpu.VMEM((seq, 512), F32),
                        pltpu.VMEM((2, seq, hk), F32)],
        compiler_params=_cp("arbitrary"),
        name="gla_ctx" if s0 is not None else "gla",
    )(*args)


def _outproj0_body(x_ref, mod_ref, odap_ref, odas_ref, ogp_ref, ogs_ref, w_ref, o_ref, *, np_blocks):
    ctx = pl.program_id(0) < np_blocks
    oda = jnp.where(ctx, odap_ref[...], odas_ref[...])
    og = jnp.where(ctx, ogp_ref[...], ogs_ref[...])
    m = (jnp.dot(oda, w_ref[0:512, :], preferred_element_type=F32)
         + jnp.dot(og, w_ref[512:1024, :], preferred_element_type=F32))
    o_ref[...] = x_ref[...] + mod_ref[0][:, 2 * D:3 * D] * m


def _split_specs(block, np_blocks, axis):
    def at(k):
        return tuple(k if a == axis else 0 for a in range(len(block)))
    return (pl.BlockSpec(block, lambda i: at(jnp.minimum(i, np_blocks - 1))),
            pl.BlockSpec(block, lambda i: at(jnp.maximum(i - np_blocks, 0))))


def _outproj0(x, mod, oda_p, oda_s, og_p, og_s, w, np_rows, seq_rows):
    n = x.shape[0]
    tm = ROW_TILE
    npb = np_rows // tm
    rowmap = _mod_row_map(npb, seq_rows // tm)
    sp, ss = _split_specs((tm, 512), npb, 0)
    return pl.pallas_call(
        functools.partial(_outproj0_body, np_blocks=npb),
        grid=(n // tm,),
        in_specs=[pl.BlockSpec((tm, D), lambda i: (i, 0)),
                  pl.BlockSpec((1, 1, 6 * D), lambda i: (rowmap(i), 0, 0)),
                  sp, ss, sp, ss,
                  pl.BlockSpec((D, D), lambda i: (0, 0))],
        out_specs=pl.BlockSpec((tm, D), lambda i: (i, 0)),
        out_shape=jax.ShapeDtypeStruct((n, D), F32),
        compiler_params=_cp("arbitrary"),
        name="outproj0",
    )(x, mod, oda_p, oda_s, og_p, og_s, w)


def _route_body(x_ref, mod_ref, g_ref, wq_ref, sk_ref, ht_o, nsel_o, e1_o, rank_o, e2_o, top_scr, s_scr):
    mod = mod_ref[0]
    h = _norm_mod(x_ref[...], g_ref[...], mod[:, 4 * D:5 * D], mod[:, 3 * D:4 * D])
    ht_o[...] = h.T.astype(BF16)
    q = jnp.dot(h.astype(BF16), wq_ref[...], preferred_element_type=F32).astype(BF16)
    tm = q.shape[0]
    neg = -jnp.inf
    k = PEER_TOPK
    row8 = lax.broadcasted_iota(jnp.int32, (8, tm), 0)
    tiles = [slice(c * 128, (c + 1) * 128) for c in range(tm // 128)]
    for hh in range(PEER_HEADS):
        hs = slice(hh * PEER_NKEYS, (hh + 1) * PEER_NKEYS)
        for t in range(2):
            c0 = (hh * 2 + t) * PEER_NKEYS
            s_scr[t] = lax.dot_general(sk_ref[hh, t], q[:, c0:c0 + PEER_NKEYS], NT,
                                       preferred_element_type=F32)
            for cs in tiles:
                cur = s_scr[t, :, cs]
                rank = jnp.full((PEER_NKEYS, 128), float(k), F32)
                for j in range(k):
                    m = jnp.max(cur, axis=0, keepdims=True)
                    top_scr[t, j:j + 1, cs] = m
                    hit = cur == m
                    if t == 1:
                        rank = jnp.where(hit, float(j), rank)
                    cur = jnp.where(hit, neg, cur)
                if t == 1:
                    rank_o[hs, cs] = rank.astype(BF16)
        t1 = top_scr[0]
        t2 = top_scr[1]
        slabs = [t1[0:1] + t2, t1[1:2] + t2[0:8], t1[2:3] + t2[0:8], t1[3:4] + t2[0:8],
                 t2[0:1] + t1[8:16]]
        for j in range(3):
            slabs.append(jnp.where(row8 >= 4, t2[j:j + 1] + t1[0:8], neg))
        cand = jnp.concatenate(slabs, axis=0)
        top = t1[0:1] + t2[0:1]
        zsum = jnp.zeros_like(top)
        kth = top
        for j in range(k):
            kth = jnp.max(cand, axis=0, keepdims=True)
            zsum = zsum + jnp.exp(kth - top)
            cand = jnp.where(cand == kth, neg, cand)
        zinv = 1.0 / zsum
        for c, cs in enumerate(tiles):
            s1 = s_scr[0, :, cs]
            nsel = jnp.zeros((PEER_NKEYS, 128), F32)
            for j in range(k):
                nsel = jnp.where(s1 + t2[j:j + 1, cs] >= kth[:, cs], float(j + 1), nsel)
            nsel_o[c, hs, :] = nsel
            e1_o[c, hs, :] = jnp.exp(s1 - t1[0:1, cs]) * zinv[:, cs]
            e2_o[hs, cs] = jnp.exp(s_scr[1, :, cs] - t2[0:1, cs]).astype(BF16)


def _route(x, mod, norm_g, wq, sk, layer, np_rows, seq_rows):
    n = x.shape[0]
    tm = ROW_TILE
    rowmap = _mod_row_map(np_rows // tm, seq_rows // tm)
    rt = lambda dt: jax.ShapeDtypeStruct((PEER_HEADS * PEER_NKEYS, n), dt)
    rspec = pl.BlockSpec((PEER_HEADS * PEER_NKEYS, tm), lambda i: (0, i))
    st_ = jax.ShapeDtypeStruct((n // 128, PEER_HEADS * PEER_NKEYS, 128), F32)
    sspec = pl.BlockSpec((tm // 128, PEER_HEADS * PEER_NKEYS, 128), lambda i: (i, 0, 0))
    return pl.pallas_call(
        _route_body,
        grid=(n // tm,),
        in_specs=[pl.BlockSpec((tm, D), lambda i: (i, 0)),
                  pl.BlockSpec((1, 1, 6 * D), lambda i: (rowmap(i), 0, 0)),
                  pl.BlockSpec((1, D), lambda i: (0, 0)),
                  pl.BlockSpec((None, D, 2 * PEER_HEADS * PEER_NKEYS), lambda i: (layer, 0, 0)),
                  pl.BlockSpec((None, PEER_HEADS, 2, PEER_NKEYS, 128), lambda i: (layer, 0, 0, 0, 0))],
        out_specs=[pl.BlockSpec((D, tm), lambda i: (0, i)), sspec, sspec, rspec, rspec],
        out_shape=[jax.ShapeDtypeStruct((D, n), BF16), st_, st_, rt(BF16), rt(BF16)],
        scratch_shapes=[pltpu.VMEM((2, PEER_TOPK, tm), F32), pltpu.VMEM((2, PEER_NKEYS, tm), F32)],
        compiler_params=_cp("arbitrary"),
        name="peer_route",
    )(x, mod, norm_g, wq, sk)


def _peer_body(ht_ref, u_ref, vt_ref, nsel_ref, e1_ref, rank_ref, e2_ref, x_ref, mod_ref, fg_ref,
               o_ref, acc_ref, p_ref, *, final_norm):
    j = pl.program_id(1)

    @pl.when(j == 0)
    def _():
        acc_ref[...] = jnp.zeros(acc_ref.shape, F32)

    nk = PEER_NKEYS
    tok = ht_ref.shape[1]
    na = PEER_EXP // nk
    a0 = pl.multiple_of(j * na, na)
    half = PEER_EXP // 2
    rg = 8
    zero = jnp.zeros((rg, PEER_SUB), BF16)

    def row_bcast(ref, hh, r, ts):
        parts = [jnp.broadcast_to(ref[c, pl.ds(hh * nk + a0, na), :][r:r + 1], (rg, 128))
                 for c in range(ts.start // 128, ts.stop // 128)]
        return jnp.concatenate(parts, axis=1).astype(BF16)

    for t0 in range(0, tok, PEER_SUB):
        ts = slice(t0, t0 + PEER_SUB)
        act = [jnp.dot(u_ref[hf * half:(hf + 1) * half, :], ht_ref[:, ts], preferred_element_type=F32)
               for hf in range(2)]
        for r in range(na):
            gate = [zero] * (nk // rg)
            for hh in range(PEER_HEADS):
                ns = row_bcast(nsel_ref, hh, r, ts)
                e1 = row_bcast(e1_ref, hh, r, ts)
                for g in range(nk // rg):
                    bs = slice(hh * nk + g * rg, hh * nk + (g + 1) * rg)
                    gate[g] = gate[g] + jnp.where(rank_ref[bs, ts] < ns, e2_ref[bs, ts], zero) * e1
            for g in range(nk // rg):
                lo = r * nk + g * rg
                a = act[lo // half][lo % half:lo % half + rg, :]
                p_ref[lo:lo + rg, ts] = _gelu(a).astype(BF16) * gate[g]
        for hf in range(2):
            ds = slice(hf * (D // 2), (hf + 1) * (D // 2))
            acc_ref[ds, ts] += jnp.dot(vt_ref[ds, :], p_ref[:, ts], preferred_element_type=F32)

    @pl.when(j == pl.num_programs(1) - 1)
    def _():
        y = x_ref[...] + mod_ref[0][:, 5 * D:6 * D] * acc_ref[...].T
        if final_norm:
            ms = jnp.mean(y * y, axis=-1, keepdims=True)
            y = y * lax.rsqrt(ms + EPS) * fg_ref[...]
        o_ref[...] = y


def _peer_dense(x, mod, ht, u, vt, layer, nsel, e1, rank, e2, fg, np_rows, seq_rows, final_norm):
    n = x.shape[0]
    tk, ex = PEER_TOK, PEER_EXP
    rowmap = _mod_row_map(np_rows // tk, seq_rows // tk)
    rspec = pl.BlockSpec((PEER_HEADS * PEER_NKEYS, tk), lambda i, j: (0, i))
    sspec = pl.BlockSpec((tk // 128, PEER_HEADS * PEER_NKEYS, 128), lambda i, j: (i, 0, 0))
    return pl.pallas_call(
        functools.partial(_peer_body, final_norm=final_norm),
        grid=(n // tk, u.shape[1] // ex),
        in_specs=[pl.BlockSpec((D, tk), lambda i, j: (0, i)),
                  pl.BlockSpec((None, ex, D), lambda i, j: (layer, j, 0)),
                  pl.BlockSpec((None, D, ex), lambda i, j: (layer, 0, j)),
                  sspec, sspec, rspec, rspec,
                  pl.BlockSpec((tk, D), lambda i, j: (i, 0)),
                  pl.BlockSpec((1, 1, 6 * D), lambda i, j: (rowmap(i), 0, 0)),
                  pl.BlockSpec((1, D), lambda i, j: (0, 0))],
        out_specs=pl.BlockSpec((tk, D), lambda i, j: (i, 0)),
        out_shape=jax.ShapeDtypeStruct((n, D), F32),
        scratch_shapes=[pltpu.VMEM((D, tk), F32), pltpu.VMEM((ex, tk), BF16)],
        compiler_params=_cp("arbitrary", "arbitrary"),
        name="peer_dense",
    )(ht, u, vt, nsel, e1, rank, e2, x, mod, fg)


def _peer_tables(w_query, sub_keys, u, v):
    return w_query.astype(BF16), sub_keys.astype(BF16), u.astype(BF16), v.transpose(0, 2, 1).astype(BF16)


def _peer(x, mod, norm_g, tables, layer, fg, np_rows, seq_rows, final_norm):
    wq, sk, u, vt = tables
    ht, nsel, e1, rank, e2 = _route(x, mod, norm_g, wq, sk, layer, np_rows, seq_rows)
    return _peer_dense(x, mod, ht, u, vt, layer, nsel, e1, rank, e2, fg, np_rows, seq_rows, final_norm)


def _window_of(lane, i):
    return (lane >> 4) == i


def _s5_in_body(x_ref, mod_ref, g_ref, w_ref, o_ref, u_scr):
    mod = mod_ref[0]
    h = _norm_mod(x_ref[...], g_ref[...], mod[:, D:2 * D], mod[:, 0:D]).astype(BF16)
    u = jnp.dot(h, w_ref[...], preferred_element_type=F32)
    cs, per = S5_CHUNK, 128 // S5_CH
    nck = u.shape[0] // cs
    for j in range(D // 128):
        u_scr[j] = u[:, j * 128:(j + 1) * 128]
    lane = lax.broadcasted_iota(jnp.int32, (nck, 128), 1)
    for j in range(D // 128):
        xs = [u_scr[j, pl.ds(s, nck, stride=cs), :] for s in range(cs)]
        for gp in range(per):
            for half in range(cs // per):
                acc = None
                for s8 in range(per):
                    piece = xs[half * per + s8]
                    shift = ((s8 - gp) * S5_CH) % 128
                    if shift:
                        piece = pltpu.roll(piece, shift, 1)
                    acc = piece if acc is None else jnp.where(_window_of(lane, s8), piece, acc)
                o_ref[j * per + gp, :, half * 128:(half + 1) * 128] = acc.astype(BF16)


def _s5_in(x, mod, norm_g, w, np_rows, seq_rows):
    n = x.shape[0]
    tm = ROW_TILE
    rowmap = _mod_row_map(np_rows // tm, seq_rows // tm)
    nck = tm // S5_CHUNK
    return pl.pallas_call(
        _s5_in_body,
        grid=(n // tm,),
        in_specs=[pl.BlockSpec((tm, D), lambda i: (i, 0)),
                  pl.BlockSpec((1, 1, 6 * D), lambda i: (rowmap(i), 0, 0)),
                  pl.BlockSpec((1, D), lambda i: (0, 0)),
                  pl.BlockSpec((D, D), lambda i: (0, 0))],
        out_specs=pl.BlockSpec((S5_GROUPS, nck, S5_CHUNK * S5_CH), lambda i: (0, i, 0)),
        out_shape=jax.ShapeDtypeStruct((S5_GROUPS, n // S5_CHUNK, S5_CHUNK * S5_CH), BF16),
        scratch_shapes=[pltpu.VMEM((D // 128, tm, 128), F32)],
        compiler_params=_cp("arbitrary"),
        name="s5_in",
    )(x, mod, norm_g, w)


def _s5_core_body(u_ref, m_ref, win_ref, cout_ref, pw_ref, *rest, kseq, has_init, emit_states):
    rest = list(rest)
    h0_ref = rest.pop(0) if has_init else None
    y_ref = rest.pop(0)
    sf_ref = rest.pop(0) if emit_states else None
    kb = u_ref.shape[1]
    row = lax.broadcasted_iota(jnp.int32, (kb, 128), 0)
    kk = row & (kseq - 1)

    def cmul(a1, a2, x):
        return a1 * x + a2 * pltpu.roll(x, 64, 1)

    for gi in range(u_ref.shape[0]):
        u = u_ref[gi]
        y = jnp.dot(u, m_ref[gi], preferred_element_type=F32)
        for d in range(2):
            s = jnp.dot(u, win_ref[gi, d], preferred_element_type=F32)
            edge = (kk == 0) if d == 0 else (kk == kseq - 1)
            if has_init:
                h0 = h0_ref[gi, d]
                s = s + jnp.where(edge, cmul(pw_ref[gi, d, 0, 0:1], pw_ref[gi, d, 0, 1:2], h0), 0.0)
            sw = pltpu.roll(s, 64, 1)
            step, lvl = 1, 0
            while step < kseq:
                shift = step if d == 0 else kb - step
                ok = (kk >= step) if d == 0 else (kk < kseq - step)
                a1, a2 = pw_ref[gi, d, lvl, 0:1], pw_ref[gi, d, lvl, 1:2]
                sh, swh = pltpu.roll(s, shift, 0), pltpu.roll(sw, shift, 0)
                s = s + jnp.where(ok, a1 * sh + a2 * swh, 0.0)
                sw = sw + jnp.where(ok, a1 * swh - a2 * sh, 0.0)
                step, lvl = step * 2, lvl + 1
            if emit_states:
                sf_ref[gi, d] = s
            hin = pltpu.roll(s, 1, 0) if d == 0 else pltpu.roll(s, kb - 1, 0)
            if has_init:
                hin = jnp.where(edge, h0, hin)
            else:
                hin = jnp.where(edge, 0.0, hin)
            y = y + jnp.dot(hin.astype(BF16), cout_ref[gi, d], preferred_element_type=F32)
        y_ref[gi] = _gelu(y).astype(BF16)


def _s5_core(ug, mats, row0, rows, kseq, h0=None, emit_states=False):
    m, win, cout, pw = mats
    kb = 128
    b0 = row0 // kb
    nlv = pw.shape[2]
    gb = S5_GROUP_BLOCK
    in_specs = [pl.BlockSpec((gb, kb, 256), lambda g, i: (g, b0 + i, 0)),
                pl.BlockSpec((gb, 256, 256), lambda g, i: (g, 0, 0)),
                pl.BlockSpec((gb, 2, 256, 128), lambda g, i: (g, 0, 0, 0)),
                pl.BlockSpec((gb, 2, 128, 256), lambda g, i: (g, 0, 0, 0)),
                pl.BlockSpec((gb, 2, nlv, 2, 128), lambda g, i: (g, 0, 0, 0, 0))]
    args = [ug, m, win, cout, pw]
    if h0 is not None:
        assert kseq == kb
        in_specs.append(pl.BlockSpec((gb, None, 2, 1, 128), lambda g, i: (g, i, 0, 0, 0)))
        args.append(h0)
    out_specs = [pl.BlockSpec((gb, kb, 256), lambda g, i: (g, i, 0))]
    out_shape = [jax.ShapeDtypeStruct((S5_GROUPS, rows, 256), BF16)]
    if emit_states:
        out_specs.append(pl.BlockSpec((gb, 2, kb, 128), lambda g, i: (g, 0, i, 0)))
        out_shape.append(jax.ShapeDtypeStruct((S5_GROUPS, 2, rows, 128), F32))
    return pl.pallas_call(
        functools.partial(_s5_core_body, kseq=kseq, has_init=h0 is not None, emit_states=emit_states),
        grid=(S5_GROUPS // gb, rows // kb),
        in_specs=in_specs,
        out_specs=out_specs,
        out_shape=out_shape,
        compiler_params=_cp("arbitrary", "arbitrary"),
        name="s5_core_ctx" if h0 is not None else "s5_core",
    )(*args)


def _s5_out_body(x_ref, mod_ref, yp_ref, ys_ref, w_ref, o_ref, y_scr, *, np_blocks):
    cs, per = S5_CHUNK, 128 // S5_CH
    nck = yp_ref.shape[1]
    lane = lax.broadcasted_iota(jnp.int32, (nck, 128), 1)
    ctx = pl.program_id(0) < np_blocks
    for j in range(D // 128):
        gs = slice(j * per, (j + 1) * per)
        yj = jnp.where(ctx, yp_ref[gs], ys_ref[gs])
        src = [[yj[gp, :, half * 128:(half + 1) * 128].astype(F32) for half in range(cs // per)]
               for gp in range(per)]
        for t in range(cs):
            half, t8 = divmod(t, per)
            acc = None
            for gp in range(per):
                piece = src[gp][half]
                shift = ((gp - t8) * S5_CH) % 128
                if shift:
                    piece = pltpu.roll(piece, shift, 1)
                acc = piece if acc is None else jnp.where(_window_of(lane, gp), piece, acc)
            y_scr[j, pl.ds(t, nck, stride=cs), :] = acc
    y = jnp.concatenate([y_scr[j] for j in range(D // 128)], axis=1).astype(BF16)
    zz = jnp.dot(y, w_ref[...], preferred_element_type=F32)
    za = zz[:, 0:D]
    zb = zz[:, D:2 * D]
    o_ref[...] = x_ref[...] + mod_ref[0][:, 2 * D:3 * D] * (za / (1.0 + jnp.exp(-zb)))


def _s5_out(x, mod, y_p, y_s, w, np_rows, seq_rows):
    n = x.shape[0]
    tm = ROW_TILE
    npb = np_rows // tm
    rowmap = _mod_row_map(npb, seq_rows // tm)
    sp, ss = _split_specs((S5_GROUPS, tm // S5_CHUNK, S5_CHUNK * S5_CH), npb, 1)
    return pl.pallas_call(
        functools.partial(_s5_out_body, np_blocks=npb),
        grid=(n // tm,),
        in_specs=[pl.BlockSpec((tm, D), lambda i: (i, 0)),
                  pl.BlockSpec((1, 1, 6 * D), lambda i: (rowmap(i), 0, 0)),
                  sp, ss,
                  pl.BlockSpec((D, 2 * D), lambda i: (0, 0))],
        out_specs=pl.BlockSpec((tm, D), lambda i: (i, 0)),
        out_shape=jax.ShapeDtypeStruct((n, D), F32),
        scratch_shapes=[pltpu.VMEM((D // 128, tm, 128), F32)],
        compiler_params=_cp("arbitrary"),
        name="s5_out",
    )(x, mod, y_p, y_s, w)


def _s5_matrices(a_re, a_im, log_dt, b_re, b_im, c_re, c_im, d_skip):
    cs = S5_CHUNK
    dt = jnp.exp(log_dt)[..., None]
    lr, li = a_re * dt, a_im * dt

    def lam_pow(tau):
        mag = jnp.exp(lr[:, :, None, :] * tau[:, None])
        ang = li[:, :, None, :] * tau[:, None]
        return mag * jnp.cos(ang), mag * jnp.sin(ang)

    l1r, l1i = jnp.exp(lr) * jnp.cos(li), jnp.exp(lr) * jnp.sin(li)
    den = a_re * a_re + a_im * a_im
    cr = ((l1r - 1.0) * a_re + l1i * a_im) / den
    ci = (l1i * a_re - (l1r - 1.0) * a_im) / den
    bt_re = b_re.transpose(0, 1, 3, 2)
    bt_im = b_im.transpose(0, 1, 3, 2)
    bbr = cr[:, :, None, :] * bt_re - ci[:, :, None, :] * bt_im
    bbi = cr[:, :, None, :] * bt_im + ci[:, :, None, :] * bt_re

    pr, pi = lam_pow(jnp.arange(cs + 1, dtype=F32))
    clr = c_re[:, :, None] * pr[:, :, :, None, :] - c_im[:, :, None] * pi[:, :, :, None, :]
    cli = c_re[:, :, None] * pi[:, :, :, None, :] + c_im[:, :, None] * pr[:, :, :, None, :]
    cl = jnp.concatenate([clr[:, :, :cs], -cli[:, :, :cs]], axis=-1).reshape(2, S5_GROUPS, cs * S5_CH, 2 * S5_P)
    bb = jnp.concatenate([bbr, bbi], axis=-1)
    kern = jnp.einsum('dgeq,dgjq->dgej', bb, cl, precision=HI)
    w = cs * S5_CH
    kern_b = kern[1].reshape(S5_GROUPS, S5_CH, cs, S5_CH)[:, :, ::-1].reshape(S5_GROUPS, S5_CH, w)
    zpad = lambda a, lo, hi: jnp.pad(a, ((0, 0), (0, 0), (lo, hi)))
    mf = jnp.stack([zpad(kern[0][:, :, :w - S5_CH * s], S5_CH * s, 0) for s in range(cs)], axis=1)
    mb = jnp.stack([zpad(kern_b[:, :, S5_CH * (cs - 1 - s):], 0, S5_CH * (cs - 1 - s)) for s in range(cs)], axis=1)
    m = (mf + mb).reshape(S5_GROUPS, w, w)
    m = m + jnp.eye(cs * S5_CH, dtype=F32) * jnp.tile(d_skip.reshape(S5_GROUPS, 1, S5_CH), (1, 1, cs))

    def win_dir(d, powers):
        qr_, qi_ = pr[d][:, powers][:, :, None, :], pi[d][:, powers][:, :, None, :]
        wr = qr_ * bbr[d][:, None] - qi_ * bbi[d][:, None]
        wi = qr_ * bbi[d][:, None] + qi_ * bbr[d][:, None]
        return jnp.concatenate([wr, wi], axis=-1).reshape(S5_GROUPS, cs * S5_CH, 2 * S5_P)

    win = jnp.stack([win_dir(0, jnp.arange(cs - 1, -1, -1)), win_dir(1, jnp.arange(cs))], axis=1)

    def cout_dir(d, powers):
        z = jnp.concatenate([clr[d][:, powers], -cli[d][:, powers]], axis=-1)
        return z.reshape(S5_GROUPS, cs * S5_CH, 2 * S5_P).transpose(0, 2, 1)

    cout = jnp.stack([cout_dir(0, jnp.arange(1, cs + 1)), cout_dir(1, jnp.arange(cs, 0, -1))], axis=1)

    qr, qi = lam_pow(cs * (2.0 ** jnp.arange(7, dtype=F32)))
    a1 = jnp.concatenate([qr, qr], axis=-1)
    a2 = jnp.concatenate([-qi, qi], axis=-1)
    pw = jnp.stack([a1, a2], axis=3).transpose(1, 0, 2, 3, 4)
    return m.astype(BF16), win.astype(BF16), cout.astype(BF16), pw


def _rope_tables(dec_seq):
    rows = dec_seq // GRID_W
    row_id = jnp.repeat(jnp.arange(rows), GRID_W).astype(F32)
    col_id = jnp.tile(jnp.arange(GRID_W), rows).astype(F32)
    quarter = DA_DIM // 4
    inv = ROPE_BASE ** (-jnp.arange(quarter, dtype=F32) / quarter)
    ang_r = row_id[:, None] * inv
    ang_c = col_id[:, None] * inv
    ang = jnp.concatenate([ang_r, ang_r, ang_c, ang_c], axis=-1)
    ang = jnp.tile(ang, (1, 2 * DA_HEADS))
    pad = jnp.zeros((ROW_TILE, ang.shape[1]), F32)
    ang = jnp.concatenate([pad, ang], axis=0)
    return jnp.cos(ang), jnp.sin(ang)


def kernel(x_prompt, x_sample, cache_da_k, cache_da_v, state_gla, state_s5, c, c_ctx,
           ada_w, ada_b, norm_g, final_norm_g, mix0_w_in, mix0_w_out, da_lam, da_subln_g,
           gla_gate_up, gla_gate_bias, gla_norm_g, s5_w_in, s5_a_re, s5_a_im, s5_log_dt,
           s5_b_re, s5_b_im, s5_c_re, s5_c_im, s5_d, s5_w_out,
           peer_w_query, peer_sub_keys, peer_u, peer_v):
    bp, lp, _ = x_prompt.shape
    bs, ls, _ = x_sample.shape
    npr, nsr = bp * lp, bs * ls
    assert npr % ls == 0 and ls % ROW_TILE == 0 and lp % ROW_TILE == 0
    x = jnp.concatenate([x_prompt.reshape(npr, D), x_sample.reshape(nsr, D)], axis=0)

    cond = jnp.concatenate([c_ctx[None], c, jnp.zeros((15 - bs, D), F32)], axis=0)
    mod_all = _ada_mod(cond, ada_w, ada_b)
    mods = [mod_all[l].reshape(16, 1, 6 * D) for l in range(mod_all.shape[0])]
    fg = final_norm_g.reshape(1, D)

    lam_init = 0.8 - 0.6 * math.exp(-0.3 * 0)
    cos_t, sin_t = _rope_tables(ls)
    w_in = jnp.pad(mix0_w_in[0], ((0, 0), (0, IN0_PAD - mix0_w_in.shape[2]))).astype(BF16)
    q, kb, vb, k32, v32, gqk, gv, gr, low = _inproj0(
        x, mods[0], norm_g[0, 0].reshape(1, D), w_in, cos_t, sin_t, npr, ls)
    sub_g = da_subln_g[0].reshape(1, 2 * DA_DIM)
    past = cache_da_k.shape[2]
    oda_p = _diff_attention(q, kb, vb, da_lam[0], sub_g, lam_init, 0, bp, lp)
    oda_s = _diff_attention(q, kb, vb, da_lam[0], sub_g, lam_init, npr, bs, ls,
                            cache=(cache_da_k[:, 0].reshape(bs, past, 512), cache_da_v[:, 0].reshape(bs, past, 512)))
    hk = GLA_HEADS * GLA_DK
    gup = jnp.zeros((2, 128, hk), F32)
    for d in range(2):
        gup = gup.at[d, d * GLA_RANK:(d + 1) * GLA_RANK].set(gla_gate_up[0, d])
    gbias = gla_gate_bias[0].reshape(2, 1, hk)
    ng = gla_norm_g[0].reshape(1, GLA_DV)
    s0 = state_gla[:, 0].transpose(0, 1, 4, 2, 3).reshape(bs, 2, GLA_DV, hk)
    og_p, st_p = _gla(gqk, gv, gr, low, gup, gbias, ng, 0, bp, lp)
    og_s, _ = _gla(gqk, gv, gr, low, gup, gbias, ng, npr, bs, ls, s0=s0)
    x = _outproj0(x, mods[0], oda_p, oda_s, og_p, og_s, mix0_w_out[0].astype(BF16), npr, ls)
    tables = _peer_tables(peer_w_query, peer_sub_keys, peer_u, peer_v)
    x = _peer(x, mods[0], norm_g[0, 1].reshape(1, D), tables, 0, fg, npr, ls, final_norm=False)

    ug = _s5_in(x, mods[1], norm_g[1, 0].reshape(1, D), s5_w_in[0].astype(BF16), npr, ls)
    cs = S5_CHUNK
    mats = _s5_matrices(s5_a_re[0], s5_a_im[0], s5_log_dt[0], s5_b_re[0], s5_b_im[0],
                        s5_c_re[0], s5_c_im[0], s5_d[0])
    h0 = state_s5[:, 0]
    h0 = h0.transpose(2, 0, 1, 4, 3).reshape(S5_GROUPS, bs, 2, 1, 2 * S5_P)
    y_p, sf = _s5_core(ug, mats, 0, npr // cs, lp // cs, emit_states=True)
    (y_s,) = _s5_core(ug, mats, npr // cs, nsr // cs, ls // cs, h0=h0)
    x = _s5_out(x, mods[1], y_p, y_s, s5_w_out[0].astype(BF16), npr, ls)
    x = _peer(x, mods[1], norm_g[1, 1].reshape(1, D), tables, 1, fg, npr, ls, final_norm=True)

    y_prompt = x[:npr].reshape(bp, lp, D)
    y_sample = x[npr:].reshape(bs, ls, D)
    new_k = k32[:npr].reshape(bp, 1, lp, 2 * DA_HEADS, DA_DIM)
    new_v = v32[:npr].reshape(bp, 1, lp, DA_HEADS, 2 * DA_DIM)
    new_gla = st_p.reshape(bp, 2, GLA_DV, GLA_HEADS, GLA_DK).transpose(0, 1, 3, 4, 2)[:, None]
    kc = lp // cs
    fin = jnp.stack([sf[:, 0, kc - 1::kc], sf[:, 1, 0::kc]], axis=1)
    new_s5 = fin.reshape(S5_GROUPS, 2, bp, 2, S5_P).transpose(2, 1, 0, 4, 3)[:, None]
    return (y_prompt, y_sample, new_k, new_v, new_gla, new_s5)
```

```python
import functools
import math

import jax
import jax.numpy as jnp
from jax import lax
from jax.experimental import pallas as pl
from jax.experimental.pallas import tpu as pltpu

F32 = jnp.float32
BF16 = jnp.bfloat16
HI = lax.Precision.HIGHEST

D = 1024
EPS = 1e-6
ROPE_BASE = 10000.0
GRID_W = 64
DA_HEADS = 4
DA_DIM = 64
GLA_HEADS = 4
GLA_DK = 64
GLA_DV = 128
GLA_RANK = 16
GLA_TAU = 16.0
GLA_CHUNK = 64
S5_CH = 16
S5_GROUPS = D // S5_CH
S5_P = 64
S5_CHUNK = 16
S5_GROUP_BLOCK = 4
PEER_HEADS = 8
PEER_NKEYS = 128
PEER_TOPK = 16
IN0_PAD = 3200

ROW_TILE = 256
PEER_TOK = 512
PEER_EXP = 2048
PEER_SUB = 256
VMEM_LIMIT = 48 * 1024 * 1024

NT = (((1,), (1,)), ((), ()))
TN = (((0,), (0,)), ((), ()))


def _cp(*sem, flags=None):
    return pltpu.CompilerParams(dimension_semantics=sem, vmem_limit_bytes=VMEM_LIMIT, flags=flags)


def _norm_mod(x, g, sc, sh):
    ms = jnp.mean(x * x, axis=-1, keepdims=True)
    return x * lax.rsqrt(ms + EPS) * g * (1.0 + sc) + sh


_GELU_C2 = -2.0 * 0.7978845608028654 * 1.4426950408889634
_GELU_C1 = _GELU_C2 * 0.044715


def _gelu(x):
    return x / (1.0 + jnp.exp2(x * (_GELU_C1 * (x * x) + _GELU_C2)))


def _mod_row_map(np_blocks, seq_blocks):
    def f(i):
        return jnp.where(i < np_blocks, 0, 1 + (i - np_blocks) // seq_blocks)
    return f


def _ada_body(c_ref, w_ref, b_ref, o_ref):
    c = c_ref[...]
    s = c / (1.0 + jnp.exp(-c))
    o_ref[0] = jnp.dot(s, w_ref[0], precision=HI, preferred_element_type=F32) + b_ref[0]


def _ada_mod(cond, ada_w, ada_b):
    depth, _, n = ada_w.shape
    rows = cond.shape[0]
    tn = 1536
    return pl.pallas_call(
        _ada_body,
        grid=(depth, n // tn),
        in_specs=[pl.BlockSpec((rows, D), lambda l, j: (0, 0)),
                  pl.BlockSpec((1, D, tn), lambda l, j: (l, 0, j)),
                  pl.BlockSpec((1, 1, tn), lambda l, j: (l, 0, j))],
        out_specs=pl.BlockSpec((1, rows, tn), lambda l, j: (l, 0, j)),
        out_shape=jax.ShapeDtypeStruct((depth, rows, n), F32),
        compiler_params=_cp("arbitrary", "arbitrary"),
        name="ada_mod",
    )(cond, ada_w, ada_b.reshape(depth, 1, n))


def _inproj0_body(xp_ref, xs_ref, mod_ref, g_ref, w_ref, cos_ref, sin_ref,
                  q_o, kb_o, vb_o, k32_o, v32_o, gqk_o, gv_o, gr_o, low_o, *, np_blocks):
    mod = mod_ref[0]
    x = jnp.where(pl.program_id(0) < np_blocks, xp_ref[...], xs_ref[...])
    h = _norm_mod(x, g_ref[...], mod[:, D:2 * D], mod[:, 0:D]).astype(BF16)
    y = jnp.dot(h, w_ref[...], preferred_element_type=F32)
    cos = cos_ref[...]
    sin = sin_ref[...]
    lane = lax.broadcasted_iota(jnp.int32, cos.shape, 1)
    first = (lane & 31) < 16

    def rope(t):
        rot = jnp.where(first, -pltpu.roll(t, 512 - 16, 1), pltpu.roll(t, 16, 1))
        return t * cos + rot * sin

    q = rope(y[:, 0:512])
    k = rope(y[:, 512:1024])
    v = y[:, 1024:1536]
    q_o[...] = q.astype(BF16)
    kb_o[...] = k.astype(BF16)
    vb_o[...] = v.astype(BF16)
    k32_o[...] = k
    v32_o[...] = v
    gqk_o[...] = y[:, 1536:2048].astype(BF16)
    gv_o[...] = y[:, 2048:2560].astype(BF16)
    gr_o[...] = y[:, 2560:3072].astype(BF16)
    low_o[...] = y[:, 3072:3200]


def _inproj0(x_p, x_s, mod, norm_g, w, cos_t, sin_t, np_rows, seq_rows):
    n = x_p.shape[0] + x_s.shape[0]
    tm = ROW_TILE
    npb, sqb = np_rows // tm, seq_rows // tm
    rowmap = _mod_row_map(npb, sqb)
    posmap = lambda i: (jnp.where(i < npb, 0, 1 + (i - npb) % sqb), 0)
    tok = lambda w_: pl.BlockSpec((tm, w_), lambda i: (i, 0))
    outs = [(512, BF16), (512, BF16), (512, BF16), (512, F32), (512, F32),
            (512, BF16), (512, BF16), (512, BF16), (128, F32)]
    return pl.pallas_call(
        functools.partial(_inproj0_body, np_blocks=npb),
        grid=(n // tm,),
        in_specs=[*_split_specs((tm, D), npb, 0),
                  pl.BlockSpec((1, 1, 6 * D), lambda i: (rowmap(i), 0, 0)),
                  pl.BlockSpec((1, D), lambda i: (0, 0)),
                  pl.BlockSpec((D, IN0_PAD), lambda i: (0, 0)),
                  pl.BlockSpec((tm, 512), posmap),
                  pl.BlockSpec((tm, 512), posmap)],
        out_specs=[tok(w_) for w_, _ in outs],
        out_shape=[jax.ShapeDtypeStruct((n, w_), dt) for w_, dt in outs],
        compiler_params=_cp("arbitrary"),
        name="inproj0",
    )(x_p, x_s, mod, norm_g, w, cos_t, sin_t)


def _attn_body(lam_ref, sg_ref, q_ref, k_ref, v_ref, *rest, lam_init, has_cache):
    if has_cache:
        kc_ref, vc_ref, o_ref = rest
    else:
        (o_ref,) = rest
    lf = lam_ref[...]
    lam = (jnp.exp(jnp.sum(lf[0:1] * lf[1:2], axis=-1, keepdims=True))
           - jnp.exp(jnp.sum(lf[2:3] * lf[3:4], axis=-1, keepdims=True)) + lam_init)
    q = q_ref[...] * (DA_DIM ** -0.5)
    lane = lax.broadcasted_iota(jnp.int32, q.shape, 1)
    zero = jnp.zeros_like(q)
    qm = (jnp.where(lane < DA_DIM, q, zero), jnp.where(lane >= DA_DIM, q, zero))
    k = k_ref[...]
    s = [lax.dot_general(qm[m], k, NT, preferred_element_type=F32) for m in range(2)]
    mx = [jnp.max(s[m], axis=-1, keepdims=True) for m in range(2)]
    if has_cache:
        kc = kc_ref[...].astype(BF16)
        sc = [lax.dot_general(qm[m], kc, NT, preferred_element_type=F32) for m in range(2)]
        mx = [jnp.maximum(mx[m], jnp.max(sc[m], axis=-1, keepdims=True)) for m in range(2)]
    e = [jnp.exp(s[m] - mx[m]) for m in range(2)]
    z = [jnp.sum(e[m], axis=-1, keepdims=True) for m in range(2)]
    if has_cache:
        ec = [jnp.exp(sc[m] - mx[m]) for m in range(2)]
        z = [z[m] + jnp.sum(ec[m], axis=-1, keepdims=True) for m in range(2)]
    w0 = 1.0 / z[0]
    w1 = lam / z[1]
    o = jnp.dot((e[0] * w0 - e[1] * w1).astype(BF16), v_ref[...], preferred_element_type=F32)
    if has_cache:
        o = o + jnp.dot((ec[0] * w0 - ec[1] * w1).astype(BF16), vc_ref[...].astype(BF16),
                        preferred_element_type=F32)
    ms = jnp.mean(o * o, axis=-1, keepdims=True)
    o_ref[...] = (o * lax.rsqrt(ms + EPS) * sg_ref[...] * (1.0 - lam_init)).astype(o_ref.dtype)


def _diff_attention(q, k, v, da_lam, subln_g, lam_init, row0, batch, seq, cache=None):
    tq = 256
    qb0, kb0 = row0 // tq, row0 // seq
    in_specs = [pl.BlockSpec((4, DA_DIM), lambda b, h, i: (0, 0)),
                pl.BlockSpec((1, 128), lambda b, h, i: (0, 0)),
                pl.BlockSpec((tq, 128), lambda b, h, i: (qb0 + b * (seq // tq) + i, h)),
                pl.BlockSpec((seq, 128), lambda b, h, i: (kb0 + b, h)),
                pl.BlockSpec((seq, 128), lambda b, h, i: (kb0 + b, h))]
    args = [da_lam, subln_g, q, k, v]
    if cache is not None:
        past = cache[0].shape[1]
        in_specs += [pl.BlockSpec((None, past, 128), lambda b, h, i: (b, 0, h))] * 2
        args += list(cache)
    return pl.pallas_call(
        functools.partial(_attn_body, lam_init=lam_init, has_cache=cache is not None),
        grid=(batch, DA_HEADS, seq // tq),
        in_specs=in_specs,
        out_specs=pl.BlockSpec((tq, 128), lambda b, h, i: (b * (seq // tq) + i, h)),
        out_shape=jax.ShapeDtypeStruct((batch * seq, 512), BF16),
        compiler_params=_cp("arbitrary", "arbitrary", "arbitrary"),
        name="diff_attn_ctx" if cache is not None else "diff_attn",
    )(*args)


def _gla_body(qk_ref, v_ref, r_ref, low_ref, gup_ref, gb_ref, ng_ref, *rest, seq, has_init):
    if has_init:
        s0_ref, o_ref, sf_ref, st_scr, of_scr, ob_scr, la_scr = rest
    else:
        o_ref, sf_ref, st_scr, of_scr, ob_scr, la_scr = rest
    c = GLA_CHUNK
    n = seq // c
    hk = GLA_HEADS * GLA_DK
    low = low_ref[...]
    for d in range(2):
        z = jnp.dot(low, gup_ref[d], precision=HI, preferred_element_type=F32) + gb_ref[d]
        la_scr[d] = (jnp.minimum(z, 0.0) - jnp.log(1.0 + jnp.exp(-jnp.abs(z)))) * (1.0 / GLA_TAU)
    if has_init:
        st_scr[...] = s0_ref[0]
    else:
        st_scr[...] = jnp.zeros(st_scr.shape, F32)
    row = lax.broadcasted_iota(jnp.int32, (2 * c, 2 * c), 0)
    col = lax.broadcasted_iota(jnp.int32, (2 * c, 2 * c), 1)
    fwd_blk = jnp.logical_and(row < c, col <= row)
    bwd_blk = jnp.logical_and(jnp.logical_and(row >= c, col >= c), col >= row)
    keep = jnp.logical_or(fwd_blk, bwd_blk)
    keep_f = keep.astype(F32)
    lane = lax.broadcasted_iota(jnp.int32, (2 * c, 128), 1)
    low_half = lane < GLA_DK
    top_rows = lax.broadcasted_iota(jnp.int32, (2 * c, hk), 0) < c

    def step(i, carry):
        rf = pl.ds(pl.multiple_of(i * c, c), c)
        rb = pl.ds(pl.multiple_of((n - 1 - i) * c, c), c)
        qk = jnp.concatenate([qk_ref[rf, :], qk_ref[rb, :]], axis=0).astype(F32)
        q = qk[:, 0:hk] * (GLA_DK ** -0.5)
        k = qk[:, hk:2 * hk]
        v = jnp.concatenate([v_ref[rf, :], v_ref[rb, :]], axis=0)
        g = jnp.concatenate([la_scr[0, rf, :], la_scr[1, rb, :]], axis=0)
        b = jnp.dot(keep_f, g, precision=HI, preferred_element_type=F32)
        b_tot = (b[c - 1:c], b[c:c + 1])
        q_dec = q * jnp.exp(b)
        k_inv = k * jnp.exp(-b)
        k_end = k * jnp.exp(jnp.where(top_rows, b_tot[0], b_tot[1]) - b)
        st = (st_scr[0], st_scr[1])
        st_b = (st[0].astype(BF16), st[1].astype(BF16))
        outs, news = [], ([], [])
        for pair in range(GLA_HEADS // 2):
            ps = slice(pair * 128, (pair + 1) * 128)
            kin = k_inv[:, ps].astype(BF16)
            new = [None, None]
            for sub in range(2):
                hh = pair * 2 + sub
                vs = slice(hh * GLA_DV, (hh + 1) * GLA_DV)
                sel = low_half if sub == 0 else jnp.logical_not(low_half)
                qd = jnp.where(sel, q_dec[:, ps], 0.0).astype(BF16)
                ke = jnp.where(sel, k_end[:, ps], 0.0).astype(BF16)
                att = lax.dot_general(qd, kin, NT, preferred_element_type=F32)
                att = jnp.where(keep, att, 0.0).astype(BF16)
                intra = jnp.dot(att, v[:, vs], preferred_element_type=F32)
                inter = jnp.concatenate(
                    [lax.dot_general(qd[d * c:(d + 1) * c], st_b[d][:, ps], NT, preferred_element_type=F32)
                     for d in range(2)], axis=0)
                outs.append(inter + intra)
                for d in range(2):
                    upd = lax.dot_general(v[d * c:(d + 1) * c, vs], ke[d * c:(d + 1) * c], TN,
                                          preferred_element_type=F32)
                    new[d] = upd if new[d] is None else new[d] + upd
            for d in range(2):
                news[d].append(new[d])
        for d in range(2):
            st_scr[d] = st[d] * jnp.exp(b_tot[d]) + jnp.concatenate(news[d], axis=1)
        o = jnp.concatenate(outs, axis=1)
        of_scr[rf, :] = o[0:c]
        ob_scr[rb, :] = o[c:2 * c]
        return carry

    lax.fori_loop(0, n, step, 0)
    sf_ref[0] = st_scr[...]

    fin = 256

    def finish(i, carry):
        rows = pl.ds(pl.multiple_of(i * fin, fin), fin)
        o = of_scr[rows, :] + ob_scr[rows, :]
        r = r_ref[rows, :].astype(F32)
        gate = r / (1.0 + jnp.exp(-r))
        ng = ng_ref[...]
        for hh in range(GLA_HEADS):
            vs = slice(hh * GLA_DV, (hh + 1) * GLA_DV)
            oh = o[:, vs]
            ms = jnp.mean(oh * oh, axis=-1, keepdims=True)
            o_ref[rows, vs] = (oh * lax.rsqrt(ms + EPS) * ng * gate[:, vs]).astype(BF16)
        return carry

    lax.fori_loop(0, seq // fin, finish, 0)


def _gla(gqk, gv, gr, low, gup, gb, ng, row0, batch, seq, s0=None):
    b0 = row0 // seq
    tokmap = lambda b: (b0 + b, 0)
    hk = GLA_HEADS * GLA_DK
    in_specs = [pl.BlockSpec((seq, 512), tokmap), pl.BlockSpec((seq, 512), tokmap),
                pl.BlockSpec((seq, 512), tokmap), pl.BlockSpec((seq, 128), tokmap),
                pl.BlockSpec((2, 128, hk), lambda b: (0, 0, 0)),
                pl.BlockSpec((2, 1, hk), lambda b: (0, 0, 0)),
                pl.BlockSpec((1, GLA_DV), lambda b: (0, 0))]
    args = [gqk, gv, gr, low, gup, gb, ng]
    if s0 is not None:
        in_specs.append(pl.BlockSpec((1, 2, GLA_DV, hk), lambda b: (b, 0, 0, 0)))
        args.append(s0)
    return pl.pallas_call(
        functools.partial(_gla_body, seq=seq, has_init=s0 is not None),
        grid=(batch,),
        in_specs=in_specs,
        out_specs=[pl.BlockSpec((seq, 512), lambda b: (b, 0)),
                   pl.BlockSpec((1, 2, GLA_DV, hk), lambda b: (b, 0, 0, 0))],
        out_shape=[jax.ShapeDtypeStruct((batch * seq, 512), BF16),
                   jax.ShapeDtypeStruct((batch, 2, GLA_DV, hk), F32)],
        scratch_shapes=[pltpu.VMEM((2, GLA_DV, hk), F32),
                        pltpu.VMEM((seq, 512), F32), pltpu.VMEM((seq, 512), F32),
                        pltpu.VMEM((2, seq, hk), F32)],
        compiler_params=_cp("arbitrary"),
        name="gla_ctx" if s0 is not None else "gla",
    )(*args)


def _outproj0_body(xp_ref, xs_ref, mod_ref, odap_ref, odas_ref, ogp_ref, ogs_ref, w_ref, o_ref, *, np_blocks):
    ctx = pl.program_id(0) < np_blocks
    x = jnp.where(ctx, xp_ref[...], xs_ref[...])
    oda = jnp.where(ctx, odap_ref[...], odas_ref[...])
    og = jnp.where(ctx, ogp_ref[...], ogs_ref[...])
    m = (jnp.dot(oda, w_ref[0:512, :], preferred_element_type=F32)
         + jnp.dot(og, w_ref[512:1024, :], preferred_element_type=F32))
    o_ref[...] = x + mod_ref[0][:, 2 * D:3 * D] * m


def _split_specs(block, np_blocks, axis):
    def at(k):
        return tuple(k if a == axis else 0 for a in range(len(block)))
    return (pl.BlockSpec(block, lambda i: at(jnp.minimum(i, np_blocks - 1))),
            pl.BlockSpec(block, lambda i: at(jnp.maximum(i - np_blocks, 0))))


def _outproj0(x_p, x_s, mod, oda_p, oda_s, og_p, og_s, w, np_rows, seq_rows):
    n = x_p.shape[0] + x_s.shape[0]
    tm = ROW_TILE
    npb = np_rows // tm
    rowmap = _mod_row_map(npb, seq_rows // tm)
    sp, ss = _split_specs((tm, 512), npb, 0)
    return pl.pallas_call(
        functools.partial(_outproj0_body, np_blocks=npb),
        grid=(n // tm,),
        in_specs=[*_split_specs((tm, D), npb, 0),
                  pl.BlockSpec((1, 1, 6 * D), lambda i: (rowmap(i), 0, 0)),
                  sp, ss, sp, ss,
                  pl.BlockSpec((D, D), lambda i: (0, 0))],
        out_specs=pl.BlockSpec((tm, D), lambda i: (i, 0)),
        out_shape=jax.ShapeDtypeStruct((n, D), F32),
        compiler_params=_cp("arbitrary"),
        name="outproj0",
    )(x_p, x_s, mod, oda_p, oda_s, og_p, og_s, w)


def _route_body(x_ref, mod_ref, g_ref, wq_ref, sk_ref, ht_o, nsel_o, e1_o, rank_o, e2_o, top_scr, s_scr):
    mod = mod_ref[0]
    h = _norm_mod(x_ref[...], g_ref[...], mod[:, 4 * D:5 * D], mod[:, 3 * D:4 * D])
    ht_o[...] = h.T.astype(BF16)
    q = jnp.dot(h.astype(BF16), wq_ref[...], preferred_element_type=F32).astype(BF16)
    tm = q.shape[0]
    neg = -jnp.inf
    k = PEER_TOPK
    row8 = lax.broadcasted_iota(jnp.int32, (8, tm), 0)
    tiles = [slice(c * 128, (c + 1) * 128) for c in range(tm // 128)]
    for hh in range(PEER_HEADS):
        hs = slice(hh * PEER_NKEYS, (hh + 1) * PEER_NKEYS)
        for t in range(2):
            c0 = (hh * 2 + t) * PEER_NKEYS
            s_scr[t] = lax.dot_general(sk_ref[hh, t], q[:, c0:c0 + PEER_NKEYS], NT,
                                       preferred_element_type=F32)
            for cs in tiles:
                cur = s_scr[t, :, cs]
                rank = jnp.full((PEER_NKEYS, 128), float(k), F32)
                for j in range(k):
                    m = jnp.max(cur, axis=0, keepdims=True)
                    top_scr[t, j:j + 1, cs] = m
                    hit = cur == m
                    if t == 1:
                        rank = jnp.where(hit, float(j), rank)
                    cur = jnp.where(hit, neg, cur)
                if t == 1:
                    rank_o[hs, cs] = rank.astype(BF16)
        t1 = top_scr[0]
        t2 = top_scr[1]
        slabs = [t1[0:1] + t2, t1[1:2] + t2[0:8], t1[2:3] + t2[0:8], t1[3:4] + t2[0:8],
                 t2[0:1] + t1[8:16]]
        for j in range(3):
            slabs.append(jnp.where(row8 >= 4, t2[j:j + 1] + t1[0:8], neg))
        cand = jnp.concatenate(slabs, axis=0)
        top = t1[0:1] + t2[0:1]
        zsum = jnp.zeros_like(top)
        kth = top
        for j in range(k):
            kth = jnp.max(cand, axis=0, keepdims=True)
            zsum = zsum + jnp.exp(kth - top)
            cand = jnp.where(cand == kth, neg, cand)
        zinv = 1.0 / zsum
        for c, cs in enumerate(tiles):
            s1 = s_scr[0, :, cs]
            nsel = jnp.zeros((PEER_NKEYS, 128), F32)
            for j in range(k):
                nsel = jnp.where(s1 + t2[j:j + 1, cs] >= kth[:, cs], float(j + 1), nsel)
            nsel_o[c, hs, :] = nsel
            e1_o[c, hs, :] = jnp.exp(s1 - t1[0:1, cs]) * zinv[:, cs]
            e2_o[hs, cs] = jnp.exp(s_scr[1, :, cs] - t2[0:1, cs]).astype(BF16)


def _route(x, mod, norm_g, wq, sk, layer, np_rows, seq_rows):
    n = x.shape[0]
    tm = ROW_TILE
    rowmap = _mod_row_map(np_rows // tm, seq_rows // tm)
    rt = lambda dt: jax.ShapeDtypeStruct((PEER_HEADS * PEER_NKEYS, n), dt)
    rspec = pl.BlockSpec((PEER_HEADS * PEER_NKEYS, tm), lambda i: (0, i))
    st_ = jax.ShapeDtypeStruct((n // 128, PEER_HEADS * PEER_NKEYS, 128), F32)
    sspec = pl.BlockSpec((tm // 128, PEER_HEADS * PEER_NKEYS, 128), lambda i: (i, 0, 0))
    return pl.pallas_call(
        _route_body,
        grid=(n // tm,),
        in_specs=[pl.BlockSpec((tm, D), lambda i: (i, 0)),
                  pl.BlockSpec((1, 1, 6 * D), lambda i: (rowmap(i), 0, 0)),
                  pl.BlockSpec((1, D), lambda i: (0, 0)),
                  pl.BlockSpec((None, D, 2 * PEER_HEADS * PEER_NKEYS), lambda i: (layer, 0, 0)),
                  pl.BlockSpec((None, PEER_HEADS, 2, PEER_NKEYS, 128), lambda i: (layer, 0, 0, 0, 0))],
        out_specs=[pl.BlockSpec((D, tm), lambda i: (0, i)), sspec, sspec, rspec, rspec],
        out_shape=[jax.ShapeDtypeStruct((D, n), BF16), st_, st_, rt(BF16), rt(BF16)],
        scratch_shapes=[pltpu.VMEM((2, PEER_TOPK, tm), F32), pltpu.VMEM((2, PEER_NKEYS, tm), F32)],
        compiler_params=_cp("arbitrary"),
        name="peer_route",
    )(x, mod, norm_g, wq, sk)


def _peer_body(ht_ref, u_ref, vt_ref, nsel_ref, e1_ref, rank_ref, e2_ref, x_ref, mod_ref, fg_ref,
               o_ref, acc_ref, p_ref, *, final_norm):
    j = pl.program_id(1)

    @pl.when(j == 0)
    def _():
        acc_ref[...] = jnp.zeros(acc_ref.shape, F32)

    nk = PEER_NKEYS
    tok = ht_ref.shape[1]
    na = PEER_EXP // nk
    a0 = pl.multiple_of(j * na, na)
    half = PEER_EXP // 2
    rg = 8
    zero = jnp.zeros((rg, PEER_SUB), BF16)

    def row_bcast(ref, hh, r, ts):
        parts = [jnp.broadcast_to(ref[c, pl.ds(hh * nk + a0, na), :][r:r + 1], (rg, 128))
                 for c in range(ts.start // 128, ts.stop // 128)]
        return jnp.concatenate(parts, axis=1).astype(BF16)

    for t0 in range(0, tok, PEER_SUB):
        ts = slice(t0, t0 + PEER_SUB)
        act = [jnp.dot(u_ref[hf * half:(hf + 1) * half, :], ht_ref[:, ts], preferred_element_type=F32)
               for hf in range(2)]
        for r in range(na):
            gate = [zero] * (nk // rg)
            for hh in range(PEER_HEADS):
                ns = row_bcast(nsel_ref, hh, r, ts)
                e1 = row_bcast(e1_ref, hh, r, ts)
                for g in range(nk // rg):
                    bs = slice(hh * nk + g * rg, hh * nk + (g + 1) * rg)
                    gate[g] = gate[g] + jnp.where(rank_ref[bs, ts] < ns, e2_ref[bs, ts], zero) * e1
            for g in range(nk // rg):
                lo = r * nk + g * rg
                a = act[lo // half][lo % half:lo % half + rg, :]
                p_ref[lo:lo + rg, ts] = _gelu(a).astype(BF16) * gate[g]
        for hf in range(2):
            ds = slice(hf * (D // 2), (hf + 1) * (D // 2))
            acc_ref[ds, ts] += jnp.dot(vt_ref[ds, :], p_ref[:, ts], preferred_element_type=F32)

    @pl.when(j == pl.num_programs(1) - 1)
    def _():
        y = x_ref[...] + mod_ref[0][:, 5 * D:6 * D] * acc_ref[...].T
        if final_norm:
            ms = jnp.mean(y * y, axis=-1, keepdims=True)
            y = y * lax.rsqrt(ms + EPS) * fg_ref[...]
        o_ref[...] = y


def _peer_dense(x, mod, ht, u, vt, layer, nsel, e1, rank, e2, fg, np_rows, seq_rows, final_norm,
                row0=0, rows=None):
    rows = x.shape[0] if rows is None else rows
    tk, ex = PEER_TOK, PEER_EXP
    b0 = row0 // tk
    rowmap = _mod_row_map(np_rows // tk, seq_rows // tk)
    rspec = pl.BlockSpec((PEER_HEADS * PEER_NKEYS, tk), lambda i, j: (0, b0 + i))
    sspec = pl.BlockSpec((tk // 128, PEER_HEADS * PEER_NKEYS, 128), lambda i, j: (b0 + i, 0, 0))
    return pl.pallas_call(
        functools.partial(_peer_body, final_norm=final_norm),
        grid=(rows // tk, u.shape[1] // ex),
        in_specs=[pl.BlockSpec((D, tk), lambda i, j: (0, b0 + i)),
                  pl.BlockSpec((None, ex, D), lambda i, j: (layer, j, 0)),
                  pl.BlockSpec((None, D, ex), lambda i, j: (layer, 0, j)),
                  sspec, sspec, rspec, rspec,
                  pl.BlockSpec((tk, D), lambda i, j: (b0 + i, 0)),
                  pl.BlockSpec((1, 1, 6 * D), lambda i, j: (rowmap(b0 + i), 0, 0)),
                  pl.BlockSpec((1, D), lambda i, j: (0, 0))],
        out_specs=pl.BlockSpec((tk, D), lambda i, j: (i, 0)),
        out_shape=jax.ShapeDtypeStruct((rows, D), F32),
        scratch_shapes=[pltpu.VMEM((D, tk), F32), pltpu.VMEM((ex, tk), BF16)],
        compiler_params=_cp("arbitrary", "arbitrary"),
        name="peer_dense",
    )(ht, u, vt, nsel, e1, rank, e2, x, mod, fg)


def _peer_tables(w_query, sub_keys, u, v):
    return w_query.astype(BF16), sub_keys.astype(BF16), u.astype(BF16), v.transpose(0, 2, 1).astype(BF16)


def _peer(x, mod, norm_g, tables, layer, fg, np_rows, seq_rows, final_norm, split=False):
    wq, sk, u, vt = tables
    ht, nsel, e1, rank, e2 = _route(x, mod, norm_g, wq, sk, layer, np_rows, seq_rows)
    dense = functools.partial(_peer_dense, x, mod, ht, u, vt, layer, nsel, e1, rank, e2, fg,
                              np_rows, seq_rows, final_norm)
    if split:
        return dense(row0=0, rows=np_rows), dense(row0=np_rows, rows=x.shape[0] - np_rows)
    return dense()


def _window_of(lane, i):
    return (lane >> 4) == i


def _s5_in_body(x_ref, mod_ref, g_ref, w_ref, o_ref, u_scr):
    mod = mod_ref[0]
    h = _norm_mod(x_ref[...], g_ref[...], mod[:, D:2 * D], mod[:, 0:D]).astype(BF16)
    u = jnp.dot(h, w_ref[...], preferred_element_type=F32)
    cs, per = S5_CHUNK, 128 // S5_CH
    nck = u.shape[0] // cs
    for j in range(D // 128):
        u_scr[j] = u[:, j * 128:(j + 1) * 128]
    lane = lax.broadcasted_iota(jnp.int32, (nck, 128), 1)
    for j in range(D // 128):
        xs = [u_scr[j, pl.ds(s, nck, stride=cs), :] for s in range(cs)]
        for gp in range(per):
            for half in range(cs // per):
                acc = None
                for s8 in range(per):
                    piece = xs[half * per + s8]
                    shift = ((s8 - gp) * S5_CH) % 128
                    if shift:
                        piece = pltpu.roll(piece, shift, 1)
                    acc = piece if acc is None else jnp.where(_window_of(lane, s8), piece, acc)
                o_ref[j * per + gp, :, half * 128:(half + 1) * 128] = acc.astype(BF16)


def _s5_in(x, mod, norm_g, w, np_rows, seq_rows):
    n = x.shape[0]
    tm = ROW_TILE
    rowmap = _mod_row_map(np_rows // tm, seq_rows // tm)
    nck = tm // S5_CHUNK
    return pl.pallas_call(
        _s5_in_body,
        grid=(n // tm,),
        in_specs=[pl.BlockSpec((tm, D), lambda i: (i, 0)),
                  pl.BlockSpec((1, 1, 6 * D), lambda i: (rowmap(i), 0, 0)),
                  pl.BlockSpec((1, D), lambda i: (0, 0)),
                  pl.BlockSpec((D, D), lambda i: (0, 0))],
        out_specs=pl.BlockSpec((S5_GROUPS, nck, S5_CHUNK * S5_CH), lambda i: (0, i, 0)),
        out_shape=jax.ShapeDtypeStruct((S5_GROUPS, n // S5_CHUNK, S5_CHUNK * S5_CH), BF16),
        scratch_shapes=[pltpu.VMEM((D // 128, tm, 128), F32)],
        compiler_params=_cp("arbitrary"),
        name="s5_in",
    )(x, mod, norm_g, w)


def _s5_core_body(u_ref, m_ref, win_ref, cout_ref, pw_ref, *rest, kseq, has_init, emit_states):
    rest = list(rest)
    h0_ref = rest.pop(0) if has_init else None
    y_ref = rest.pop(0)
    sf_ref = rest.pop(0) if emit_states else None
    kb = u_ref.shape[1]
    row = lax.broadcasted_iota(jnp.int32, (kb, 128), 0)
    kk = row & (kseq - 1)

    def cmul(a1, a2, x):
        return a1 * x + a2 * pltpu.roll(x, 64, 1)

    for gi in range(u_ref.shape[0]):
        u = u_ref[gi]
        y = jnp.dot(u, m_ref[gi], preferred_element_type=F32)
        for d in range(2):
            s = jnp.dot(u, win_ref[gi, d], preferred_element_type=F32)
            edge = (kk == 0) if d == 0 else (kk == kseq - 1)
            if has_init:
                h0 = h0_ref[gi, d]
                s = s + jnp.where(edge, cmul(pw_ref[gi, d, 0, 0:1], pw_ref[gi, d, 0, 1:2], h0), 0.0)
            sw = pltpu.roll(s, 64, 1)
            step, lvl = 1, 0
            while step < kseq:
                shift = step if d == 0 else kb - step
                ok = (kk >= step) if d == 0 else (kk < kseq - step)
                a1, a2 = pw_ref[gi, d, lvl, 0:1], pw_ref[gi, d, lvl, 1:2]
                sh, swh = pltpu.roll(s, shift, 0), pltpu.roll(sw, shift, 0)
                s = s + jnp.where(ok, a1 * sh + a2 * swh, 0.0)
                sw = sw + jnp.where(ok, a1 * swh - a2 * sh, 0.0)
                step, lvl = step * 2, lvl + 1
            if emit_states:
                sf_ref[gi, d] = s
            hin = pltpu.roll(s, 1, 0) if d == 0 else pltpu.roll(s, kb - 1, 0)
            if has_init:
                hin = jnp.where(edge, h0, hin)
            else:
                hin = jnp.where(edge, 0.0, hin)
            y = y + jnp.dot(hin.astype(BF16), cout_ref[gi, d], preferred_element_type=F32)
        y_ref[gi] = _gelu(y).astype(BF16)


def _s5_core(ug, mats, row0, rows, kseq, h0=None, emit_states=False):
    m, win, cout, pw = mats
    kb = 128
    b0 = row0 // kb
    nlv = pw.shape[2]
    gb = S5_GROUP_BLOCK
    in_specs = [pl.BlockSpec((gb, kb, 256), lambda g, i: (g, b0 + i, 0)),
                pl.BlockSpec((gb, 256, 256), lambda g, i: (g, 0, 0)),
                pl.BlockSpec((gb, 2, 256, 128), lambda g, i: (g, 0, 0, 0)),
                pl.BlockSpec((gb, 2, 128, 256), lambda g, i: (g, 0, 0, 0)),
                pl.BlockSpec((gb, 2, nlv, 2, 128), lambda g, i: (g, 0, 0, 0, 0))]
    args = [ug, m, win, cout, pw]
    if h0 is not None:
        assert kseq == kb
        in_specs.append(pl.BlockSpec((gb, None, 2, 1, 128), lambda g, i: (g, i, 0, 0, 0)))
        args.append(h0)
    out_specs = [pl.BlockSpec((gb, kb, 256), lambda g, i: (g, i, 0))]
    out_shape = [jax.ShapeDtypeStruct((S5_GROUPS, rows, 256), BF16)]
    if emit_states:
        out_specs.append(pl.BlockSpec((gb, 2, kb, 128), lambda g, i: (g, 0, i, 0)))
        out_shape.append(jax.ShapeDtypeStruct((S5_GROUPS, 2, rows, 128), F32))
    return pl.pallas_call(
        functools.partial(_s5_core_body, kseq=kseq, has_init=h0 is not None, emit_states=emit_states),
        grid=(S5_GROUPS // gb, rows // kb),
        in_specs=in_specs,
        out_specs=out_specs,
        out_shape=out_shape,
        compiler_params=_cp("arbitrary", "arbitrary"),
        name="s5_core_ctx" if h0 is not None else "s5_core",
    )(*args)


def _s5_out_body(x_ref, mod_ref, yp_ref, ys_ref, w_ref, o_ref, y_scr, *, np_blocks):
    cs, per = S5_CHUNK, 128 // S5_CH
    nck = yp_ref.shape[1]
    lane = lax.broadcasted_iota(jnp.int32, (nck, 128), 1)
    ctx = pl.program_id(0) < np_blocks
    for j in range(D // 128):
        gs = slice(j * per, (j + 1) * per)
        yj = jnp.where(ctx, yp_ref[gs], ys_ref[gs])
        src = [[yj[gp, :, half * 128:(half + 1) * 128].astype(F32) for half in range(cs // per)]
               for gp in range(per)]
        for t in range(cs):
            half, t8 = divmod(t, per)
            acc = None
            for gp in range(per):
                piece = src[gp][half]
                shift = ((gp - t8) * S5_CH) % 128
                if shift:
                    piece = pltpu.roll(piece, shift, 1)
                acc = piece if acc is None else jnp.where(_window_of(lane, gp), piece, acc)
            y_scr[j, pl.ds(t, nck, stride=cs), :] = acc
    y = jnp.concatenate([y_scr[j] for j in range(D // 128)], axis=1).astype(BF16)
    zz = jnp.dot(y, w_ref[...], preferred_element_type=F32)
    za = zz[:, 0:D]
    zb = zz[:, D:2 * D]
    o_ref[...] = x_ref[...] + mod_ref[0][:, 2 * D:3 * D] * (za / (1.0 + jnp.exp(-zb)))


def _s5_out(x, mod, y_p, y_s, w, np_rows, seq_rows):
    n = x.shape[0]
    tm = ROW_TILE
    npb = np_rows // tm
    rowmap = _mod_row_map(npb, seq_rows // tm)
    sp, ss = _split_specs((S5_GROUPS, tm // S5_CHUNK, S5_CHUNK * S5_CH), npb, 1)
    return pl.pallas_call(
        functools.partial(_s5_out_body, np_blocks=npb),
        grid=(n // tm,),
        in_specs=[pl.BlockSpec((tm, D), lambda i: (i, 0)),
                  pl.BlockSpec((1, 1, 6 * D), lambda i: (rowmap(i), 0, 0)),
                  sp, ss,
                  pl.BlockSpec((D, 2 * D), lambda i: (0, 0))],
        out_specs=pl.BlockSpec((tm, D), lambda i: (i, 0)),
        out_shape=jax.ShapeDtypeStruct((n, D), F32),
        scratch_shapes=[pltpu.VMEM((D // 128, tm, 128), F32)],
        compiler_params=_cp("arbitrary"),
        name="s5_out",
    )(x, mod, y_p, y_s, w)


def _s5_matrices(a_re, a_im, log_dt, b_re, b_im, c_re, c_im, d_skip):
    cs = S5_CHUNK
    dt = jnp.exp(log_dt)[..., None]
    lr, li = a_re * dt, a_im * dt

    def lam_pow(tau):
        mag = jnp.exp(lr[:, :, None, :] * tau[:, None])
        ang = li[:, :, None, :] * tau[:, None]
        return mag * jnp.cos(ang), mag * jnp.sin(ang)

    l1r, l1i = jnp.exp(lr) * jnp.cos(li), jnp.exp(lr) * jnp.sin(li)
    den = a_re * a_re + a_im * a_im
    cr = ((l1r - 1.0) * a_re + l1i * a_im) / den
    ci = (l1i * a_re - (l1r - 1.0) * a_im) / den
    bt_re = b_re.transpose(0, 1, 3, 2)
    bt_im = b_im.transpose(0, 1, 3, 2)
    bbr = cr[:, :, None, :] * bt_re - ci[:, :, None, :] * bt_im
    bbi = cr[:, :, None, :] * bt_im + ci[:, :, None, :] * bt_re

    pr, pi = lam_pow(jnp.arange(cs + 1, dtype=F32))
    clr = c_re[:, :, None] * pr[:, :, :, None, :] - c_im[:, :, None] * pi[:, :, :, None, :]
    cli = c_re[:, :, None] * pi[:, :, :, None, :] + c_im[:, :, None] * pr[:, :, :, None, :]
    cl = jnp.concatenate([clr[:, :, :cs], -cli[:, :, :cs]], axis=-1).reshape(2, S5_GROUPS, cs * S5_CH, 2 * S5_P)
    bb = jnp.concatenate([bbr, bbi], axis=-1)
    kern = jnp.einsum('dgeq,dgjq->dgej', bb, cl, precision=HI)
    w = cs * S5_CH
    kern_b = kern[1].reshape(S5_GROUPS, S5_CH, cs, S5_CH)[:, :, ::-1].reshape(S5_GROUPS, S5_CH, w)
    zpad = lambda a, lo, hi: jnp.pad(a, ((0, 0), (0, 0), (lo, hi)))
    mf = jnp.stack([zpad(kern[0][:, :, :w - S5_CH * s], S5_CH * s, 0) for s in range(cs)], axis=1)
    mb = jnp.stack([zpad(kern_b[:, :, S5_CH * (cs - 1 - s):], 0, S5_CH * (cs - 1 - s)) for s in range(cs)], axis=1)
    m = (mf + mb).reshape(S5_GROUPS, w, w)
    m = m + jnp.eye(cs * S5_CH, dtype=F32) * jnp.tile(d_skip.reshape(S5_GROUPS, 1, S5_CH), (1, 1, cs))

    def win_dir(d, powers):
        qr_, qi_ = pr[d][:, powers][:, :, None, :], pi[d][:, powers][:, :, None, :]
        wr = qr_ * bbr[d][:, None] - qi_ * bbi[d][:, None]
        wi = qr_ * bbi[d][:, None] + qi_ * bbr[d][:, None]
        return jnp.concatenate([wr, wi], axis=-1).reshape(S5_GROUPS, cs * S5_CH, 2 * S5_P)

    win = jnp.stack([win_dir(0, jnp.arange(cs - 1, -1, -1)), win_dir(1, jnp.arange(cs))], axis=1)

    def cout_dir(d, powers):
        z = jnp.concatenate([clr[d][:, powers], -cli[d][:, powers]], axis=-1)
        return z.reshape(S5_GROUPS, cs * S5_CH, 2 * S5_P).transpose(0, 2, 1)

    cout = jnp.stack([cout_dir(0, jnp.arange(1, cs + 1)), cout_dir(1, jnp.arange(cs, 0, -1))], axis=1)

    qr, qi = lam_pow(cs * (2.0 ** jnp.arange(7, dtype=F32)))
    a1 = jnp.concatenate([qr, qr], axis=-1)
    a2 = jnp.concatenate([-qi, qi], axis=-1)
    pw = jnp.stack([a1, a2], axis=3).transpose(1, 0, 2, 3, 4)
    return m.astype(BF16), win.astype(BF16), cout.astype(BF16), pw


def _rope_tables(dec_seq):
    rows = dec_seq // GRID_W
    row_id = jnp.repeat(jnp.arange(rows), GRID_W).astype(F32)
    col_id = jnp.tile(jnp.arange(GRID_W), rows).astype(F32)
    quarter = DA_DIM // 4
    inv = ROPE_BASE ** (-jnp.arange(quarter, dtype=F32) / quarter)
    ang_r = row_id[:, None] * inv
    ang_c = col_id[:, None] * inv
    ang = jnp.concatenate([ang_r, ang_r, ang_c, ang_c], axis=-1)
    ang = jnp.tile(ang, (1, 2 * DA_HEADS))
    pad = jnp.zeros((ROW_TILE, ang.shape[1]), F32)
    ang = jnp.concatenate([pad, ang], axis=0)
    return jnp.cos(ang), jnp.sin(ang)


def kernel(x_prompt, x_sample, cache_da_k, cache_da_v, state_gla, state_s5, c, c_ctx,
           ada_w, ada_b, norm_g, final_norm_g, mix0_w_in, mix0_w_out, da_lam, da_subln_g,
           gla_gate_up, gla_gate_bias, gla_norm_g, s5_w_in, s5_a_re, s5_a_im, s5_log_dt,
           s5_b_re, s5_b_im, s5_c_re, s5_c_im, s5_d, s5_w_out,
           peer_w_query, peer_sub_keys, peer_u, peer_v):
    bp, lp, _ = x_prompt.shape
    bs, ls, _ = x_sample.shape
    npr, nsr = bp * lp, bs * ls
    assert npr % ls == 0 and ls % ROW_TILE == 0 and lp % ROW_TILE == 0
    x_p, x_s = x_prompt.reshape(npr, D), x_sample.reshape(nsr, D)

    cond = jnp.concatenate([c_ctx[None], c, jnp.zeros((15 - bs, D), F32)], axis=0)
    mod_all = _ada_mod(cond, ada_w, ada_b)
    mods = [mod_all[l].reshape(16, 1, 6 * D) for l in range(mod_all.shape[0])]
    fg = final_norm_g.reshape(1, D)

    lam_init = 0.8 - 0.6 * math.exp(-0.3 * 0)
    cos_t, sin_t = _rope_tables(ls)
    w_in = jnp.pad(mix0_w_in[0], ((0, 0), (0, IN0_PAD - mix0_w_in.shape[2]))).astype(BF16)
    q, kb, vb, k32, v32, gqk, gv, gr, low = _inproj0(
        x_p, x_s, mods[0], norm_g[0, 0].reshape(1, D), w_in, cos_t, sin_t, npr, ls)
    sub_g = da_subln_g[0].reshape(1, 2 * DA_DIM)
    past = cache_da_k.shape[2]
    oda_p = _diff_attention(q, kb, vb, da_lam[0], sub_g, lam_init, 0, bp, lp)
    oda_s = _diff_attention(q, kb, vb, da_lam[0], sub_g, lam_init, npr, bs, ls,
                            cache=(cache_da_k[:, 0].reshape(bs, past, 512), cache_da_v[:, 0].reshape(bs, past, 512)))
    hk = GLA_HEADS * GLA_DK
    gup = jnp.zeros((2, 128, hk), F32)
    for d in range(2):
        gup = gup.at[d, d * GLA_RANK:(d + 1) * GLA_RANK].set(gla_gate_up[0, d])
    gbias = gla_gate_bias[0].reshape(2, 1, hk)
    ng = gla_norm_g[0].reshape(1, GLA_DV)
    s0 = state_gla[:, 0].transpose(0, 1, 4, 2, 3).reshape(bs, 2, GLA_DV, hk)
    og_p, st_p = _gla(gqk, gv, gr, low, gup, gbias, ng, 0, bp, lp)
    og_s, _ = _gla(gqk, gv, gr, low, gup, gbias, ng, npr, bs, ls, s0=s0)
    x = _outproj0(x_p, x_s, mods[0], oda_p, oda_s, og_p, og_s, mix0_w_out[0].astype(BF16), npr, ls)
    tables = _peer_tables(peer_w_query, peer_sub_keys, peer_u, peer_v)
    x = _peer(x, mods[0], norm_g[0, 1].reshape(1, D), tables, 0, fg, npr, ls, final_norm=False)

    ug = _s5_in(x, mods[1], norm_g[1, 0].reshape(1, D), s5_w_in[0].astype(BF16), npr, ls)
    cs = S5_CHUNK
    mats = _s5_matrices(s5_a_re[0], s5_a_im[0], s5_log_dt[0], s5_b_re[0], s5_b_im[0],
                        s5_c_re[0], s5_c_im[0], s5_d[0])
    h0 = state_s5[:, 0]
    h0 = h0.transpose(2, 0, 1, 4, 3).reshape(S5_GROUPS, bs, 2, 1, 2 * S5_P)
    y_p, sf = _s5_core(ug, mats, 0, npr // cs, lp // cs, emit_states=True)
    (y_s,) = _s5_core(ug, mats, npr // cs, nsr // cs, ls // cs, h0=h0)
    x = _s5_out(x, mods[1], y_p, y_s, s5_w_out[0].astype(BF16), npr, ls)
    y_p, y_s = _peer(x, mods[1], norm_g[1, 1].reshape(1, D), tables, 1, fg, npr, ls, final_norm=True, split=True)

    y_prompt = y_p.reshape(bp, lp, D)
    y_sample = y_s.reshape(bs, ls, D)
    new_k = k32[:npr].reshape(bp, 1, lp, 2 * DA_HEADS, DA_DIM)
    new_v = v32[:npr].reshape(bp, 1, lp, DA_HEADS, 2 * DA_DIM)
    new_gla = st_p.reshape(bp, 2, GLA_DV, GLA_HEADS, GLA_DK).transpose(0, 1, 3, 4, 2)[:, None]
    kc = lp // cs
    fin = jnp.stack([sf[:, 0, kc - 1::kc], sf[:, 1, 0::kc]], axis=1)
    new_s5 = fin.reshape(S5_GROUPS, 2, bp, 2, S5_P).transpose(2, 1, 0, 4, 3)[:, None]
    return (y_prompt, y_sample, new_k, new_v, new_gla, new_s5)
```

```python
import functools
import math

import jax
import jax.numpy as jnp
from jax import lax
from jax.experimental import pallas as pl
from jax.experimental.pallas import tpu as pltpu

F32 = jnp.float32
BF16 = jnp.bfloat16
HI = lax.Precision.HIGHEST

D = 1024
EPS = 1e-6
ROPE_BASE = 10000.0
GRID_W = 64
DA_HEADS = 4
DA_DIM = 64
GLA_HEADS = 4
GLA_DK = 64
GLA_DV = 128
GLA_RANK = 16
GLA_TAU = 16.0
GLA_CHUNK = 64
S5_CH = 16
S5_GROUPS = D // S5_CH
S5_P = 64
S5_CHUNK = 16
S5_GROUP_BLOCK = 4
PEER_HEADS = 8
PEER_NKEYS = 128
PEER_TOPK = 16
IN0_PAD = 3200

ROW_TILE = 256
PEER_TOK = 512
PEER_EXP = 2048
PEER_SUB = 256
VMEM_LIMIT = 48 * 1024 * 1024

NT = (((1,), (1,)), ((), ()))
TN = (((0,), (0,)), ((), ()))


def _cp(*sem):
    return pltpu.CompilerParams(dimension_semantics=sem, vmem_limit_bytes=VMEM_LIMIT)


def _norm_mod(x, g, sc, sh):
    ms = jnp.mean(x * x, axis=-1, keepdims=True)
    return x * lax.rsqrt(ms + EPS) * g * (1.0 + sc) + sh


_GELU_C2 = -2.0 * 0.7978845608028654 * 1.4426950408889634
_GELU_C1 = _GELU_C2 * 0.044715


def _gelu(x):
    return x / (1.0 + jnp.exp2(x * (_GELU_C1 * (x * x) + _GELU_C2)))


def _mod_row_map(np_blocks, seq_blocks):
    def f(i):
        return jnp.where(i < np_blocks, 0, 1 + (i - np_blocks) // seq_blocks)
    return f


def _ada_body(c_ref, w_ref, b_ref, o_ref):
    c = c_ref[...]
    s = c / (1.0 + jnp.exp(-c))
    o_ref[0] = jnp.dot(s, w_ref[0], precision=HI, preferred_element_type=F32) + b_ref[0]


def _ada_mod(cond, ada_w, ada_b):
    depth, _, n = ada_w.shape
    rows = cond.shape[0]
    tn = 1536
    return pl.pallas_call(
        _ada_body,
        grid=(depth, n // tn),
        in_specs=[pl.BlockSpec((rows, D), lambda l, j: (0, 0)),
                  pl.BlockSpec((1, D, tn), lambda l, j: (l, 0, j)),
                  pl.BlockSpec((1, 1, tn), lambda l, j: (l, 0, j))],
        out_specs=pl.BlockSpec((1, rows, tn), lambda l, j: (l, 0, j)),
        out_shape=jax.ShapeDtypeStruct((depth, rows, n), F32),
        compiler_params=_cp("arbitrary", "arbitrary"),
        name="ada_mod",
    )(cond, ada_w, ada_b.reshape(depth, 1, n))


def _inproj0_body(xp_ref, xs_ref, mod_ref, g_ref, w_ref, cos_ref, sin_ref,
                  q_o, kb_o, vb_o, k32_o, v32_o, gqk_o, gv_o, gr_o, low_o, *, np_blocks):
    mod = mod_ref[0]
    x = jnp.where(pl.program_id(0) < np_blocks, xp_ref[...], xs_ref[...])
    h = _norm_mod(x, g_ref[...], mod[:, D:2 * D], mod[:, 0:D]).astype(BF16)
    y = jnp.dot(h, w_ref[...], preferred_element_type=F32)
    cos = cos_ref[...]
    sin = sin_ref[...]
    lane = lax.broadcasted_iota(jnp.int32, cos.shape, 1)
    first = (lane & 31) < 16

    def rope(t):
        rot = jnp.where(first, -pltpu.roll(t, 512 - 16, 1), pltpu.roll(t, 16, 1))
        return t * cos + rot * sin

    q = rope(y[:, 0:512])
    k = rope(y[:, 512:1024])
    v = y[:, 1024:1536]
    q_o[...] = q.astype(BF16)
    kb_o[...] = k.astype(BF16)
    vb_o[...] = v.astype(BF16)
    k32_o[...] = k
    v32_o[...] = v
    gqk_o[...] = y[:, 1536:2048].astype(BF16)
    gv_o[...] = y[:, 2048:2560].astype(BF16)
    gr_o[...] = y[:, 2560:3072].astype(BF16)
    low_o[...] = y[:, 3072:3200]


def _inproj0(x_p, x_s, mod, norm_g, w, cos_t, sin_t, np_rows, seq_rows):
    n = x_p.shape[0] + x_s.shape[0]
    tm = ROW_TILE
    npb, sqb = np_rows // tm, seq_rows // tm
    rowmap = _mod_row_map(npb, sqb)
    posmap = lambda i: (jnp.where(i < npb, 0, 1 + (i - npb) % sqb), 0)
    tok = lambda w_: pl.BlockSpec((tm, w_), lambda i: (i, 0))
    outs = [(512, BF16), (512, BF16), (512, BF16), (512, F32), (512, F32),
            (512, BF16), (512, BF16), (512, BF16), (128, F32)]
    return pl.pallas_call(
        functools.partial(_inproj0_body, np_blocks=npb),
        grid=(n // tm,),
        in_specs=[*_split_specs((tm, D), npb, 0),
                  pl.BlockSpec((1, 1, 6 * D), lambda i: (rowmap(i), 0, 0)),
                  pl.BlockSpec((1, D), lambda i: (0, 0)),
                  pl.BlockSpec((D, IN0_PAD), lambda i: (0, 0)),
                  pl.BlockSpec((tm, 512), posmap),
                  pl.BlockSpec((tm, 512), posmap)],
        out_specs=[tok(w_) for w_, _ in outs],
        out_shape=[jax.ShapeDtypeStruct((n, w_), dt) for w_, dt in outs],
        compiler_params=_cp("arbitrary"),
        name="inproj0",
    )(x_p, x_s, mod, norm_g, w, cos_t, sin_t)


def _attn_body(lam_ref, sg_ref, q_ref, k_ref, v_ref, *rest, lam_init, has_cache):
    if has_cache:
        kc_ref, vc_ref, o_ref = rest
    else:
        (o_ref,) = rest
    lf = lam_ref[...]
    lam = (jnp.exp(jnp.sum(lf[0:1] * lf[1:2], axis=-1, keepdims=True))
           - jnp.exp(jnp.sum(lf[2:3] * lf[3:4], axis=-1, keepdims=True)) + lam_init)
    q = q_ref[...] * (DA_DIM ** -0.5)
    lane = lax.broadcasted_iota(jnp.int32, q.shape, 1)
    zero = jnp.zeros_like(q)
    qm = (jnp.where(lane < DA_DIM, q, zero), jnp.where(lane >= DA_DIM, q, zero))
    k = k_ref[...]
    s = [lax.dot_general(qm[m], k, NT, preferred_element_type=F32) for m in range(2)]
    mx = [jnp.max(s[m], axis=-1, keepdims=True) for m in range(2)]
    if has_cache:
        kc = kc_ref[...].astype(BF16)
        sc = [lax.dot_general(qm[m], kc, NT, preferred_element_type=F32) for m in range(2)]
        mx = [jnp.maximum(mx[m], jnp.max(sc[m], axis=-1, keepdims=True)) for m in range(2)]
    e = [jnp.exp(s[m] - mx[m]) for m in range(2)]
    z = [jnp.sum(e[m], axis=-1, keepdims=True) for m in range(2)]
    if has_cache:
        ec = [jnp.exp(sc[m] - mx[m]) for m in range(2)]
        z = [z[m] + jnp.sum(ec[m], axis=-1, keepdims=True) for m in range(2)]
    w0 = 1.0 / z[0]
    w1 = lam / z[1]
    o = jnp.dot((e[0] * w0 - e[1] * w1).astype(BF16), v_ref[...], preferred_element_type=F32)
    if has_cache:
        o = o + jnp.dot((ec[0] * w0 - ec[1] * w1).astype(BF16), vc_ref[...].astype(BF16),
                        preferred_element_type=F32)
    ms = jnp.mean(o * o, axis=-1, keepdims=True)
    o_ref[...] = (o * lax.rsqrt(ms + EPS) * sg_ref[...] * (1.0 - lam_init)).astype(o_ref.dtype)


def _diff_attention(q, k, v, da_lam, subln_g, lam_init, row0, batch, seq, cache=None):
    tq = 256
    qb0, kb0 = row0 // tq, row0 // seq
    in_specs = [pl.BlockSpec((4, DA_DIM), lambda b, h, i: (0, 0)),
                pl.BlockSpec((1, 128), lambda b, h, i: (0, 0)),
                pl.BlockSpec((tq, 128), lambda b, h, i: (qb0 + b * (seq // tq) + i, h)),
                pl.BlockSpec((seq, 128), lambda b, h, i: (kb0 + b, h)),
                pl.BlockSpec((seq, 128), lambda b, h, i: (kb0 + b, h))]
    args = [da_lam, subln_g, q, k, v]
    if cache is not None:
        kc, vc, j = cache
        past = kc.shape[2]
        in_specs += [pl.BlockSpec((None, None, past, 128), lambda b, h, i: (b, j, 0, h))] * 2
        args += [kc, vc]
    return pl.pallas_call(
        functools.partial(_attn_body, lam_init=lam_init, has_cache=cache is not None),
        grid=(batch, DA_HEADS, seq // tq),
        in_specs=in_specs,
        out_specs=pl.BlockSpec((tq, 128), lambda b, h, i: (b * (seq // tq) + i, h)),
        out_shape=jax.ShapeDtypeStruct((batch * seq, 512), BF16),
        compiler_params=_cp("arbitrary", "arbitrary", "arbitrary"),
        name="diff_attn_ctx" if cache is not None else "diff_attn",
    )(*args)


def _gla_body(qk_ref, v_ref, r_ref, low_ref, gup_ref, gb_ref, ng_ref, *rest, seq, has_init):
    if has_init:
        s0_ref, o_ref, sf_ref, st_scr, of_scr, ob_scr, la_scr = rest
    else:
        o_ref, sf_ref, st_scr, of_scr, ob_scr, la_scr = rest
    c = GLA_CHUNK
    n = seq // c
    hk = GLA_HEADS * GLA_DK
    low = low_ref[...]
    for d in range(2):
        z = jnp.dot(low, gup_ref[d], precision=HI, preferred_element_type=F32) + gb_ref[d]
        la_scr[d] = (jnp.minimum(z, 0.0) - jnp.log(1.0 + jnp.exp(-jnp.abs(z)))) * (1.0 / GLA_TAU)
    if has_init:
        st_scr[...] = s0_ref[0]
    else:
        st_scr[...] = jnp.zeros(st_scr.shape, F32)
    row = lax.broadcasted_iota(jnp.int32, (2 * c, 2 * c), 0)
    col = lax.broadcasted_iota(jnp.int32, (2 * c, 2 * c), 1)
    fwd_blk = jnp.logical_and(row < c, col <= row)
    bwd_blk = jnp.logical_and(jnp.logical_and(row >= c, col >= c), col >= row)
    keep = jnp.logical_or(fwd_blk, bwd_blk)
    keep_f = keep.astype(F32)
    lane = lax.broadcasted_iota(jnp.int32, (2 * c, 128), 1)
    low_half = lane < GLA_DK
    top_rows = lax.broadcasted_iota(jnp.int32, (2 * c, hk), 0) < c

    def step(i, carry):
        rf = pl.ds(pl.multiple_of(i * c, c), c)
        rb = pl.ds(pl.multiple_of((n - 1 - i) * c, c), c)
        qk = jnp.concatenate([qk_ref[rf, :], qk_ref[rb, :]], axis=0).astype(F32)
        q = qk[:, 0:hk] * (GLA_DK ** -0.5)
        k = qk[:, hk:2 * hk]
        v = jnp.concatenate([v_ref[rf, :], v_ref[rb, :]], axis=0)
        g = jnp.concatenate([la_scr[0, rf, :], la_scr[1, rb, :]], axis=0)
        b = jnp.dot(keep_f, g, precision=HI, preferred_element_type=F32)
        b_tot = (b[c - 1:c], b[c:c + 1])
        q_dec = q * jnp.exp(b)
        k_inv = k * jnp.exp(-b)
        k_end = k * jnp.exp(jnp.where(top_rows, b_tot[0], b_tot[1]) - b)
        st = (st_scr[0], st_scr[1])
        st_b = (st[0].astype(BF16), st[1].astype(BF16))
        outs, news = [], ([], [])
        for pair in range(GLA_HEADS // 2):
            ps = slice(pair * 128, (pair + 1) * 128)
            kin = k_inv[:, ps].astype(BF16)
            new = [None, None]
            for sub in range(2):
                hh = pair * 2 + sub
                vs = slice(hh * GLA_DV, (hh + 1) * GLA_DV)
                sel = low_half if sub == 0 else jnp.logical_not(low_half)
                qd = jnp.where(sel, q_dec[:, ps], 0.0).astype(BF16)
                ke = jnp.where(sel, k_end[:, ps], 0.0).astype(BF16)
                att = lax.dot_general(qd, kin, NT, preferred_element_type=F32)
                att = jnp.where(keep, att, 0.0).astype(BF16)
                intra = jnp.dot(att, v[:, vs], preferred_element_type=F32)
                inter = jnp.concatenate(
                    [lax.dot_general(qd[d * c:(d + 1) * c], st_b[d][:, ps], NT, preferred_element_type=F32)
                     for d in range(2)], axis=0)
                outs.append(inter + intra)
                for d in range(2):
                    upd = lax.dot_general(v[d * c:(d + 1) * c, vs], ke[d * c:(d + 1) * c], TN,
                                          preferred_element_type=F32)
                    new[d] = upd if new[d] is None else new[d] + upd
            for d in range(2):
                news[d].append(new[d])
        for d in range(2):
            st_scr[d] = st[d] * jnp.exp(b_tot[d]) + jnp.concatenate(news[d], axis=1)
        o = jnp.concatenate(outs, axis=1)
        of_scr[rf, :] = o[0:c]
        ob_scr[rb, :] = o[c:2 * c]
        return carry

    lax.fori_loop(0, n, step, 0)
    sf_ref[0] = st_scr[...]

    fin = 256

    def finish(i, carry):
        rows = pl.ds(pl.multiple_of(i * fin, fin), fin)
        o = of_scr[rows, :] + ob_scr[rows, :]
        r = r_ref[rows, :].astype(F32)
        gate = r / (1.0 + jnp.exp(-r))
        ng = ng_ref[...]
        for hh in range(GLA_HEADS):
            vs = slice(hh * GLA_DV, (hh + 1) * GLA_DV)
            oh = o[:, vs]
            ms = jnp.mean(oh * oh, axis=-1, keepdims=True)
            o_ref[rows, vs] = (oh * lax.rsqrt(ms + EPS) * ng * gate[:, vs]).astype(BF16)
        return carry

    lax.fori_loop(0, seq // fin, finish, 0)


def _gla(gqk, gv, gr, low, gup, gb, ng, row0, batch, seq, s0=None):
    b0 = row0 // seq
    tokmap = lambda b: (b0 + b, 0)
    hk = GLA_HEADS * GLA_DK
    in_specs = [pl.BlockSpec((seq, 512), tokmap), pl.BlockSpec((seq, 512), tokmap),
                pl.BlockSpec((seq, 512), tokmap), pl.BlockSpec((seq, 128), tokmap),
                pl.BlockSpec((2, 128, hk), lambda b: (0, 0, 0)),
                pl.BlockSpec((2, 1, hk), lambda b: (0, 0, 0)),
                pl.BlockSpec((1, GLA_DV), lambda b: (0, 0))]
    args = [gqk, gv, gr, low, gup, gb, ng]
    if s0 is not None:
        in_specs.append(pl.BlockSpec((1, 2, GLA_DV, hk), lambda b: (b, 0, 0, 0)))
        args.append(s0)
    return pl.pallas_call(
        functools.partial(_gla_body, seq=seq, has_init=s0 is not None),
        grid=(batch,),
        in_specs=in_specs,
        out_specs=[pl.BlockSpec((seq, 512), lambda b: (b, 0)),
                   pl.BlockSpec((1, 2, GLA_DV, hk), lambda b: (b, 0, 0, 0))],
        out_shape=[jax.ShapeDtypeStruct((batch * seq, 512), BF16),
                   jax.ShapeDtypeStruct((batch, 2, GLA_DV, hk), F32)],
        scratch_shapes=[pltpu.VMEM((2, GLA_DV, hk), F32),
                        pltpu.VMEM((seq, 512), F32), pltpu.VMEM((seq, 512), F32),
                        pltpu.VMEM((2, seq, hk), F32)],
        compiler_params=_cp("arbitrary"),
        name="gla_ctx" if s0 is not None else "gla",
    )(*args)


def _outproj0_body(xp_ref, xs_ref, mod_ref, odap_ref, odas_ref, ogp_ref, ogs_ref, w_ref, o_ref, *, np_blocks):
    ctx = pl.program_id(0) < np_blocks
    x = jnp.where(ctx, xp_ref[...], xs_ref[...])
    oda = jnp.where(ctx, odap_ref[...], odas_ref[...])
    og = jnp.where(ctx, ogp_ref[...], ogs_ref[...])
    m = (jnp.dot(oda, w_ref[0:512, :], preferred_element_type=F32)
         + jnp.dot(og, w_ref[512:1024, :], preferred_element_type=F32))
    o_ref[...] = x + mod_ref[0][:, 2 * D:3 * D] * m


def _split_specs(block, np_blocks, axis):
    def at(k):
        return tuple(k if a == axis else 0 for a in range(len(block)))
    return (pl.BlockSpec(block, lambda i: at(jnp.minimum(i, np_blocks - 1))),
            pl.BlockSpec(block, lambda i: at(jnp.maximum(i - np_blocks, 0))))


def _outproj0(x_p, x_s, mod, oda_p, oda_s, og_p, og_s, w, np_rows, seq_rows):
    n = x_p.shape[0] + x_s.shape[0]
    tm = ROW_TILE
    npb = np_rows // tm
    rowmap = _mod_row_map(npb, seq_rows // tm)
    sp, ss = _split_specs((tm, 512), npb, 0)
    return pl.pallas_call(
        functools.partial(_outproj0_body, np_blocks=npb),
        grid=(n // tm,),
        in_specs=[*_split_specs((tm, D), npb, 0),
                  pl.BlockSpec((1, 1, 6 * D), lambda i: (rowmap(i), 0, 0)),
                  sp, ss, sp, ss,
                  pl.BlockSpec((D, D), lambda i: (0, 0))],
        out_specs=pl.BlockSpec((tm, D), lambda i: (i, 0)),
        out_shape=jax.ShapeDtypeStruct((n, D), F32),
        compiler_params=_cp("arbitrary"),
        name="outproj0",
    )(x_p, x_s, mod, oda_p, oda_s, og_p, og_s, w)


def _route_body(x_ref, mod_ref, g_ref, wq_ref, sk_ref, ht_o, nsel_o, e1_o, rank_o, e2_o, top_scr, s_scr):
    mod = mod_ref[0]
    h = _norm_mod(x_ref[...], g_ref[...], mod[:, 4 * D:5 * D], mod[:, 3 * D:4 * D])
    ht_o[...] = h.T.astype(BF16)
    q = jnp.dot(h.astype(BF16), wq_ref[...], preferred_element_type=F32).astype(BF16)
    tm = q.shape[0]
    neg = -jnp.inf
    k = PEER_TOPK
    row8 = lax.broadcasted_iota(jnp.int32, (8, tm), 0)
    tiles = [slice(c * 128, (c + 1) * 128) for c in range(tm // 128)]
    for hh in range(PEER_HEADS):
        hs = slice(hh * PEER_NKEYS, (hh + 1) * PEER_NKEYS)
        for t in range(2):
            c0 = (hh * 2 + t) * PEER_NKEYS
            s_scr[t] = lax.dot_general(sk_ref[hh, t], q[:, c0:c0 + PEER_NKEYS], NT,
                                       preferred_element_type=F32)
            for cs in tiles:
                cur = s_scr[t, :, cs]
                rank = jnp.full((PEER_NKEYS, 128), float(k), F32)
                for j in range(k):
                    m = jnp.max(cur, axis=0, keepdims=True)
                    top_scr[t, j:j + 1, cs] = m
                    hit = cur == m
                    if t == 1:
                        rank = jnp.where(hit, float(j), rank)
                    cur = jnp.where(hit, neg, cur)
                if t == 1:
                    rank_o[hs, cs] = rank.astype(BF16)
        t1 = top_scr[0]
        t2 = top_scr[1]
        slabs = [t1[0:1] + t2, t1[1:2] + t2[0:8], t1[2:3] + t2[0:8], t1[3:4] + t2[0:8],
                 t2[0:1] + t1[8:16]]
        for j in range(3):
            slabs.append(jnp.where(row8 >= 4, t2[j:j + 1] + t1[0:8], neg))
        cand = jnp.concatenate(slabs, axis=0)
        top = t1[0:1] + t2[0:1]
        zsum = jnp.zeros_like(top)
        kth = top
        for j in range(k):
            kth = jnp.max(cand, axis=0, keepdims=True)
            zsum = zsum + jnp.exp(kth - top)
            cand = jnp.where(cand == kth, neg, cand)
        zinv = 1.0 / zsum
        for c, cs in enumerate(tiles):
            s1 = s_scr[0, :, cs]
            nsel = jnp.zeros((PEER_NKEYS, 128), F32)
            for j in range(k):
                nsel = jnp.where(s1 + t2[j:j + 1, cs] >= kth[:, cs], float(j + 1), nsel)
            nsel_o[c, hs, :] = nsel
            e1_o[c, hs, :] = jnp.exp(s1 - t1[0:1, cs]) * zinv[:, cs]
            e2_o[hs, cs] = jnp.exp(s_scr[1, :, cs] - t2[0:1, cs]).astype(BF16)


def _route(x, mod, norm_g, wq, sk, layer, np_rows, seq_rows):
    n = x.shape[0]
    tm = ROW_TILE
    rowmap = _mod_row_map(np_rows // tm, seq_rows // tm)
    rt = lambda dt: jax.ShapeDtypeStruct((PEER_HEADS * PEER_NKEYS, n), dt)
    rspec = pl.BlockSpec((PEER_HEADS * PEER_NKEYS, tm), lambda i: (0, i))
    st_ = jax.ShapeDtypeStruct((n // 128, PEER_HEADS * PEER_NKEYS, 128), F32)
    sspec = pl.BlockSpec((tm // 128, PEER_HEADS * PEER_NKEYS, 128), lambda i: (i, 0, 0))
    return pl.pallas_call(
        _route_body,
        grid=(n // tm,),
        in_specs=[pl.BlockSpec((tm, D), lambda i: (i, 0)),
                  pl.BlockSpec((1, 1, 6 * D), lambda i: (rowmap(i), 0, 0)),
                  pl.BlockSpec((1, D), lambda i: (0, 0)),
                  pl.BlockSpec((None, D, 2 * PEER_HEADS * PEER_NKEYS), lambda i: (layer, 0, 0)),
                  pl.BlockSpec((None, PEER_HEADS, 2, PEER_NKEYS, 128), lambda i: (layer, 0, 0, 0, 0))],
        out_specs=[pl.BlockSpec((D, tm), lambda i: (0, i)), sspec, sspec, rspec, rspec],
        out_shape=[jax.ShapeDtypeStruct((D, n), BF16), st_, st_, rt(BF16), rt(BF16)],
        scratch_shapes=[pltpu.VMEM((2, PEER_TOPK, tm), F32), pltpu.VMEM((2, PEER_NKEYS, tm), F32)],
        compiler_params=_cp("arbitrary"),
        name="peer_route",
    )(x, mod, norm_g, wq, sk)


def _peer_body(ht_ref, u_ref, vt_ref, nsel_ref, e1_ref, rank_ref, e2_ref, x_ref, mod_ref, fg_ref,
               o_ref, acc_ref, p_ref, *, final_norm):
    j = pl.program_id(1)

    @pl.when(j == 0)
    def _():
        acc_ref[...] = jnp.zeros(acc_ref.shape, F32)

    nk = PEER_NKEYS
    tok = ht_ref.shape[1]
    na = PEER_EXP // nk
    a0 = pl.multiple_of(j * na, na)
    half = PEER_EXP // 2
    rg = 8
    zero = jnp.zeros((rg, PEER_SUB), BF16)

    def row_bcast(ref, hh, r, ts):
        parts = [jnp.broadcast_to(ref[c, pl.ds(hh * nk + a0, na), :][r:r + 1], (rg, 128))
                 for c in range(ts.start // 128, ts.stop // 128)]
        return jnp.concatenate(parts, axis=1).astype(BF16)

    for t0 in range(0, tok, PEER_SUB):
        ts = slice(t0, t0 + PEER_SUB)
        act = [jnp.dot(u_ref[hf * half:(hf + 1) * half, :], ht_ref[:, ts], preferred_element_type=F32)
               for hf in range(2)]
        for r in range(na):
            gate = [zero] * (nk // rg)
            for hh in range(PEER_HEADS):
                ns = row_bcast(nsel_ref, hh, r, ts)
                e1 = row_bcast(e1_ref, hh, r, ts)
                for g in range(nk // rg):
                    bs = slice(hh * nk + g * rg, hh * nk + (g + 1) * rg)
                    gate[g] = gate[g] + jnp.where(rank_ref[bs, ts] < ns, e2_ref[bs, ts], zero) * e1
            for g in range(nk // rg):
                lo = r * nk + g * rg
                a = act[lo // half][lo % half:lo % half + rg, :]
                p_ref[lo:lo + rg, ts] = _gelu(a).astype(BF16) * gate[g]
        for hf in range(2):
            ds = slice(hf * (D // 2), (hf + 1) * (D // 2))
            acc_ref[ds, ts] += jnp.dot(vt_ref[ds, :], p_ref[:, ts], preferred_element_type=F32)

    @pl.when(j == pl.num_programs(1) - 1)
    def _():
        y = x_ref[...] + mod_ref[0][:, 5 * D:6 * D] * acc_ref[...].T
        if final_norm:
            ms = jnp.mean(y * y, axis=-1, keepdims=True)
            y = y * lax.rsqrt(ms + EPS) * fg_ref[...]
        o_ref[...] = y


def _peer_dense(x, mod, ht, u, vt, layer, nsel, e1, rank, e2, fg, np_rows, seq_rows, final_norm,
                row0=0, rows=None):
    rows = x.shape[0] if rows is None else rows
    tk, ex = PEER_TOK, PEER_EXP
    b0 = row0 // tk
    rowmap = _mod_row_map(np_rows // tk, seq_rows // tk)
    rspec = pl.BlockSpec((PEER_HEADS * PEER_NKEYS, tk), lambda i, j: (0, b0 + i))
    sspec = pl.BlockSpec((tk // 128, PEER_HEADS * PEER_NKEYS, 128), lambda i, j: (b0 + i, 0, 0))
    return pl.pallas_call(
        functools.partial(_peer_body, final_norm=final_norm),
        grid=(rows // tk, u.shape[1] // ex),
        in_specs=[pl.BlockSpec((D, tk), lambda i, j: (0, b0 + i)),
                  pl.BlockSpec((None, ex, D), lambda i, j: (layer, j, 0)),
                  pl.BlockSpec((None, D, ex), lambda i, j: (layer, 0, j)),
                  sspec, sspec, rspec, rspec,
                  pl.BlockSpec((tk, D), lambda i, j: (b0 + i, 0)),
                  pl.BlockSpec((1, 1, 6 * D), lambda i, j: (rowmap(b0 + i), 0, 0)),
                  pl.BlockSpec((1, D), lambda i, j: (0, 0))],
        out_specs=pl.BlockSpec((tk, D), lambda i, j: (i, 0)),
        out_shape=jax.ShapeDtypeStruct((rows, D), F32),
        scratch_shapes=[pltpu.VMEM((D, tk), F32), pltpu.VMEM((ex, tk), BF16)],
        compiler_params=_cp("arbitrary", "arbitrary"),
        name="peer_dense",
    )(ht, u, vt, nsel, e1, rank, e2, x, mod, fg)


def _peer_tables(w_query, sub_keys, u, v):
    return w_query.astype(BF16), sub_keys.astype(BF16), u.astype(BF16), v.transpose(0, 2, 1).astype(BF16)


def _peer(x, mod, norm_g, tables, layer, fg, np_rows, seq_rows, final_norm, split=False):
    wq, sk, u, vt = tables
    ht, nsel, e1, rank, e2 = _route(x, mod, norm_g, wq, sk, layer, np_rows, seq_rows)
    dense = functools.partial(_peer_dense, x, mod, ht, u, vt, layer, nsel, e1, rank, e2, fg,
                              np_rows, seq_rows, final_norm)
    if split:
        return dense(row0=0, rows=np_rows), dense(row0=np_rows, rows=x.shape[0] - np_rows)
    return dense()


def _window_of(lane, i):
    return (lane >> 4) == i


def _s5_in_body(x_ref, mod_ref, g_ref, w_ref, o_ref, u_scr):
    mod = mod_ref[0]
    h = _norm_mod(x_ref[...], g_ref[...], mod[:, D:2 * D], mod[:, 0:D]).astype(BF16)
    u = jnp.dot(h, w_ref[...], preferred_element_type=F32)
    cs, per = S5_CHUNK, 128 // S5_CH
    nck = u.shape[0] // cs
    for j in range(D // 128):
        u_scr[j] = u[:, j * 128:(j + 1) * 128]
    lane = lax.broadcasted_iota(jnp.int32, (nck, 128), 1)
    for j in range(D // 128):
        xs = [u_scr[j, pl.ds(s, nck, stride=cs), :] for s in range(cs)]
        for gp in range(per):
            for half in range(cs // per):
                acc = None
                for s8 in range(per):
                    piece = xs[half * per + s8]
                    shift = ((s8 - gp) * S5_CH) % 128
                    if shift:
                        piece = pltpu.roll(piece, shift, 1)
                    acc = piece if acc is None else jnp.where(_window_of(lane, s8), piece, acc)
                o_ref[j * per + gp, :, half * 128:(half + 1) * 128] = acc.astype(BF16)


def _s5_in(x, mod, norm_g, w, np_rows, seq_rows):
    n = x.shape[0]
    tm = ROW_TILE
    rowmap = _mod_row_map(np_rows // tm, seq_rows // tm)
    nck = tm // S5_CHUNK
    return pl.pallas_call(
        _s5_in_body,
        grid=(n // tm,),
        in_specs=[pl.BlockSpec((tm, D), lambda i: (i, 0)),
                  pl.BlockSpec((1, 1, 6 * D), lambda i: (rowmap(i), 0, 0)),
                  pl.BlockSpec((1, D), lambda i: (0, 0)),
                  pl.BlockSpec((D, D), lambda i: (0, 0))],
        out_specs=pl.BlockSpec((S5_GROUPS, nck, S5_CHUNK * S5_CH), lambda i: (0, i, 0)),
        out_shape=jax.ShapeDtypeStruct((S5_GROUPS, n // S5_CHUNK, S5_CHUNK * S5_CH), BF16),
        scratch_shapes=[pltpu.VMEM((D // 128, tm, 128), F32)],
        compiler_params=_cp("arbitrary"),
        name="s5_in",
    )(x, mod, norm_g, w)


def _s5_core_body(u_ref, m_ref, win_ref, cout_ref, pw_ref, *rest, kseq, has_init, emit_states):
    rest = list(rest)
    h0_ref = rest.pop(0) if has_init else None
    y_ref = rest.pop(0)
    sf_ref = rest.pop(0) if emit_states else None
    kb = u_ref.shape[1]
    row = lax.broadcasted_iota(jnp.int32, (kb, 128), 0)
    kk = row & (kseq - 1)

    def cmul(a1, a2, x):
        return a1 * x + a2 * pltpu.roll(x, 64, 1)

    for gi in range(u_ref.shape[0]):
        u = u_ref[gi]
        y = jnp.dot(u, m_ref[gi], preferred_element_type=F32)
        for d in range(2):
            s = jnp.dot(u, win_ref[gi, d], preferred_element_type=F32)
            edge = (kk == 0) if d == 0 else (kk == kseq - 1)
            if has_init:
                h0 = h0_ref[gi, d]
                s = s + jnp.where(edge, cmul(pw_ref[gi, d, 0, 0:1], pw_ref[gi, d, 0, 1:2], h0), 0.0)
            sw = pltpu.roll(s, 64, 1)
            step, lvl = 1, 0
            while step < kseq:
                shift = step if d == 0 else kb - step
                ok = (kk >= step) if d == 0 else (kk < kseq - step)
                a1, a2 = pw_ref[gi, d, lvl, 0:1], pw_ref[gi, d, lvl, 1:2]
                sh, swh = pltpu.roll(s, shift, 0), pltpu.roll(sw, shift, 0)
                s = s + jnp.where(ok, a1 * sh + a2 * swh, 0.0)
                sw = sw + jnp.where(ok, a1 * swh - a2 * sh, 0.0)
                step, lvl = step * 2, lvl + 1
            if emit_states:
                sf_ref[gi, d] = s
            hin = pltpu.roll(s, 1, 0) if d == 0 else pltpu.roll(s, kb - 1, 0)
            if has_init:
                hin = jnp.where(edge, h0, hin)
            else:
                hin = jnp.where(edge, 0.0, hin)
            y = y + jnp.dot(hin.astype(BF16), cout_ref[gi, d], preferred_element_type=F32)
        y_ref[gi] = _gelu(y).astype(BF16)


def _s5_core(ug, mats, row0, rows, kseq, h0=None, emit_states=False):
    m, win, cout, pw = mats
    kb = 128
    b0 = row0 // kb
    nlv = pw.shape[2]
    gb = S5_GROUP_BLOCK
    in_specs = [pl.BlockSpec((gb, kb, 256), lambda g, i: (g, b0 + i, 0)),
                pl.BlockSpec((gb, 256, 256), lambda g, i: (g, 0, 0)),
                pl.BlockSpec((gb, 2, 256, 128), lambda g, i: (g, 0, 0, 0)),
                pl.BlockSpec((gb, 2, 128, 256), lambda g, i: (g, 0, 0, 0)),
                pl.BlockSpec((gb, 2, nlv, 2, 128), lambda g, i: (g, 0, 0, 0, 0))]
    args = [ug, m, win, cout, pw]
    if h0 is not None:
        assert kseq == kb
        in_specs.append(pl.BlockSpec((gb, None, 2, 1, 128), lambda g, i: (g, i, 0, 0, 0)))
        args.append(h0)
    out_specs = [pl.BlockSpec((gb, kb, 256), lambda g, i: (g, i, 0))]
    out_shape = [jax.ShapeDtypeStruct((S5_GROUPS, rows, 256), BF16)]
    if emit_states:
        out_specs.append(pl.BlockSpec((gb, 2, kb, 128), lambda g, i: (g, 0, i, 0)))
        out_shape.append(jax.ShapeDtypeStruct((S5_GROUPS, 2, rows, 128), F32))
    return pl.pallas_call(
        functools.partial(_s5_core_body, kseq=kseq, has_init=h0 is not None, emit_states=emit_states),
        grid=(S5_GROUPS // gb, rows // kb),
        in_specs=in_specs,
        out_specs=out_specs,
        out_shape=out_shape,
        compiler_params=_cp("arbitrary", "arbitrary"),
        name="s5_core_ctx" if h0 is not None else "s5_core",
    )(*args)


def _s5_out_body(x_ref, mod_ref, yp_ref, ys_ref, w_ref, o_ref, y_scr, *, np_blocks):
    cs, per = S5_CHUNK, 128 // S5_CH
    nck = yp_ref.shape[1]
    lane = lax.broadcasted_iota(jnp.int32, (nck, 128), 1)
    ctx = pl.program_id(0) < np_blocks
    for j in range(D // 128):
        gs = slice(j * per, (j + 1) * per)
        yj = jnp.where(ctx, yp_ref[gs], ys_ref[gs])
        src = [[yj[gp, :, half * 128:(half + 1) * 128].astype(F32) for half in range(cs // per)]
               for gp in range(per)]
        for t in range(cs):
            half, t8 = divmod(t, per)
            acc = None
            for gp in range(per):
                piece = src[gp][half]
                shift = ((gp - t8) * S5_CH) % 128
                if shift:
                    piece = pltpu.roll(piece, shift, 1)
                acc = piece if acc is None else jnp.where(_window_of(lane, gp), piece, acc)
            y_scr[j, pl.ds(t, nck, stride=cs), :] = acc
    y = jnp.concatenate([y_scr[j] for j in range(D // 128)], axis=1).astype(BF16)
    zz = jnp.dot(y, w_ref[...], preferred_element_type=F32)
    za = zz[:, 0:D]
    zb = zz[:, D:2 * D]
    o_ref[...] = x_ref[...] + mod_ref[0][:, 2 * D:3 * D] * (za / (1.0 + jnp.exp(-zb)))


def _s5_out(x, mod, y_p, y_s, w, np_rows, seq_rows):
    n = x.shape[0]
    tm = ROW_TILE
    npb = np_rows // tm
    rowmap = _mod_row_map(npb, seq_rows // tm)
    sp, ss = _split_specs((S5_GROUPS, tm // S5_CHUNK, S5_CHUNK * S5_CH), npb, 1)
    return pl.pallas_call(
        functools.partial(_s5_out_body, np_blocks=npb),
        grid=(n // tm,),
        in_specs=[pl.BlockSpec((tm, D), lambda i: (i, 0)),
                  pl.BlockSpec((1, 1, 6 * D), lambda i: (rowmap(i), 0, 0)),
                  sp, ss,
                  pl.BlockSpec((D, 2 * D), lambda i: (0, 0))],
        out_specs=pl.BlockSpec((tm, D), lambda i: (i, 0)),
        out_shape=jax.ShapeDtypeStruct((n, D), F32),
        scratch_shapes=[pltpu.VMEM((D // 128, tm, 128), F32)],
        compiler_params=_cp("arbitrary"),
        name="s5_out",
    )(x, mod, y_p, y_s, w)


def _s5_matrices(a_re, a_im, log_dt, b_re, b_im, c_re, c_im, d_skip):
    cs = S5_CHUNK
    dt = jnp.exp(log_dt)[..., None]
    lr, li = a_re * dt, a_im * dt

    def lam_pow(tau):
        mag = jnp.exp(lr[:, :, None, :] * tau[:, None])
        ang = li[:, :, None, :] * tau[:, None]
        return mag * jnp.cos(ang), mag * jnp.sin(ang)

    l1r, l1i = jnp.exp(lr) * jnp.cos(li), jnp.exp(lr) * jnp.sin(li)
    den = a_re * a_re + a_im * a_im
    cr = ((l1r - 1.0) * a_re + l1i * a_im) / den
    ci = (l1i * a_re - (l1r - 1.0) * a_im) / den
    bt_re = b_re.transpose(0, 1, 3, 2)
    bt_im = b_im.transpose(0, 1, 3, 2)
    bbr = cr[:, :, None, :] * bt_re - ci[:, :, None, :] * bt_im
    bbi = cr[:, :, None, :] * bt_im + ci[:, :, None, :] * bt_re

    pr, pi = lam_pow(jnp.arange(cs + 1, dtype=F32))
    clr = c_re[:, :, None] * pr[:, :, :, None, :] - c_im[:, :, None] * pi[:, :, :, None, :]
    cli = c_re[:, :, None] * pi[:, :, :, None, :] + c_im[:, :, None] * pr[:, :, :, None, :]
    cl = jnp.concatenate([clr[:, :, :cs], -cli[:, :, :cs]], axis=-1).reshape(2, S5_GROUPS, cs * S5_CH, 2 * S5_P)
    bb = jnp.concatenate([bbr, bbi], axis=-1)
    kern = jnp.einsum('dgeq,dgjq->dgej', bb, cl, precision=HI)
    w = cs * S5_CH
    kern_b = kern[1].reshape(S5_GROUPS, S5_CH, cs, S5_CH)[:, :, ::-1].reshape(S5_GROUPS, S5_CH, w)
    zpad = lambda a, lo, hi: jnp.pad(a, ((0, 0), (0, 0), (lo, hi)))
    mf = jnp.stack([zpad(kern[0][:, :, :w - S5_CH * s], S5_CH * s, 0) for s in range(cs)], axis=1)
    mb = jnp.stack([zpad(kern_b[:, :, S5_CH * (cs - 1 - s):], 0, S5_CH * (cs - 1 - s)) for s in range(cs)], axis=1)
    m = (mf + mb).reshape(S5_GROUPS, w, w)
    m = m + jnp.eye(cs * S5_CH, dtype=F32) * jnp.tile(d_skip.reshape(S5_GROUPS, 1, S5_CH), (1, 1, cs))

    def win_dir(d, powers):
        qr_, qi_ = pr[d][:, powers][:, :, None, :], pi[d][:, powers][:, :, None, :]
        wr = qr_ * bbr[d][:, None] - qi_ * bbi[d][:, None]
        wi = qr_ * bbi[d][:, None] + qi_ * bbr[d][:, None]
        return jnp.concatenate([wr, wi], axis=-1).reshape(S5_GROUPS, cs * S5_CH, 2 * S5_P)

    win = jnp.stack([win_dir(0, jnp.arange(cs - 1, -1, -1)), win_dir(1, jnp.arange(cs))], axis=1)

    def cout_dir(d, powers):
        z = jnp.concatenate([clr[d][:, powers], -cli[d][:, powers]], axis=-1)
        return z.reshape(S5_GROUPS, cs * S5_CH, 2 * S5_P).transpose(0, 2, 1)

    cout = jnp.stack([cout_dir(0, jnp.arange(1, cs + 1)), cout_dir(1, jnp.arange(cs, 0, -1))], axis=1)

    qr, qi = lam_pow(cs * (2.0 ** jnp.arange(7, dtype=F32)))
    a1 = jnp.concatenate([qr, qr], axis=-1)
    a2 = jnp.concatenate([-qi, qi], axis=-1)
    pw = jnp.stack([a1, a2], axis=3).transpose(1, 0, 2, 3, 4)
    return m.astype(BF16), win.astype(BF16), cout.astype(BF16), pw


def _rope_tables(dec_seq):
    rows = dec_seq // GRID_W
    row_id = jnp.repeat(jnp.arange(rows), GRID_W).astype(F32)
    col_id = jnp.tile(jnp.arange(GRID_W), rows).astype(F32)
    quarter = DA_DIM // 4
    inv = ROPE_BASE ** (-jnp.arange(quarter, dtype=F32) / quarter)
    ang_r = row_id[:, None] * inv
    ang_c = col_id[:, None] * inv
    ang = jnp.concatenate([ang_r, ang_r, ang_c, ang_c], axis=-1)
    ang = jnp.tile(ang, (1, 2 * DA_HEADS))
    pad = jnp.zeros((ROW_TILE, ang.shape[1]), F32)
    ang = jnp.concatenate([pad, ang], axis=0)
    return jnp.cos(ang), jnp.sin(ang)


def kernel(x_prompt, x_sample, cache_da_k, cache_da_v, state_gla, state_s5, c, c_ctx,
           ada_w, ada_b, norm_g, final_norm_g, mix0_w_in, mix0_w_out, da_lam, da_subln_g,
           gla_gate_up, gla_gate_bias, gla_norm_g, s5_w_in, s5_a_re, s5_a_im, s5_log_dt,
           s5_b_re, s5_b_im, s5_c_re, s5_c_im, s5_d, s5_w_out,
           peer_w_query, peer_sub_keys, peer_u, peer_v):
    bp, lp, _ = x_prompt.shape
    bs, ls, _ = x_sample.shape
    npr, nsr = bp * lp, bs * ls
    assert npr % ls == 0 and ls % ROW_TILE == 0 and lp % ROW_TILE == 0
    x_p, x_s = x_prompt.reshape(npr, D), x_sample.reshape(nsr, D)

    cond = jnp.concatenate([c_ctx[None], c, jnp.zeros((15 - bs, D), F32)], axis=0)
    mod_all = _ada_mod(cond, ada_w, ada_b)
    mods = [mod_all[l].reshape(16, 1, 6 * D) for l in range(mod_all.shape[0])]
    fg = final_norm_g.reshape(1, D)

    lam_init = 0.8 - 0.6 * math.exp(-0.3 * 0)
    cos_t, sin_t = _rope_tables(ls)
    w_in = jnp.pad(mix0_w_in[0], ((0, 0), (0, IN0_PAD - mix0_w_in.shape[2]))).astype(BF16)
    q, kb, vb, k32, v32, gqk, gv, gr, low = _inproj0(
        x_p, x_s, mods[0], norm_g[0, 0].reshape(1, D), w_in, cos_t, sin_t, npr, ls)
    sub_g = da_subln_g[0].reshape(1, 2 * DA_DIM)
    past = cache_da_k.shape[2]
    oda_p = _diff_attention(q, kb, vb, da_lam[0], sub_g, lam_init, 0, bp, lp)
    oda_s = _diff_attention(q, kb, vb, da_lam[0], sub_g, lam_init, npr, bs, ls,
                            cache=(cache_da_k.reshape(bs, -1, past, 512), cache_da_v.reshape(bs, -1, past, 512), 0))
    hk = GLA_HEADS * GLA_DK
    gup = jnp.zeros((2, 128, hk), F32)
    for d in range(2):
        gup = gup.at[d, d * GLA_RANK:(d + 1) * GLA_RANK].set(gla_gate_up[0, d])
    gbias = gla_gate_bias[0].reshape(2, 1, hk)
    ng = gla_norm_g[0].reshape(1, GLA_DV)
    s0 = state_gla[:, 0].transpose(0, 1, 4, 2, 3).reshape(bs, 2, GLA_DV, hk)
    og_p, st_p = _gla(gqk, gv, gr, low, gup, gbias, ng, 0, bp, lp)
    og_s, _ = _gla(gqk, gv, gr, low, gup, gbias, ng, npr, bs, ls, s0=s0)
    x = _outproj0(x_p, x_s, mods[0], oda_p, oda_s, og_p, og_s, mix0_w_out[0].astype(BF16), npr, ls)
    tables = _peer_tables(peer_w_query, peer_sub_keys, peer_u, peer_v)
    x = _peer(x, mods[0], norm_g[0, 1].reshape(1, D), tables, 0, fg, npr, ls, final_norm=False)

    ug = _s5_in(x, mods[1], norm_g[1, 0].reshape(1, D), s5_w_in[0].astype(BF16), npr, ls)
    cs = S5_CHUNK
    mats = _s5_matrices(s5_a_re[0], s5_a_im[0], s5_log_dt[0], s5_b_re[0], s5_b_im[0],
                        s5_c_re[0], s5_c_im[0], s5_d[0])
    h0 = state_s5[:, 0]
    h0 = h0.transpose(2, 0, 1, 4, 3).reshape(S5_GROUPS, bs, 2, 1, 2 * S5_P)
    y_p, sf = _s5_core(ug, mats, 0, npr // cs, lp // cs, emit_states=True)
    (y_s,) = _s5_core(ug, mats, npr // cs, nsr // cs, ls // cs, h0=h0)
    x = _s5_out(x, mods[1], y_p, y_s, s5_w_out[0].astype(BF16), npr, ls)
    y_p, y_s = _peer(x, mods[1], norm_g[1, 1].reshape(1, D), tables, 1, fg, npr, ls, final_norm=True, split=True)

    y_prompt = y_p.reshape(bp, lp, D)
    y_sample = y_s.reshape(bs, ls, D)
    new_k = k32[:npr].reshape(bp, 1, lp, 2 * DA_HEADS, DA_DIM)
    new_v = v32[:npr].reshape(bp, 1, lp, DA_HEADS, 2 * DA_DIM)
    new_gla = st_p.reshape(bp, 2, GLA_DV, GLA_HEADS, GLA_DK).transpose(0, 1, 3, 4, 2)[:, None]
    kc = lp // cs
    fin = jnp.stack([sf[:, 0, kc - 1::kc], sf[:, 1, 0::kc]], axis=1)
    new_s5 = fin.reshape(S5_GROUPS, 2, bp, 2, S5_P).transpose(2, 1, 0, 4, 3)[:, None]
    return (y_prompt, y_sample, new_k, new_v, new_gla, new_s5)
```

```python
import functools
import math

import jax
import jax.numpy as jnp
from jax import lax
from jax.experimental import pallas as pl
from jax.experimental.pallas import tpu as pltpu

F32 = jnp.float32
BF16 = jnp.bfloat16
HI = lax.Precision.HIGHEST

D = 1024
EPS = 1e-6
ROPE_BASE = 10000.0
GRID_W = 64
DA_HEADS = 4
DA_DIM = 64
GLA_HEADS = 4
GLA_DK = 64
GLA_DV = 128
GLA_RANK = 16
GLA_TAU = 16.0
GLA_CHUNK = 64
S5_CH = 16
S5_GROUPS = D // S5_CH
S5_P = 64
S5_CHUNK = 16
S5_GROUP_BLOCK = 8
PEER_HEADS = 8
PEER_NKEYS = 128
PEER_TOPK = 16
IN0_PAD = 3200

ROW_TILE = 256
PEER_TOK = 512
PEER_EXP = 2048
PEER_SUB = 256
VMEM_LIMIT = 48 * 1024 * 1024

NT = (((1,), (1,)), ((), ()))
TN = (((0,), (0,)), ((), ()))


def _cp(*sem):
    return pltpu.CompilerParams(dimension_semantics=sem, vmem_limit_bytes=VMEM_LIMIT)


def _norm_mod(x, g, sc, sh):
    ms = jnp.mean(x * x, axis=-1, keepdims=True)
    return x * lax.rsqrt(ms + EPS) * g * (1.0 + sc) + sh


_GELU_C2 = -2.0 * 0.7978845608028654 * 1.4426950408889634
_GELU_C1 = _GELU_C2 * 0.044715


def _gelu(x):
    return x / (1.0 + jnp.exp2(x * (_GELU_C1 * (x * x) + _GELU_C2)))


def _mod_row_map(np_blocks, seq_blocks):
    def f(i):
        return jnp.where(i < np_blocks, 0, 1 + (i - np_blocks) // seq_blocks)
    return f


def _ada_body(c_ref, w_ref, b_ref, o_ref):
    c = c_ref[...]
    s = c / (1.0 + jnp.exp(-c))
    o_ref[0] = jnp.dot(s, w_ref[0], precision=HI, preferred_element_type=F32) + b_ref[0]


def _ada_mod(cond, ada_w, ada_b):
    depth, _, n = ada_w.shape
    rows = cond.shape[0]
    tn = 1536
    return pl.pallas_call(
        _ada_body,
        grid=(depth, n // tn),
        in_specs=[pl.BlockSpec((rows, D), lambda l, j: (0, 0)),
                  pl.BlockSpec((1, D, tn), lambda l, j: (l, 0, j)),
                  pl.BlockSpec((1, 1, tn), lambda l, j: (l, 0, j))],
        out_specs=pl.BlockSpec((1, rows, tn), lambda l, j: (l, 0, j)),
        out_shape=jax.ShapeDtypeStruct((depth, rows, n), F32),
        compiler_params=_cp("arbitrary", "arbitrary"),
        name="ada_mod",
    )(cond, ada_w, ada_b.reshape(depth, 1, n))


def _inproj0_body(xp_ref, xs_ref, mod_ref, g_ref, w_ref, cos_ref, sin_ref,
                  q_o, kb_o, vb_o, k32_o, v32_o, gqk_o, gv_o, gr_o, low_o, *, np_blocks):
    mod = mod_ref[0]
    x = jnp.where(pl.program_id(0) < np_blocks, xp_ref[...], xs_ref[...])
    h = _norm_mod(x, g_ref[...], mod[:, D:2 * D], mod[:, 0:D]).astype(BF16)
    y = jnp.dot(h, w_ref[...], preferred_element_type=F32)
    cos = cos_ref[...]
    sin = sin_ref[...]
    lane = lax.broadcasted_iota(jnp.int32, cos.shape, 1)
    first = (lane & 31) < 16

    def rope(t):
        rot = jnp.where(first, -pltpu.roll(t, 512 - 16, 1), pltpu.roll(t, 16, 1))
        return t * cos + rot * sin

    q = rope(y[:, 0:512])
    k = rope(y[:, 512:1024])
    v = y[:, 1024:1536]
    q_o[...] = q.astype(BF16)
    kb_o[...] = k.astype(BF16)
    vb_o[...] = v.astype(BF16)
    k32_o[...] = k
    v32_o[...] = v
    gqk_o[...] = y[:, 1536:2048].astype(BF16)
    gv_o[...] = y[:, 2048:2560].astype(BF16)
    gr_o[...] = y[:, 2560:3072].astype(BF16)
    low_o[...] = y[:, 3072:3200]


def _inproj0(x_p, x_s, mod, norm_g, w, cos_t, sin_t, np_rows, seq_rows):
    n = x_p.shape[0] + x_s.shape[0]
    tm = ROW_TILE
    npb, sqb = np_rows // tm, seq_rows // tm
    rowmap = _mod_row_map(npb, sqb)
    posmap = lambda i: (jnp.where(i < npb, 0, 1 + (i - npb) % sqb), 0)
    tok = lambda w_: pl.BlockSpec((tm, w_), lambda i: (i, 0))
    outs = [(512, BF16), (512, BF16), (512, BF16), (512, F32), (512, F32),
            (512, BF16), (512, BF16), (512, BF16), (128, F32)]
    return pl.pallas_call(
        functools.partial(_inproj0_body, np_blocks=npb),
        grid=(n // tm,),
        in_specs=[*_split_specs((tm, D), npb, 0),
                  pl.BlockSpec((1, 1, 6 * D), lambda i: (rowmap(i), 0, 0)),
                  pl.BlockSpec((1, D), lambda i: (0, 0)),
                  pl.BlockSpec((D, IN0_PAD), lambda i: (0, 0)),
                  pl.BlockSpec((tm, 512), posmap),
                  pl.BlockSpec((tm, 512), posmap)],
        out_specs=[tok(w_) for w_, _ in outs],
        out_shape=[jax.ShapeDtypeStruct((n, w_), dt) for w_, dt in outs],
        compiler_params=_cp("arbitrary"),
        name="inproj0",
    )(x_p, x_s, mod, norm_g, w, cos_t, sin_t)


def _attn_body(lam_ref, sg_ref, q_ref, k_ref, v_ref, *rest, lam_init, has_cache):
    if has_cache:
        kc_ref, vc_ref, o_ref = rest
    else:
        (o_ref,) = rest
    lf = lam_ref[...]
    lam = (jnp.exp(jnp.sum(lf[0:1] * lf[1:2], axis=-1, keepdims=True))
           - jnp.exp(jnp.sum(lf[2:3] * lf[3:4], axis=-1, keepdims=True)) + lam_init)
    q = q_ref[...] * (DA_DIM ** -0.5)
    lane = lax.broadcasted_iota(jnp.int32, q.shape, 1)
    zero = jnp.zeros_like(q)
    qm = (jnp.where(lane < DA_DIM, q, zero), jnp.where(lane >= DA_DIM, q, zero))
    k = k_ref[...]
    s = [lax.dot_general(qm[m], k, NT, preferred_element_type=F32) for m in range(2)]
    mx = [jnp.max(s[m], axis=-1, keepdims=True) for m in range(2)]
    if has_cache:
        kc = kc_ref[...].astype(BF16)
        sc = [lax.dot_general(qm[m], kc, NT, preferred_element_type=F32) for m in range(2)]
        mx = [jnp.maximum(mx[m], jnp.max(sc[m], axis=-1, keepdims=True)) for m in range(2)]
    e = [jnp.exp(s[m] - mx[m]) for m in range(2)]
    z = [jnp.sum(e[m], axis=-1, keepdims=True) for m in range(2)]
    if has_cache:
        ec = [jnp.exp(sc[m] - mx[m]) for m in range(2)]
        z = [z[m] + jnp.sum(ec[m], axis=-1, keepdims=True) for m in range(2)]
    w0 = 1.0 / z[0]
    w1 = lam / z[1]
    o = jnp.dot((e[0] * w0 - e[1] * w1).astype(BF16), v_ref[...], preferred_element_type=F32)
    if has_cache:
        o = o + jnp.dot((ec[0] * w0 - ec[1] * w1).astype(BF16), vc_ref[...].astype(BF16),
                        preferred_element_type=F32)
    ms = jnp.mean(o * o, axis=-1, keepdims=True)
    o_ref[...] = (o * lax.rsqrt(ms + EPS) * sg_ref[...] * (1.0 - lam_init)).astype(o_ref.dtype)


def _diff_attention(q, k, v, da_lam, subln_g, lam_init, row0, batch, seq, cache=None):
    tq = 256
    qb0, kb0 = row0 // tq, row0 // seq
    in_specs = [pl.BlockSpec((4, DA_DIM), lambda b, h, i: (0, 0)),
                pl.BlockSpec((1, 128), lambda b, h, i: (0, 0)),
                pl.BlockSpec((tq, 128), lambda b, h, i: (qb0 + b * (seq // tq) + i, h)),
                pl.BlockSpec((seq, 128), lambda b, h, i: (kb0 + b, h)),
                pl.BlockSpec((seq, 128), lambda b, h, i: (kb0 + b, h))]
    args = [da_lam, subln_g, q, k, v]
    if cache is not None:
        kc, vc, j = cache
        past = kc.shape[2]
        in_specs += [pl.BlockSpec((None, None, past, 128), lambda b, h, i: (b, j, 0, h))] * 2
        args += [kc, vc]
    return pl.pallas_call(
        functools.partial(_attn_body, lam_init=lam_init, has_cache=cache is not None),
        grid=(batch, DA_HEADS, seq // tq),
        in_specs=in_specs,
        out_specs=pl.BlockSpec((tq, 128), lambda b, h, i: (b * (seq // tq) + i, h)),
        out_shape=jax.ShapeDtypeStruct((batch * seq, 512), BF16),
        compiler_params=_cp("arbitrary", "arbitrary", "arbitrary"),
        name="diff_attn_ctx" if cache is not None else "diff_attn",
    )(*args)


def _gla_body(qk_ref, v_ref, r_ref, low_ref, gup_ref, gb_ref, ng_ref, *rest, seq, has_init):
    if has_init:
        s0_ref, o_ref, sf_ref, st_scr, of_scr, ob_scr, la_scr = rest
    else:
        o_ref, sf_ref, st_scr, of_scr, ob_scr, la_scr = rest
    c = GLA_CHUNK
    n = seq // c
    hk = GLA_HEADS * GLA_DK
    low = low_ref[...]
    for d in range(2):
        z = jnp.dot(low, gup_ref[d], precision=HI, preferred_element_type=F32) + gb_ref[d]
        la_scr[d] = (jnp.minimum(z, 0.0) - jnp.log(1.0 + jnp.exp(-jnp.abs(z)))) * (1.0 / GLA_TAU)
    if has_init:
        st_scr[...] = s0_ref[0]
    else:
        st_scr[...] = jnp.zeros(st_scr.shape, F32)
    row = lax.broadcasted_iota(jnp.int32, (2 * c, 2 * c), 0)
    col = lax.broadcasted_iota(jnp.int32, (2 * c, 2 * c), 1)
    fwd_blk = jnp.logical_and(row < c, col <= row)
    bwd_blk = jnp.logical_and(jnp.logical_and(row >= c, col >= c), col >= row)
    keep = jnp.logical_or(fwd_blk, bwd_blk)
    keep_f = keep.astype(F32)
    lane = lax.broadcasted_iota(jnp.int32, (2 * c, 128), 1)
    low_half = lane < GLA_DK
    top_rows = lax.broadcasted_iota(jnp.int32, (2 * c, hk), 0) < c

    def step(i, carry):
        rf = pl.ds(pl.multiple_of(i * c, c), c)
        rb = pl.ds(pl.multiple_of((n - 1 - i) * c, c), c)
        qk = jnp.concatenate([qk_ref[rf, :], qk_ref[rb, :]], axis=0).astype(F32)
        q = qk[:, 0:hk] * (GLA_DK ** -0.5)
        k = qk[:, hk:2 * hk]
        v = jnp.concatenate([v_ref[rf, :], v_ref[rb, :]], axis=0)
        g = jnp.concatenate([la_scr[0, rf, :], la_scr[1, rb, :]], axis=0)
        b = jnp.dot(keep_f, g, precision=HI, preferred_element_type=F32)
        b_tot = (b[c - 1:c], b[c:c + 1])
        q_dec = q * jnp.exp(b)
        k_inv = k * jnp.exp(-b)
        k_end = k * jnp.exp(jnp.where(top_rows, b_tot[0], b_tot[1]) - b)
        st = (st_scr[0], st_scr[1])
        st_b = (st[0].astype(BF16), st[1].astype(BF16))
        outs, news = [], ([], [])
        for pair in range(GLA_HEADS // 2):
            ps = slice(pair * 128, (pair + 1) * 128)
            kin = k_inv[:, ps].astype(BF16)
            new = [None, None]
            for sub in range(2):
                hh = pair * 2 + sub
                vs = slice(hh * GLA_DV, (hh + 1) * GLA_DV)
                sel = low_half if sub == 0 else jnp.logical_not(low_half)
                qd = jnp.where(sel, q_dec[:, ps], 0.0).astype(BF16)
                ke = jnp.where(sel, k_end[:, ps], 0.0).astype(BF16)
                att = lax.dot_general(qd, kin, NT, preferred_element_type=F32)
                att = jnp.where(keep, att, 0.0).astype(BF16)
                intra = jnp.dot(att, v[:, vs], preferred_element_type=F32)
                inter = jnp.concatenate(
                    [lax.dot_general(qd[d * c:(d + 1) * c], st_b[d][:, ps], NT, preferred_element_type=F32)
                     for d in range(2)], axis=0)
                outs.append(inter + intra)
                for d in range(2):
                    upd = lax.dot_general(v[d * c:(d + 1) * c, vs], ke[d * c:(d + 1) * c], TN,
                                          preferred_element_type=F32)
                    new[d] = upd if new[d] is None else new[d] + upd
            for d in range(2):
                news[d].append(new[d])
        for d in range(2):
            st_scr[d] = st[d] * jnp.exp(b_tot[d]) + jnp.concatenate(news[d], axis=1)
        o = jnp.concatenate(outs, axis=1)
        of_scr[rf, :] = o[0:c]
        ob_scr[rb, :] = o[c:2 * c]
        return carry

    lax.fori_loop(0, n, step, 0, unroll=2)
    sf_ref[0] = st_scr[...]

    fin = 256

    def finish(i, carry):
        rows = pl.ds(pl.multiple_of(i * fin, fin), fin)
        o = of_scr[rows, :] + ob_scr[rows, :]
        r = r_ref[rows, :].astype(F32)
        gate = r / (1.0 + jnp.exp(-r))
        ng = ng_ref[...]
        for hh in range(GLA_HEADS):
            vs = slice(hh * GLA_DV, (hh + 1) * GLA_DV)
            oh = o[:, vs]
            ms = jnp.mean(oh * oh, axis=-1, keepdims=True)
            o_ref[rows, vs] = (oh * lax.rsqrt(ms + EPS) * ng * gate[:, vs]).astype(BF16)
        return carry

    lax.fori_loop(0, seq // fin, finish, 0)


def _gla(gqk, gv, gr, low, gup, gb, ng, row0, batch, seq, s0=None):
    b0 = row0 // seq
    tokmap = lambda b: (b0 + b, 0)
    hk = GLA_HEADS * GLA_DK
    in_specs = [pl.BlockSpec((seq, 512), tokmap), pl.BlockSpec((seq, 512), tokmap),
                pl.BlockSpec((seq, 512), tokmap), pl.BlockSpec((seq, 128), tokmap),
                pl.BlockSpec((2, 128, hk), lambda b: (0, 0, 0)),
                pl.BlockSpec((2, 1, hk), lambda b: (0, 0, 0)),
                pl.BlockSpec((1, GLA_DV), lambda b: (0, 0))]
    args = [gqk, gv, gr, low, gup, gb, ng]
    if s0 is not None:
        in_specs.append(pl.BlockSpec((1, 2, GLA_DV, hk), lambda b: (b, 0, 0, 0)))
        args.append(s0)
    return pl.pallas_call(
        functools.partial(_gla_body, seq=seq, has_init=s0 is not None),
        grid=(batch,),
        in_specs=in_specs,
        out_specs=[pl.BlockSpec((seq, 512), lambda b: (b, 0)),
                   pl.BlockSpec((1, 2, GLA_DV, hk), lambda b: (b, 0, 0, 0))],
        out_shape=[jax.ShapeDtypeStruct((batch * seq, 512), BF16),
                   jax.ShapeDtypeStruct((batch, 2, GLA_DV, hk), F32)],
        scratch_shapes=[pltpu.VMEM((2, GLA_DV, hk), F32),
                        pltpu.VMEM((seq, 512), F32), pltpu.VMEM((seq, 512), F32),
                        pltpu.VMEM((2, seq, hk), F32)],
        compiler_params=_cp("arbitrary"),
        name="gla_ctx" if s0 is not None else "gla",
    )(*args)


def _outproj0_body(xp_ref, xs_ref, mod_ref, odap_ref, odas_ref, ogp_ref, ogs_ref, w_ref, o_ref, *, np_blocks):
    ctx = pl.program_id(0) < np_blocks
    x = jnp.where(ctx, xp_ref[...], xs_ref[...])
    oda = jnp.where(ctx, odap_ref[...], odas_ref[...])
    og = jnp.where(ctx, ogp_ref[...], ogs_ref[...])
    m = (jnp.dot(oda, w_ref[0:512, :], preferred_element_type=F32)
         + jnp.dot(og, w_ref[512:1024, :], preferred_element_type=F32))
    o_ref[...] = x + mod_ref[0][:, 2 * D:3 * D] * m


def _split_specs(block, np_blocks, axis):
    def at(k):
        return tuple(k if a == axis else 0 for a in range(len(block)))
    return (pl.BlockSpec(block, lambda i: at(jnp.minimum(i, np_blocks - 1))),
            pl.BlockSpec(block, lambda i: at(jnp.maximum(i - np_blocks, 0))))


def _outproj0(x_p, x_s, mod, oda_p, oda_s, og_p, og_s, w, np_rows, seq_rows):
    n = x_p.shape[0] + x_s.shape[0]
    tm = ROW_TILE
    npb = np_rows // tm
    rowmap = _mod_row_map(npb, seq_rows // tm)
    sp, ss = _split_specs((tm, 512), npb, 0)
    return pl.pallas_call(
        functools.partial(_outproj0_body, np_blocks=npb),
        grid=(n // tm,),
        in_specs=[*_split_specs((tm, D), npb, 0),
                  pl.BlockSpec((1, 1, 6 * D), lambda i: (rowmap(i), 0, 0)),
                  sp, ss, sp, ss,
                  pl.BlockSpec((D, D), lambda i: (0, 0))],
        out_specs=pl.BlockSpec((tm, D), lambda i: (i, 0)),
        out_shape=jax.ShapeDtypeStruct((n, D), F32),
        compiler_params=_cp("arbitrary"),
        name="outproj0",
    )(x_p, x_s, mod, oda_p, oda_s, og_p, og_s, w)


def _route_body(x_ref, mod_ref, g_ref, wq_ref, sk_ref, ht_o, nsel_o, e1_o, rank_o, e2_o, top_scr, s_scr):
    mod = mod_ref[0]
    h = _norm_mod(x_ref[...], g_ref[...], mod[:, 4 * D:5 * D], mod[:, 3 * D:4 * D])
    ht_o[...] = h.T.astype(BF16)
    q = jnp.dot(h.astype(BF16), wq_ref[...], preferred_element_type=F32).astype(BF16)
    tm = q.shape[0]
    neg = -jnp.inf
    k = PEER_TOPK
    row8 = lax.broadcasted_iota(jnp.int32, (8, tm), 0)
    tiles = [slice(c * 128, (c + 1) * 128) for c in range(tm // 128)]
    for hh in range(PEER_HEADS):
        hs = slice(hh * PEER_NKEYS, (hh + 1) * PEER_NKEYS)
        for t in range(2):
            c0 = (hh * 2 + t) * PEER_NKEYS
            s_scr[t] = lax.dot_general(sk_ref[hh, t], q[:, c0:c0 + PEER_NKEYS], NT,
                                       preferred_element_type=F32)
            for cs in tiles:
                cur = s_scr[t, :, cs]
                rank = jnp.full((PEER_NKEYS, 128), float(k), F32)
                for j in range(k):
                    m = jnp.max(cur, axis=0, keepdims=True)
                    top_scr[t, j:j + 1, cs] = m
                    hit = cur == m
                    if t == 1:
                        rank = jnp.where(hit, float(j), rank)
                    cur = jnp.where(hit, neg, cur)
                if t == 1:
                    rank_o[hs, cs] = rank.astype(BF16)
        t1 = top_scr[0]
        t2 = top_scr[1]
        slabs = [t1[0:1] + t2, t1[1:2] + t2[0:8], t1[2:3] + t2[0:8], t1[3:4] + t2[0:8],
                 t2[0:1] + t1[8:16]]
        for j in range(3):
            slabs.append(jnp.where(row8 >= 4, t2[j:j + 1] + t1[0:8], neg))
        cand = jnp.concatenate(slabs, axis=0)
        top = t1[0:1] + t2[0:1]
        zsum = jnp.zeros_like(top)
        kth = top
        for j in range(k):
            kth = jnp.max(cand, axis=0, keepdims=True)
            zsum = zsum + jnp.exp(kth - top)
            cand = jnp.where(cand == kth, neg, cand)
        zinv = 1.0 / zsum
        for c, cs in enumerate(tiles):
            s1 = s_scr[0, :, cs]
            nsel = jnp.zeros((PEER_NKEYS, 128), F32)
            for j in range(k):
                nsel = jnp.where(s1 + t2[j:j + 1, cs] >= kth[:, cs], float(j + 1), nsel)
            nsel_o[c, hs, :] = nsel
            e1_o[c, hs, :] = jnp.exp(s1 - t1[0:1, cs]) * zinv[:, cs]
            e2_o[hs, cs] = jnp.exp(s_scr[1, :, cs] - t2[0:1, cs]).astype(BF16)


def _route(x, mod, norm_g, wq, sk, layer, np_rows, seq_rows):
    n = x.shape[0]
    tm = ROW_TILE
    rowmap = _mod_row_map(np_rows // tm, seq_rows // tm)
    rt = lambda dt: jax.ShapeDtypeStruct((PEER_HEADS * PEER_NKEYS, n), dt)
    rspec = pl.BlockSpec((PEER_HEADS * PEER_NKEYS, tm), lambda i: (0, i))
    st_ = jax.ShapeDtypeStruct((n // 128, PEER_HEADS * PEER_NKEYS, 128), F32)
    sspec = pl.BlockSpec((tm // 128, PEER_HEADS * PEER_NKEYS, 128), lambda i: (i, 0, 0))
    return pl.pallas_call(
        _route_body,
        grid=(n // tm,),
        in_specs=[pl.BlockSpec((tm, D), lambda i: (i, 0)),
                  pl.BlockSpec((1, 1, 6 * D), lambda i: (rowmap(i), 0, 0)),
                  pl.BlockSpec((1, D), lambda i: (0, 0)),
                  pl.BlockSpec((None, D, 2 * PEER_HEADS * PEER_NKEYS), lambda i: (layer, 0, 0)),
                  pl.BlockSpec((None, PEER_HEADS, 2, PEER_NKEYS, 128), lambda i: (layer, 0, 0, 0, 0))],
        out_specs=[pl.BlockSpec((D, tm), lambda i: (0, i)), sspec, sspec, rspec, rspec],
        out_shape=[jax.ShapeDtypeStruct((D, n), BF16), st_, st_, rt(BF16), rt(BF16)],
        scratch_shapes=[pltpu.VMEM((2, PEER_TOPK, tm), F32), pltpu.VMEM((2, PEER_NKEYS, tm), F32)],
        compiler_params=_cp("arbitrary"),
        name="peer_route",
    )(x, mod, norm_g, wq, sk)


def _peer_body(ht_ref, u_ref, vt_ref, nsel_ref, e1_ref, rank_ref, e2_ref, x_ref, mod_ref, fg_ref,
               o_ref, acc_ref, p_ref, *, final_norm):
    j = pl.program_id(1)

    @pl.when(j == 0)
    def _():
        acc_ref[...] = jnp.zeros(acc_ref.shape, F32)

    nk = PEER_NKEYS
    tok = ht_ref.shape[1]
    na = PEER_EXP // nk
    a0 = pl.multiple_of(j * na, na)
    half = PEER_EXP // 2
    rg = 8
    zero = jnp.zeros((rg, PEER_SUB), BF16)

    def row_bcast(ref, hh, r, ts):
        parts = [jnp.broadcast_to(ref[c, pl.ds(hh * nk + a0, na), :][r:r + 1], (rg, 128))
                 for c in range(ts.start // 128, ts.stop // 128)]
        return jnp.concatenate(parts, axis=1).astype(BF16)

    for t0 in range(0, tok, PEER_SUB):
        ts = slice(t0, t0 + PEER_SUB)
        act = [jnp.dot(u_ref[hf * half:(hf + 1) * half, :], ht_ref[:, ts], preferred_element_type=F32)
               for hf in range(2)]
        for r in range(na):
            gate = [zero] * (nk // rg)
            for hh in range(PEER_HEADS):
                ns = row_bcast(nsel_ref, hh, r, ts)
                e1 = row_bcast(e1_ref, hh, r, ts)
                for g in range(nk // rg):
                    bs = slice(hh * nk + g * rg, hh * nk + (g + 1) * rg)
                    gate[g] = gate[g] + jnp.where(rank_ref[bs, ts] < ns, e2_ref[bs, ts], zero) * e1
            for g in range(nk // rg):
                lo = r * nk + g * rg
                a = act[lo // half][lo % half:lo % half + rg, :]
                p_ref[lo:lo + rg, ts] = _gelu(a).astype(BF16) * gate[g]
        for hf in range(2):
            ds = slice(hf * (D // 2), (hf + 1) * (D // 2))
            acc_ref[ds, ts] += jnp.dot(vt_ref[ds, :], p_ref[:, ts], preferred_element_type=F32)

    @pl.when(j == pl.num_programs(1) - 1)
    def _():
        y = x_ref[...] + mod_ref[0][:, 5 * D:6 * D] * acc_ref[...].T
        if final_norm:
            ms = jnp.mean(y * y, axis=-1, keepdims=True)
            y = y * lax.rsqrt(ms + EPS) * fg_ref[...]
        o_ref[...] = y


def _peer_dense(x, mod, ht, u, vt, layer, nsel, e1, rank, e2, fg, np_rows, seq_rows, final_norm,
                row0=0, rows=None):
    rows = x.shape[0] if rows is None else rows
    tk, ex = PEER_TOK, PEER_EXP
    b0 = row0 // tk
    rowmap = _mod_row_map(np_rows // tk, seq_rows // tk)
    rspec = pl.BlockSpec((PEER_HEADS * PEER_NKEYS, tk), lambda i, j: (0, b0 + i))
    sspec = pl.BlockSpec((tk // 128, PEER_HEADS * PEER_NKEYS, 128), lambda i, j: (b0 + i, 0, 0))
    return pl.pallas_call(
        functools.partial(_peer_body, final_norm=final_norm),
        grid=(rows // tk, u.shape[1] // ex),
        in_specs=[pl.BlockSpec((D, tk), lambda i, j: (0, b0 + i)),
                  pl.BlockSpec((None, ex, D), lambda i, j: (layer, j, 0)),
                  pl.BlockSpec((None, D, ex), lambda i, j: (layer, 0, j)),
                  sspec, sspec, rspec, rspec,
                  pl.BlockSpec((tk, D), lambda i, j: (b0 + i, 0)),
                  pl.BlockSpec((1, 1, 6 * D), lambda i, j: (rowmap(b0 + i), 0, 0)),
                  pl.BlockSpec((1, D), lambda i, j: (0, 0))],
        out_specs=pl.BlockSpec((tk, D), lambda i, j: (i, 0)),
        out_shape=jax.ShapeDtypeStruct((rows, D), F32),
        scratch_shapes=[pltpu.VMEM((D, tk), F32), pltpu.VMEM((ex, tk), BF16)],
        compiler_params=_cp("arbitrary", "arbitrary"),
        name="peer_dense",
    )(ht, u, vt, nsel, e1, rank, e2, x, mod, fg)


def _peer_tables(w_query, sub_keys, u, v):
    return w_query.astype(BF16), sub_keys.astype(BF16), u.astype(BF16), v.transpose(0, 2, 1).astype(BF16)


def _peer(x, mod, norm_g, tables, layer, fg, np_rows, seq_rows, final_norm, split=False):
    wq, sk, u, vt = tables
    ht, nsel, e1, rank, e2 = _route(x, mod, norm_g, wq, sk, layer, np_rows, seq_rows)
    dense = functools.partial(_peer_dense, x, mod, ht, u, vt, layer, nsel, e1, rank, e2, fg,
                              np_rows, seq_rows, final_norm)
    if split:
        return dense(row0=0, rows=np_rows), dense(row0=np_rows, rows=x.shape[0] - np_rows)
    return dense()


def _window_of(lane, i):
    return (lane >> 4) == i


def _s5_in_body(x_ref, mod_ref, g_ref, w_ref, o_ref, u_scr):
    mod = mod_ref[0]
    h = _norm_mod(x_ref[...], g_ref[...], mod[:, D:2 * D], mod[:, 0:D]).astype(BF16)
    u = jnp.dot(h, w_ref[...], preferred_element_type=F32)
    cs, per = S5_CHUNK, 128 // S5_CH
    nck = u.shape[0] // cs
    for j in range(D // 128):
        u_scr[j] = u[:, j * 128:(j + 1) * 128]
    lane = lax.broadcasted_iota(jnp.int32, (nck, 128), 1)
    for j in range(D // 128):
        xs = [u_scr[j, pl.ds(s, nck, stride=cs), :] for s in range(cs)]
        for gp in range(per):
            for half in range(cs // per):
                acc = None
                for s8 in range(per):
                    piece = xs[half * per + s8]
                    shift = ((s8 - gp) * S5_CH) % 128
                    if shift:
                        piece = pltpu.roll(piece, shift, 1)
                    acc = piece if acc is None else jnp.where(_window_of(lane, s8), piece, acc)
                o_ref[j * per + gp, :, half * 128:(half + 1) * 128] = acc.astype(BF16)


def _s5_in(x, mod, norm_g, w, np_rows, seq_rows):
    n = x.shape[0]
    tm = ROW_TILE
    rowmap = _mod_row_map(np_rows // tm, seq_rows // tm)
    nck = tm // S5_CHUNK
    return pl.pallas_call(
        _s5_in_body,
        grid=(n // tm,),
        in_specs=[pl.BlockSpec((tm, D), lambda i: (i, 0)),
                  pl.BlockSpec((1, 1, 6 * D), lambda i: (rowmap(i), 0, 0)),
                  pl.BlockSpec((1, D), lambda i: (0, 0)),
                  pl.BlockSpec((D, D), lambda i: (0, 0))],
        out_specs=pl.BlockSpec((S5_GROUPS, nck, S5_CHUNK * S5_CH), lambda i: (0, i, 0)),
        out_shape=jax.ShapeDtypeStruct((S5_GROUPS, n // S5_CHUNK, S5_CHUNK * S5_CH), BF16),
        scratch_shapes=[pltpu.VMEM((D // 128, tm, 128), F32)],
        compiler_params=_cp("arbitrary"),
        name="s5_in",
    )(x, mod, norm_g, w)


def _s5_core_body(u_ref, m_ref, win_ref, cout_ref, pw_ref, *rest, kseq, has_init, emit_states):
    rest = list(rest)
    h0_ref = rest.pop(0) if has_init else None
    y_ref = rest.pop(0)
    sf_ref = rest.pop(0) if emit_states else None
    kb = u_ref.shape[1]
    row = lax.broadcasted_iota(jnp.int32, (kb, 128), 0)
    kk = row & (kseq - 1)

    def cmul(a1, a2, x):
        return a1 * x + a2 * pltpu.roll(x, 64, 1)

    for gi in range(u_ref.shape[0]):
        u = u_ref[gi]
        y = jnp.dot(u, m_ref[gi], preferred_element_type=F32)
        for d in range(2):
            s = jnp.dot(u, win_ref[gi, d], preferred_element_type=F32)
            edge = (kk == 0) if d == 0 else (kk == kseq - 1)
            if has_init:
                h0 = h0_ref[gi, d]
                s = s + jnp.where(edge, cmul(pw_ref[gi, d, 0, 0:1], pw_ref[gi, d, 0, 1:2], h0), 0.0)
            sw = pltpu.roll(s, 64, 1)
            step, lvl = 1, 0
            while step < kseq:
                shift = step if d == 0 else kb - step
                ok = (kk >= step) if d == 0 else (kk < kseq - step)
                a1, a2 = pw_ref[gi, d, lvl, 0:1], pw_ref[gi, d, lvl, 1:2]
                sh, swh = pltpu.roll(s, shift, 0), pltpu.roll(sw, shift, 0)
                s = s + jnp.where(ok, a1 * sh + a2 * swh, 0.0)
                sw = sw + jnp.where(ok, a1 * swh - a2 * sh, 0.0)
                step, lvl = step * 2, lvl + 1
            if emit_states:
                sf_ref[gi, d] = s
            hin = pltpu.roll(s, 1, 0) if d == 0 else pltpu.roll(s, kb - 1, 0)
            if has_init:
                hin = jnp.where(edge, h0, hin)
            else:
                hin = jnp.where(edge, 0.0, hin)
            y = y + jnp.dot(hin.astype(BF16), cout_ref[gi, d], preferred_element_type=F32)
        y_ref[gi] = _gelu(y).astype(BF16)


def _s5_core(ug, mats, row0, rows, kseq, h0=None, emit_states=False):
    m, win, cout, pw = mats
    kb = 128
    b0 = row0 // kb
    nlv = pw.shape[2]
    gb = S5_GROUP_BLOCK
    in_specs = [pl.BlockSpec((gb, kb, 256), lambda g, i: (g, b0 + i, 0)),
                pl.BlockSpec((gb, 256, 256), lambda g, i: (g, 0, 0)),
                pl.BlockSpec((gb, 2, 256, 128), lambda g, i: (g, 0, 0, 0)),
                pl.BlockSpec((gb, 2, 128, 256), lambda g, i: (g, 0, 0, 0)),
                pl.BlockSpec((gb, 2, nlv, 2, 128), lambda g, i: (g, 0, 0, 0, 0))]
    args = [ug, m, win, cout, pw]
    if h0 is not None:
        assert kseq == kb
        in_specs.append(pl.BlockSpec((gb, None, 2, 1, 128), lambda g, i: (g, i, 0, 0, 0)))
        args.append(h0)
    out_specs = [pl.BlockSpec((gb, kb, 256), lambda g, i: (g, i, 0))]
    out_shape = [jax.ShapeDtypeStruct((S5_GROUPS, rows, 256), BF16)]
    if emit_states:
        out_specs.append(pl.BlockSpec((gb, 2, kb, 128), lambda g, i: (g, 0, i, 0)))
        out_shape.append(jax.ShapeDtypeStruct((S5_GROUPS, 2, rows, 128), F32))
    return pl.pallas_call(
        functools.partial(_s5_core_body, kseq=kseq, has_init=h0 is not None, emit_states=emit_states),
        grid=(S5_GROUPS // gb, rows // kb),
        in_specs=in_specs,
        out_specs=out_specs,
        out_shape=out_shape,
        compiler_params=_cp("arbitrary", "arbitrary"),
        name="s5_core_ctx" if h0 is not None else "s5_core",
    )(*args)


def _s5_out_body(x_ref, mod_ref, yp_ref, ys_ref, w_ref, o_ref, y_scr, *, np_blocks):
    cs, per = S5_CHUNK, 128 // S5_CH
    nck = yp_ref.shape[1]
    lane = lax.broadcasted_iota(jnp.int32, (nck, 128), 1)
    ctx = pl.program_id(0) < np_blocks
    for j in range(D // 128):
        gs = slice(j * per, (j + 1) * per)
        yj = jnp.where(ctx, yp_ref[gs], ys_ref[gs])
        src = [[yj[gp, :, half * 128:(half + 1) * 128].astype(F32) for half in range(cs // per)]
               for gp in range(per)]
        for t in range(cs):
            half, t8 = divmod(t, per)
            acc = None
            for gp in range(per):
                piece = src[gp][half]
                shift = ((gp - t8) * S5_CH) % 128
                if shift:
                    piece = pltpu.roll(piece, shift, 1)
                acc = piece if acc is None else jnp.where(_window_of(lane, gp), piece, acc)
            y_scr[j, pl.ds(t, nck, stride=cs), :] = acc
    y = jnp.concatenate([y_scr[j] for j in range(D // 128)], axis=1).astype(BF16)
    zz = jnp.dot(y, w_ref[...], preferred_element_type=F32)
    za = zz[:, 0:D]
    zb = zz[:, D:2 * D]
    o_ref[...] = x_ref[...] + mod_ref[0][:, 2 * D:3 * D] * (za / (1.0 + jnp.exp(-zb)))


def _s5_out(x, mod, y_p, y_s, w, np_rows, seq_rows):
    n = x.shape[0]
    tm = ROW_TILE
    npb = np_rows // tm
    rowmap = _mod_row_map(npb, seq_rows // tm)
    sp, ss = _split_specs((S5_GROUPS, tm // S5_CHUNK, S5_CHUNK * S5_CH), npb, 1)
    return pl.pallas_call(
        functools.partial(_s5_out_body, np_blocks=npb),
        grid=(n // tm,),
        in_specs=[pl.BlockSpec((tm, D), lambda i: (i, 0)),
                  pl.BlockSpec((1, 1, 6 * D), lambda i: (rowmap(i), 0, 0)),
                  sp, ss,
                  pl.BlockSpec((D, 2 * D), lambda i: (0, 0))],
        out_specs=pl.BlockSpec((tm, D), lambda i: (i, 0)),
        out_shape=jax.ShapeDtypeStruct((n, D), F32),
        scratch_shapes=[pltpu.VMEM((D // 128, tm, 128), F32)],
        compiler_params=_cp("arbitrary"),
        name="s5_out",
    )(x, mod, y_p, y_s, w)


def _s5_matrices(a_re, a_im, log_dt, b_re, b_im, c_re, c_im, d_skip):
    cs = S5_CHUNK
    dt = jnp.exp(log_dt)[..., None]
    lr, li = a_re * dt, a_im * dt

    def lam_pow(tau):
        mag = jnp.exp(lr[:, :, None, :] * tau[:, None])
        ang = li[:, :, None, :] * tau[:, None]
        return mag * jnp.cos(ang), mag * jnp.sin(ang)

    l1r, l1i = jnp.exp(lr) * jnp.cos(li), jnp.exp(lr) * jnp.sin(li)
    den = a_re * a_re + a_im * a_im
    cr = ((l1r - 1.0) * a_re + l1i * a_im) / den
    ci = (l1i * a_re - (l1r - 1.0) * a_im) / den
    bt_re = b_re.transpose(0, 1, 3, 2)
    bt_im = b_im.transpose(0, 1, 3, 2)
    bbr = cr[:, :, None, :] * bt_re - ci[:, :, None, :] * bt_im
    bbi = cr[:, :, None, :] * bt_im + ci[:, :, None, :] * bt_re

    pr, pi = lam_pow(jnp.arange(cs + 1, dtype=F32))
    clr = c_re[:, :, None] * pr[:, :, :, None, :] - c_im[:, :, None] * pi[:, :, :, None, :]
    cli = c_re[:, :, None] * pi[:, :, :, None, :] + c_im[:, :, None] * pr[:, :, :, None, :]
    cl = jnp.concatenate([clr[:, :, :cs], -cli[:, :, :cs]], axis=-1).reshape(2, S5_GROUPS, cs * S5_CH, 2 * S5_P)
    bb = jnp.concatenate([bbr, bbi], axis=-1)
    kern = jnp.einsum('dgeq,dgjq->dgej', bb, cl, precision=HI)
    w = cs * S5_CH
    kern_b = kern[1].reshape(S5_GROUPS, S5_CH, cs, S5_CH)[:, :, ::-1].reshape(S5_GROUPS, S5_CH, w)
    zpad = lambda a, lo, hi: jnp.pad(a, ((0, 0), (0, 0), (lo, hi)))
    mf = jnp.stack([zpad(kern[0][:, :, :w - S5_CH * s], S5_CH * s, 0) for s in range(cs)], axis=1)
    mb = jnp.stack([zpad(kern_b[:, :, S5_CH * (cs - 1 - s):], 0, S5_CH * (cs - 1 - s)) for s in range(cs)], axis=1)
    m = (mf + mb).reshape(S5_GROUPS, w, w)
    m = m + jnp.eye(cs * S5_CH, dtype=F32) * jnp.tile(d_skip.reshape(S5_GROUPS, 1, S5_CH), (1, 1, cs))

    def win_dir(d, powers):
        qr_, qi_ = pr[d][:, powers][:, :, None, :], pi[d][:, powers][:, :, None, :]
        wr = qr_ * bbr[d][:, None] - qi_ * bbi[d][:, None]
        wi = qr_ * bbi[d][:, None] + qi_ * bbr[d][:, None]
        return jnp.concatenate([wr, wi], axis=-1).reshape(S5_GROUPS, cs * S5_CH, 2 * S5_P)

    win = jnp.stack([win_dir(0, jnp.arange(cs - 1, -1, -1)), win_dir(1, jnp.arange(cs))], axis=1)

    def cout_dir(d, powers):
        z = jnp.concatenate([clr[d][:, powers], -cli[d][:, powers]], axis=-1)
        return z.reshape(S5_GROUPS, cs * S5_CH, 2 * S5_P).transpose(0, 2, 1)

    cout = jnp.stack([cout_dir(0, jnp.arange(1, cs + 1)), cout_dir(1, jnp.arange(cs, 0, -1))], axis=1)

    qr, qi = lam_pow(cs * (2.0 ** jnp.arange(7, dtype=F32)))
    a1 = jnp.concatenate([qr, qr], axis=-1)
    a2 = jnp.concatenate([-qi, qi], axis=-1)
    pw = jnp.stack([a1, a2], axis=3).transpose(1, 0, 2, 3, 4)
    return m.astype(BF16), win.astype(BF16), cout.astype(BF16), pw


def _rope_tables(dec_seq):
    rows = dec_seq // GRID_W
    row_id = jnp.repeat(jnp.arange(rows), GRID_W).astype(F32)
    col_id = jnp.tile(jnp.arange(GRID_W), rows).astype(F32)
    quarter = DA_DIM // 4
    inv = ROPE_BASE ** (-jnp.arange(quarter, dtype=F32) / quarter)
    ang_r = row_id[:, None] * inv
    ang_c = col_id[:, None] * inv
    ang = jnp.concatenate([ang_r, ang_r, ang_c, ang_c], axis=-1)
    ang = jnp.tile(ang, (1, 2 * DA_HEADS))
    pad = jnp.zeros((ROW_TILE, ang.shape[1]), F32)
    ang = jnp.concatenate([pad, ang], axis=0)
    return jnp.cos(ang), jnp.sin(ang)


def kernel(x_prompt, x_sample, cache_da_k, cache_da_v, state_gla, state_s5, c, c_ctx,
           ada_w, ada_b, norm_g, final_norm_g, mix0_w_in, mix0_w_out, da_lam, da_subln_g,
           gla_gate_up, gla_gate_bias, gla_norm_g, s5_w_in, s5_a_re, s5_a_im, s5_log_dt,
           s5_b_re, s5_b_im, s5_c_re, s5_c_im, s5_d, s5_w_out,
           peer_w_query, peer_sub_keys, peer_u, peer_v):
    bp, lp, _ = x_prompt.shape
    bs, ls, _ = x_sample.shape
    npr, nsr = bp * lp, bs * ls
    assert npr % ls == 0 and ls % ROW_TILE == 0 and lp % ROW_TILE == 0
    x_p, x_s = x_prompt.reshape(npr, D), x_sample.reshape(nsr, D)

    cond = jnp.concatenate([c_ctx[None], c, jnp.zeros((15 - bs, D), F32)], axis=0)
    mod_all = _ada_mod(cond, ada_w, ada_b)
    mods = [mod_all[l].reshape(16, 1, 6 * D) for l in range(mod_all.shape[0])]
    fg = final_norm_g.reshape(1, D)

    lam_init = 0.8 - 0.6 * math.exp(-0.3 * 0)
    cos_t, sin_t = _rope_tables(ls)
    w_in = jnp.pad(mix0_w_in[0], ((0, 0), (0, IN0_PAD - mix0_w_in.shape[2]))).astype(BF16)
    q, kb, vb, k32, v32, gqk, gv, gr, low = _inproj0(
        x_p, x_s, mods[0], norm_g[0, 0].reshape(1, D), w_in, cos_t, sin_t, npr, ls)
    sub_g = da_subln_g[0].reshape(1, 2 * DA_DIM)
    past = cache_da_k.shape[2]
    oda_p = _diff_attention(q, kb, vb, da_lam[0], sub_g, lam_init, 0, bp, lp)
    oda_s = _diff_attention(q, kb, vb, da_lam[0], sub_g, lam_init, npr, bs, ls,
                            cache=(cache_da_k.reshape(bs, -1, past, 512), cache_da_v.reshape(bs, -1, past, 512), 0))
    hk = GLA_HEADS * GLA_DK
    gup = jnp.zeros((2, 128, hk), F32)
    for d in range(2):
        gup = gup.at[d, d * GLA_RANK:(d + 1) * GLA_RANK].set(gla_gate_up[0, d])
    gbias = gla_gate_bias[0].reshape(2, 1, hk)
    ng = gla_norm_g[0].reshape(1, GLA_DV)
    s0 = state_gla[:, 0].transpose(0, 1, 4, 2, 3).reshape(bs, 2, GLA_DV, hk)
    og_p, st_p = _gla(gqk, gv, gr, low, gup, gbias, ng, 0, bp, lp)
    og_s, _ = _gla(gqk, gv, gr, low, gup, gbias, ng, npr, bs, ls, s0=s0)
    x = _outproj0(x_p, x_s, mods[0], oda_p, oda_s, og_p, og_s, mix0_w_out[0].astype(BF16), npr, ls)
    tables = _peer_tables(peer_w_query, peer_sub_keys, peer_u, peer_v)
    x = _peer(x, mods[0], norm_g[0, 1].reshape(1, D), tables, 0, fg, npr, ls, final_norm=False)

    ug = _s5_in(x, mods[1], norm_g[1, 0].reshape(1, D), s5_w_in[0].astype(BF16), npr, ls)
    cs = S5_CHUNK
    mats = _s5_matrices(s5_a_re[0], s5_a_im[0], s5_log_dt[0], s5_b_re[0], s5_b_im[0],
                        s5_c_re[0], s5_c_im[0], s5_d[0])
    h0 = state_s5[:, 0]
    h0 = h0.transpose(2, 0, 1, 4, 3).reshape(S5_GROUPS, bs, 2, 1, 2 * S5_P)
    y_p, sf = _s5_core(ug, mats, 0, npr // cs, lp // cs, emit_states=True)
    (y_s,) = _s5_core(ug, mats, npr // cs, nsr // cs, ls // cs, h0=h0)
    x = _s5_out(x, mods[1], y_p, y_s, s5_w_out[0].astype(BF16), npr, ls)
    y_p, y_s = _peer(x, mods[1], norm_g[1, 1].reshape(1, D), tables, 1, fg, npr, ls, final_norm=True, split=True)

    y_prompt = y_p.reshape(bp, lp, D)
    y_sample = y_s.reshape(bs, ls, D)
    new_k = k32[:npr].reshape(bp, 1, lp, 2 * DA_HEADS, DA_DIM)
    new_v = v32[:npr].reshape(bp, 1, lp, DA_HEADS, 2 * DA_DIM)
    new_gla = st_p.reshape(bp, 2, GLA_DV, GLA_HEADS, GLA_DK).transpose(0, 1, 3, 4, 2)[:, None]
    kc = lp // cs
    fin = jnp.stack([sf[:, 0, kc - 1::kc], sf[:, 1, 0::kc]], axis=1)
    new_s5 = fin.reshape(S5_GROUPS, 2, bp, 2, S5_P).transpose(2, 1, 0, 4, 3)[:, None]
    return (y_prompt, y_sample, new_k, new_v, new_gla, new_s5)
```

```python
import functools
import math

import jax
import jax.numpy as jnp
from jax import lax
from jax.experimental import pallas as pl
from jax.experimental.pallas import tpu as pltpu

F32 = jnp.float32
BF16 = jnp.bfloat16
HI = lax.Precision.HIGHEST

D = 1024
EPS = 1e-6
ROPE_BASE = 10000.0
GRID_W = 64
DA_HEADS = 4
DA_DIM = 64
GLA_HEADS = 4
GLA_DK = 64
GLA_DV = 128
GLA_RANK = 16
GLA_TAU = 16.0
GLA_CHUNK = 64
S5_CH = 16
S5_GROUPS = D // S5_CH
S5_P = 64
S5_CHUNK = 16
S5_GROUP_BLOCK = 8
PEER_HEADS = 8
PEER_NKEYS = 128
PEER_TOPK = 16
IN0_PAD = 3200

ROW_TILE = 256
ATTN_KEY_CHUNK = 1024
PEER_TOK = 512
PEER_EXP = 2048
PEER_SUB = 256
VMEM_LIMIT = 48 * 1024 * 1024

NT = (((1,), (1,)), ((), ()))
TN = (((0,), (0,)), ((), ()))


def _cp(*sem):
    return pltpu.CompilerParams(dimension_semantics=sem, vmem_limit_bytes=VMEM_LIMIT)


def _norm_mod(x, g, sc, sh):
    ms = jnp.mean(x * x, axis=-1, keepdims=True)
    return x * lax.rsqrt(ms + EPS) * g * (1.0 + sc) + sh


_GELU_C2 = -2.0 * 0.7978845608028654 * 1.4426950408889634
_GELU_C1 = _GELU_C2 * 0.044715


def _gelu(x):
    return x / (1.0 + jnp.exp2(x * (_GELU_C1 * (x * x) + _GELU_C2)))


def _mod_row_map(np_blocks, seq_blocks):
    def f(i):
        return jnp.where(i < np_blocks, 0, 1 + (i - np_blocks) // seq_blocks)
    return f


def _ada_body(c_ref, w_ref, b_ref, o_ref):
    c = c_ref[...]
    s = c / (1.0 + jnp.exp(-c))
    o_ref[0] = jnp.dot(s, w_ref[0], precision=HI, preferred_element_type=F32) + b_ref[0]


def _ada_mod(cond, ada_w, ada_b):
    depth, _, n = ada_w.shape
    rows = cond.shape[0]
    tn = 1536
    return pl.pallas_call(
        _ada_body,
        grid=(depth, n // tn),
        in_specs=[pl.BlockSpec((rows, D), lambda l, j: (0, 0)),
                  pl.BlockSpec((1, D, tn), lambda l, j: (l, 0, j)),
                  pl.BlockSpec((1, 1, tn), lambda l, j: (l, 0, j))],
        out_specs=pl.BlockSpec((1, rows, tn), lambda l, j: (l, 0, j)),
        out_shape=jax.ShapeDtypeStruct((depth, rows, n), F32),
        compiler_params=_cp("arbitrary", "arbitrary"),
        name="ada_mod",
    )(cond, ada_w, ada_b.reshape(depth, 1, n))


def _inproj0_body(xp_ref, xs_ref, mod_ref, g_ref, w_ref, cos_ref, sin_ref,
                  q_o, kb_o, vb_o, k32_o, v32_o, gqk_o, gv_o, gr_o, low_o, *, np_blocks):
    mod = mod_ref[0]
    x = jnp.where(pl.program_id(0) < np_blocks, xp_ref[...], xs_ref[...])
    h = _norm_mod(x, g_ref[...], mod[:, D:2 * D], mod[:, 0:D]).astype(BF16)
    y = jnp.dot(h, w_ref[...], preferred_element_type=F32)
    cos = cos_ref[...]
    sin = sin_ref[...]
    lane = lax.broadcasted_iota(jnp.int32, cos.shape, 1)
    first = (lane & 31) < 16

    def rope(t):
        rot = jnp.where(first, -pltpu.roll(t, 512 - 16, 1), pltpu.roll(t, 16, 1))
        return t * cos + rot * sin

    q = rope(y[:, 0:512])
    k = rope(y[:, 512:1024])
    v = y[:, 1024:1536]
    q_o[...] = q.astype(BF16)
    kb_o[...] = k.astype(BF16)
    vb_o[...] = v.astype(BF16)
    k32_o[...] = k
    v32_o[...] = v
    gqk_o[...] = y[:, 1536:2048].astype(BF16)
    gv_o[...] = y[:, 2048:2560].astype(BF16)
    gr_o[...] = y[:, 2560:3072].astype(BF16)
    low_o[...] = y[:, 3072:3200]


def _inproj0(x_p, x_s, mod, norm_g, w, cos_t, sin_t, np_rows, seq_rows):
    n = x_p.shape[0] + x_s.shape[0]
    tm = ROW_TILE
    npb, sqb = np_rows // tm, seq_rows // tm
    rowmap = _mod_row_map(npb, sqb)
    posmap = lambda i: (jnp.where(i < npb, 0, 1 + (i - npb) % sqb), 0)
    tok = lambda w_: pl.BlockSpec((tm, w_), lambda i: (i, 0))
    outs = [(512, BF16), (512, BF16), (512, BF16), (512, F32), (512, F32),
            (512, BF16), (512, BF16), (512, BF16), (128, F32)]
    return pl.pallas_call(
        functools.partial(_inproj0_body, np_blocks=npb),
        grid=(n // tm,),
        in_specs=[*_split_specs((tm, D), npb, 0),
                  pl.BlockSpec((1, 1, 6 * D), lambda i: (rowmap(i), 0, 0)),
                  pl.BlockSpec((1, D), lambda i: (0, 0)),
                  pl.BlockSpec((D, IN0_PAD), lambda i: (0, 0)),
                  pl.BlockSpec((tm, 512), posmap),
                  pl.BlockSpec((tm, 512), posmap)],
        out_specs=[tok(w_) for w_, _ in outs],
        out_shape=[jax.ShapeDtypeStruct((n, w_), dt) for w_, dt in outs],
        compiler_params=_cp("arbitrary"),
        name="inproj0",
    )(x_p, x_s, mod, norm_g, w, cos_t, sin_t)


def _attn_body(lam_ref, sg_ref, q_ref, k_ref, v_ref, *rest, lam_init, has_cache):
    if has_cache:
        kc_ref, vc_ref, o_ref = rest
    else:
        (o_ref,) = rest
    lf = lam_ref[...]
    lam = (jnp.exp(jnp.sum(lf[0:1] * lf[1:2], axis=-1, keepdims=True))
           - jnp.exp(jnp.sum(lf[2:3] * lf[3:4], axis=-1, keepdims=True)) + lam_init)
    q = q_ref[...] * (DA_DIM ** -0.5)
    lane = lax.broadcasted_iota(jnp.int32, q.shape, 1)
    zero = jnp.zeros_like(q)
    qm = (jnp.where(lane < DA_DIM, q, zero), jnp.where(lane >= DA_DIM, q, zero))
    ck = min(ATTN_KEY_CHUNK, k_ref.shape[0])
    chunks = [(k_ref[c0:c0 + ck, :], v_ref[c0:c0 + ck, :]) for c0 in range(0, k_ref.shape[0], ck)]
    if has_cache:
        chunks.append((kc_ref[...].astype(BF16), vc_ref[...].astype(BF16)))
    tq = q.shape[0]
    mx = [jnp.full((tq, 1), -1e30, F32) for _ in range(2)]
    z = [jnp.zeros((tq, 1), F32) for _ in range(2)]
    acc = [jnp.zeros((tq, 2 * DA_DIM), F32) for _ in range(2)]
    for kk, vv in chunks:
        for m in range(2):
            s = lax.dot_general(qm[m], kk, NT, preferred_element_type=F32)
            new = jnp.maximum(mx[m], jnp.max(s, axis=-1, keepdims=True))
            alpha = jnp.exp(mx[m] - new)
            p = jnp.exp(s - new)
            z[m] = alpha * z[m] + jnp.sum(p, axis=-1, keepdims=True)
            acc[m] = alpha * acc[m] + jnp.dot(p.astype(BF16), vv, preferred_element_type=F32)
            mx[m] = new
    o = acc[0] * (1.0 / z[0]) - acc[1] * (lam / z[1])
    ms = jnp.mean(o * o, axis=-1, keepdims=True)
    o_ref[...] = (o * lax.rsqrt(ms + EPS) * sg_ref[...] * (1.0 - lam_init)).astype(o_ref.dtype)


def _diff_attention(q, k, v, da_lam, subln_g, lam_init, row0, batch, seq, cache=None):
    tq = 256
    qb0, kb0 = row0 // tq, row0 // seq
    in_specs = [pl.BlockSpec((4, DA_DIM), lambda b, h, i: (0, 0)),
                pl.BlockSpec((1, 128), lambda b, h, i: (0, 0)),
                pl.BlockSpec((tq, 128), lambda b, h, i: (qb0 + b * (seq // tq) + i, h)),
                pl.BlockSpec((seq, 128), lambda b, h, i: (kb0 + b, h)),
                pl.BlockSpec((seq, 128), lambda b, h, i: (kb0 + b, h))]
    args = [da_lam, subln_g, q, k, v]
    if cache is not None:
        kc, vc, j = cache
        past = kc.shape[2]
        in_specs += [pl.BlockSpec((None, None, past, 128), lambda b, h, i: (b, j, 0, h))] * 2
        args += [kc, vc]
    return pl.pallas_call(
        functools.partial(_attn_body, lam_init=lam_init, has_cache=cache is not None),
        grid=(batch, DA_HEADS, seq // tq),
        in_specs=in_specs,
        out_specs=pl.BlockSpec((tq, 128), lambda b, h, i: (b * (seq // tq) + i, h)),
        out_shape=jax.ShapeDtypeStruct((batch * seq, 512), BF16),
        compiler_params=_cp("arbitrary", "arbitrary", "arbitrary"),
        name="diff_attn_ctx" if cache is not None else "diff_attn",
    )(*args)


def _gla_body(qk_ref, v_ref, r_ref, low_ref, gup_ref, gb_ref, ng_ref, *rest, seq, has_init):
    if has_init:
        s0_ref, o_ref, sf_ref, st_scr, of_scr, ob_scr, la_scr = rest
    else:
        o_ref, sf_ref, st_scr, of_scr, ob_scr, la_scr = rest
    c = GLA_CHUNK
    n = seq // c
    hk = GLA_HEADS * GLA_DK
    low = low_ref[...]
    for d in range(2):
        z = jnp.dot(low, gup_ref[d], precision=HI, preferred_element_type=F32) + gb_ref[d]
        la_scr[d] = (jnp.minimum(z, 0.0) - jnp.log(1.0 + jnp.exp(-jnp.abs(z)))) * (1.0 / GLA_TAU)
    if has_init:
        st_scr[...] = s0_ref[0]
    else:
        st_scr[...] = jnp.zeros(st_scr.shape, F32)
    row = lax.broadcasted_iota(jnp.int32, (2 * c, 2 * c), 0)
    col = lax.broadcasted_iota(jnp.int32, (2 * c, 2 * c), 1)
    fwd_blk = jnp.logical_and(row < c, col <= row)
    bwd_blk = jnp.logical_and(jnp.logical_and(row >= c, col >= c), col >= row)
    keep = jnp.logical_or(fwd_blk, bwd_blk)
    keep_f = keep.astype(F32)
    lane = lax.broadcasted_iota(jnp.int32, (2 * c, 128), 1)
    low_half = lane < GLA_DK
    top_rows = lax.broadcasted_iota(jnp.int32, (2 * c, hk), 0) < c

    def step(i, carry):
        rf = pl.ds(pl.multiple_of(i * c, c), c)
        rb = pl.ds(pl.multiple_of((n - 1 - i) * c, c), c)
        qk = jnp.concatenate([qk_ref[rf, :], qk_ref[rb, :]], axis=0).astype(F32)
        q = qk[:, 0:hk] * (GLA_DK ** -0.5)
        k = qk[:, hk:2 * hk]
        v = jnp.concatenate([v_ref[rf, :], v_ref[rb, :]], axis=0)
        g = jnp.concatenate([la_scr[0, rf, :], la_scr[1, rb, :]], axis=0)
        b = jnp.dot(keep_f, g, precision=HI, preferred_element_type=F32)
        b_tot = (b[c - 1:c], b[c:c + 1])
        q_dec = q * jnp.exp(b)
        k_inv = k * jnp.exp(-b)
        k_end = k * jnp.exp(jnp.where(top_rows, b_tot[0], b_tot[1]) - b)
        st = (st_scr[0], st_scr[1])
        st_b = (st[0].astype(BF16), st[1].astype(BF16))
        outs, news = [], ([], [])
        for pair in range(GLA_HEADS // 2):
            ps = slice(pair * 128, (pair + 1) * 128)
            kin = k_inv[:, ps].astype(BF16)
            new = [None, None]
            for sub in range(2):
                hh = pair * 2 + sub
                vs = slice(hh * GLA_DV, (hh + 1) * GLA_DV)
                sel = low_half if sub == 0 else jnp.logical_not(low_half)
                qd = jnp.where(sel, q_dec[:, ps], 0.0).astype(BF16)
                ke = jnp.where(sel, k_end[:, ps], 0.0).astype(BF16)
                att = lax.dot_general(qd, kin, NT, preferred_element_type=F32)
                att = jnp.where(keep, att, 0.0).astype(BF16)
                intra = jnp.dot(att, v[:, vs], preferred_element_type=F32)
                inter = jnp.concatenate(
                    [lax.dot_general(qd[d * c:(d + 1) * c], st_b[d][:, ps], NT, preferred_element_type=F32)
                     for d in range(2)], axis=0)
                outs.append(inter + intra)
                for d in range(2):
                    upd = lax.dot_general(v[d * c:(d + 1) * c, vs], ke[d * c:(d + 1) * c], TN,
                                          preferred_element_type=F32)
                    new[d] = upd if new[d] is None else new[d] + upd
            for d in range(2):
                news[d].append(new[d])
        for d in range(2):
            st_scr[d] = st[d] * jnp.exp(b_tot[d]) + jnp.concatenate(news[d], axis=1)
        o = jnp.concatenate(outs, axis=1)
        of_scr[rf, :] = o[0:c]
        ob_scr[rb, :] = o[c:2 * c]
        return carry

    lax.fori_loop(0, n, step, 0, unroll=2)
    sf_ref[0] = st_scr[...]

    fin = 256

    def finish(i, carry):
        rows = pl.ds(pl.multiple_of(i * fin, fin), fin)
        o = of_scr[rows, :] + ob_scr[rows, :]
        r = r_ref[rows, :].astype(F32)
        gate = r / (1.0 + jnp.exp(-r))
        ng = ng_ref[...]
        for hh in range(GLA_HEADS):
            vs = slice(hh * GLA_DV, (hh + 1) * GLA_DV)
            oh = o[:, vs]
            ms = jnp.mean(oh * oh, axis=-1, keepdims=True)
            o_ref[rows, vs] = (oh * lax.rsqrt(ms + EPS) * ng * gate[:, vs]).astype(BF16)
        return carry

    lax.fori_loop(0, seq // fin, finish, 0)


def _gla(gqk, gv, gr, low, gup, gb, ng, row0, batch, seq, s0=None):
    b0 = row0 // seq
    tokmap = lambda b: (b0 + b, 0)
    hk = GLA_HEADS * GLA_DK
    in_specs = [pl.BlockSpec((seq, 512), tokmap), pl.BlockSpec((seq, 512), tokmap),
                pl.BlockSpec((seq, 512), tokmap), pl.BlockSpec((seq, 128), tokmap),
                pl.BlockSpec((2, 128, hk), lambda b: (0, 0, 0)),
                pl.BlockSpec((2, 1, hk), lambda b: (0, 0, 0)),
                pl.BlockSpec((1, GLA_DV), lambda b: (0, 0))]
    args = [gqk, gv, gr, low, gup, gb, ng]
    if s0 is not None:
        in_specs.append(pl.BlockSpec((1, 2, GLA_DV, hk), lambda b: (b, 0, 0, 0)))
        args.append(s0)
    return pl.pallas_call(
        functools.partial(_gla_body, seq=seq, has_init=s0 is not None),
        grid=(batch,),
        in_specs=in_specs,
        out_specs=[pl.BlockSpec((seq, 512), lambda b: (b, 0)),
                   pl.BlockSpec((1, 2, GLA_DV, hk), lambda b: (b, 0, 0, 0))],
        out_shape=[jax.ShapeDtypeStruct((batch * seq, 512), BF16),
                   jax.ShapeDtypeStruct((batch, 2, GLA_DV, hk), F32)],
        scratch_shapes=[pltpu.VMEM((2, GLA_DV, hk), F32),
                        pltpu.VMEM((seq, 512), F32), pltpu.VMEM((seq, 512), F32),
                        pltpu.VMEM((2, seq, hk), F32)],
        compiler_params=_cp("arbitrary"),
        name="gla_ctx" if s0 is not None else "gla",
    )(*args)


def _outproj0_body(xp_ref, xs_ref, mod_ref, odap_ref, odas_ref, ogp_ref, ogs_ref, w_ref, o_ref, *, np_blocks):
    ctx = pl.program_id(0) < np_blocks
    x = jnp.where(ctx, xp_ref[...], xs_ref[...])
    oda = jnp.where(ctx, odap_ref[...], odas_ref[...])
    og = jnp.where(ctx, ogp_ref[...], ogs_ref[...])
    m = (jnp.dot(oda, w_ref[0:512, :], preferred_element_type=F32)
         + jnp.dot(og, w_ref[512:1024, :], preferred_element_type=F32))
    o_ref[...] = x + mod_ref[0][:, 2 * D:3 * D] * m


def _split_specs(block, np_blocks, axis):
    def at(k):
        return tuple(k if a == axis else 0 for a in range(len(block)))
    return (pl.BlockSpec(block, lambda i: at(jnp.minimum(i, np_blocks - 1))),
            pl.BlockSpec(block, lambda i: at(jnp.maximum(i - np_blocks, 0))))


def _outproj0(x_p, x_s, mod, oda_p, oda_s, og_p, og_s, w, np_rows, seq_rows):
    n = x_p.shape[0] + x_s.shape[0]
    tm = ROW_TILE
    npb = np_rows // tm
    rowmap = _mod_row_map(npb, seq_rows // tm)
    sp, ss = _split_specs((tm, 512), npb, 0)
    return pl.pallas_call(
        functools.partial(_outproj0_body, np_blocks=npb),
        grid=(n // tm,),
        in_specs=[*_split_specs((tm, D), npb, 0),
                  pl.BlockSpec((1, 1, 6 * D), lambda i: (rowmap(i), 0, 0)),
                  sp, ss, sp, ss,
                  pl.BlockSpec((D, D), lambda i: (0, 0))],
        out_specs=pl.BlockSpec((tm, D), lambda i: (i, 0)),
        out_shape=jax.ShapeDtypeStruct((n, D), F32),
        compiler_params=_cp("arbitrary"),
        name="outproj0",
    )(x_p, x_s, mod, oda_p, oda_s, og_p, og_s, w)


def _route_body(x_ref, mod_ref, g_ref, wq_ref, sk_ref, ht_o, nsel_o, e1_o, rank_o, e2_o, top_scr, s_scr):
    mod = mod_ref[0]
    h = _norm_mod(x_ref[...], g_ref[...], mod[:, 4 * D:5 * D], mod[:, 3 * D:4 * D])
    ht_o[...] = h.T.astype(BF16)
    q = jnp.dot(h.astype(BF16), wq_ref[...], preferred_element_type=F32).astype(BF16)
    tm = q.shape[0]
    neg = -jnp.inf
    k = PEER_TOPK
    row8 = lax.broadcasted_iota(jnp.int32, (8, tm), 0)
    tiles = [slice(c * 128, (c + 1) * 128) for c in range(tm // 128)]
    for hh in range(PEER_HEADS):
        hs = slice(hh * PEER_NKEYS, (hh + 1) * PEER_NKEYS)
        for t in range(2):
            c0 = (hh * 2 + t) * PEER_NKEYS
            s_scr[t] = lax.dot_general(sk_ref[hh, t], q[:, c0:c0 + PEER_NKEYS], NT,
                                       preferred_element_type=F32)
            for cs in tiles:
                cur = s_scr[t, :, cs]
                rank = jnp.full((PEER_NKEYS, 128), float(k), F32)
                for j in range(k):
                    m = jnp.max(cur, axis=0, keepdims=True)
                    top_scr[t, j:j + 1, cs] = m
                    hit = cur == m
                    if t == 1:
                        rank = jnp.where(hit, float(j), rank)
                    cur = jnp.where(hit, neg, cur)
                if t == 1:
                    rank_o[hs, cs] = rank.astype(BF16)
        t1 = top_scr[0]
        t2 = top_scr[1]
        slabs = [t1[0:1] + t2, t1[1:2] + t2[0:8], t1[2:3] + t2[0:8], t1[3:4] + t2[0:8],
                 t2[0:1] + t1[8:16]]
        for j in range(3):
            slabs.append(jnp.where(row8 >= 4, t2[j:j + 1] + t1[0:8], neg))
        cand = jnp.concatenate(slabs, axis=0)
        top = t1[0:1] + t2[0:1]
        zsum = jnp.zeros_like(top)
        kth = top
        for j in range(k):
            kth = jnp.max(cand, axis=0, keepdims=True)
            zsum = zsum + jnp.exp(kth - top)
            cand = jnp.where(cand == kth, neg, cand)
        zinv = 1.0 / zsum
        for c, cs in enumerate(tiles):
            s1 = s_scr[0, :, cs]
            nsel = jnp.zeros((PEER_NKEYS, 128), F32)
            for j in range(k):
                nsel = jnp.where(s1 + t2[j:j + 1, cs] >= kth[:, cs], float(j + 1), nsel)
            nsel_o[c, hs, :] = nsel
            e1_o[c, hs, :] = jnp.exp(s1 - t1[0:1, cs]) * zinv[:, cs]
            e2_o[hs, cs] = jnp.exp(s_scr[1, :, cs] - t2[0:1, cs]).astype(BF16)


def _route(x, mod, norm_g, wq, sk, layer, np_rows, seq_rows):
    n = x.shape[0]
    tm = ROW_TILE
    rowmap = _mod_row_map(np_rows // tm, seq_rows // tm)
    rt = lambda dt: jax.ShapeDtypeStruct((PEER_HEADS * PEER_NKEYS, n), dt)
    rspec = pl.BlockSpec((PEER_HEADS * PEER_NKEYS, tm), lambda i: (0, i))
    st_ = jax.ShapeDtypeStruct((n // 128, PEER_HEADS * PEER_NKEYS, 128), F32)
    sspec = pl.BlockSpec((tm // 128, PEER_HEADS * PEER_NKEYS, 128), lambda i: (i, 0, 0))
    return pl.pallas_call(
        _route_body,
        grid=(n // tm,),
        in_specs=[pl.BlockSpec((tm, D), lambda i: (i, 0)),
                  pl.BlockSpec((1, 1, 6 * D), lambda i: (rowmap(i), 0, 0)),
                  pl.BlockSpec((1, D), lambda i: (0, 0)),
                  pl.BlockSpec((None, D, 2 * PEER_HEADS * PEER_NKEYS), lambda i: (layer, 0, 0)),
                  pl.BlockSpec((None, PEER_HEADS, 2, PEER_NKEYS, 128), lambda i: (layer, 0, 0, 0, 0))],
        out_specs=[pl.BlockSpec((D, tm), lambda i: (0, i)), sspec, sspec, rspec, rspec],
        out_shape=[jax.ShapeDtypeStruct((D, n), BF16), st_, st_, rt(BF16), rt(BF16)],
        scratch_shapes=[pltpu.VMEM((2, PEER_TOPK, tm), F32), pltpu.VMEM((2, PEER_NKEYS, tm), F32)],
        compiler_params=_cp("arbitrary"),
        name="peer_route",
    )(x, mod, norm_g, wq, sk)


def _peer_body(ht_ref, u_ref, vt_ref, nsel_ref, e1_ref, rank_ref, e2_ref, x_ref, mod_ref, fg_ref,
               o_ref, acc_ref, p_ref, *, final_norm):
    j = pl.program_id(1)

    @pl.when(j == 0)
    def _():
        acc_ref[...] = jnp.zeros(acc_ref.shape, F32)

    nk = PEER_NKEYS
    tok = ht_ref.shape[1]
    na = PEER_EXP // nk
    a0 = pl.multiple_of(j * na, na)
    half = PEER_EXP // 2
    rg = 8
    zero = jnp.zeros((rg, PEER_SUB), BF16)

    def row_bcast(ref, hh, r, ts):
        parts = [jnp.broadcast_to(ref[c, pl.ds(hh * nk + a0, na), :][r:r + 1], (rg, 128))
                 for c in range(ts.start // 128, ts.stop // 128)]
        return jnp.concatenate(parts, axis=1).astype(BF16)

    for t0 in range(0, tok, PEER_SUB):
        ts = slice(t0, t0 + PEER_SUB)
        act = [jnp.dot(u_ref[hf * half:(hf + 1) * half, :], ht_ref[:, ts], preferred_element_type=F32)
               for hf in range(2)]
        for r in range(na):
            gate = [zero] * (nk // rg)
            for hh in range(PEER_HEADS):
                ns = row_bcast(nsel_ref, hh, r, ts)
                e1 = row_bcast(e1_ref, hh, r, ts)
                for g in range(nk // rg):
                    bs = slice(hh * nk + g * rg, hh * nk + (g + 1) * rg)
                    gate[g] = gate[g] + jnp.where(rank_ref[bs, ts] < ns, e2_ref[bs, ts], zero) * e1
            for g in range(nk // rg):
                lo = r * nk + g * rg
                a = act[lo // half][lo % half:lo % half + rg, :]
                p_ref[lo:lo + rg, ts] = _gelu(a).astype(BF16) * gate[g]
        for hf in range(2):
            ds = slice(hf * (D // 2), (hf + 1) * (D // 2))
            acc_ref[ds, ts] += jnp.dot(vt_ref[ds, :], p_ref[:, ts], preferred_element_type=F32)

    @pl.when(j == pl.num_programs(1) - 1)
    def _():
        y = x_ref[...] + mod_ref[0][:, 5 * D:6 * D] * acc_ref[...].T
        if final_norm:
            ms = jnp.mean(y * y, axis=-1, keepdims=True)
            y = y * lax.rsqrt(ms + EPS) * fg_ref[...]
        o_ref[...] = y


def _peer_dense(x, mod, ht, u, vt, layer, nsel, e1, rank, e2, fg, np_rows, seq_rows, final_norm,
                row0=0, rows=None):
    rows = x.shape[0] if rows is None else rows
    tk, ex = PEER_TOK, PEER_EXP
    b0 = row0 // tk
    rowmap = _mod_row_map(np_rows // tk, seq_rows // tk)
    rspec = pl.BlockSpec((PEER_HEADS * PEER_NKEYS, tk), lambda i, j: (0, b0 + i))
    sspec = pl.BlockSpec((tk // 128, PEER_HEADS * PEER_NKEYS, 128), lambda i, j: (b0 + i, 0, 0))
    return pl.pallas_call(
        functools.partial(_peer_body, final_norm=final_norm),
        grid=(rows // tk, u.shape[1] // ex),
        in_specs=[pl.BlockSpec((D, tk), lambda i, j: (0, b0 + i)),
                  pl.BlockSpec((None, ex, D), lambda i, j: (layer, j, 0)),
                  pl.BlockSpec((None, D, ex), lambda i, j: (layer, 0, j)),
                  sspec, sspec, rspec, rspec,
                  pl.BlockSpec((tk, D), lambda i, j: (b0 + i, 0)),
                  pl.BlockSpec((1, 1, 6 * D), lambda i, j: (rowmap(b0 + i), 0, 0)),
                  pl.BlockSpec((1, D), lambda i, j: (0, 0))],
        out_specs=pl.BlockSpec((tk, D), lambda i, j: (i, 0)),
        out_shape=jax.ShapeDtypeStruct((rows, D), F32),
        scratch_shapes=[pltpu.VMEM((D, tk), F32), pltpu.VMEM((ex, tk), BF16)],
        compiler_params=_cp("arbitrary", "arbitrary"),
        name="peer_dense",
    )(ht, u, vt, nsel, e1, rank, e2, x, mod, fg)


def _peer_tables(w_query, sub_keys, u, v):
    return w_query.astype(BF16), sub_keys.astype(BF16), u.astype(BF16), v.transpose(0, 2, 1).astype(BF16)


def _peer(x, mod, norm_g, tables, layer, fg, np_rows, seq_rows, final_norm, split=False):
    wq, sk, u, vt = tables
    ht, nsel, e1, rank, e2 = _route(x, mod, norm_g, wq, sk, layer, np_rows, seq_rows)
    dense = functools.partial(_peer_dense, x, mod, ht, u, vt, layer, nsel, e1, rank, e2, fg,
                              np_rows, seq_rows, final_norm)
    if split:
        return dense(row0=0, rows=np_rows), dense(row0=np_rows, rows=x.shape[0] - np_rows)
    return dense()


def _window_of(lane, i):
    return (lane >> 4) == i


def _s5_in_body(x_ref, mod_ref, g_ref, w_ref, o_ref, u_scr):
    mod = mod_ref[0]
    h = _norm_mod(x_ref[...], g_ref[...], mod[:, D:2 * D], mod[:, 0:D]).astype(BF16)
    u = jnp.dot(h, w_ref[...], preferred_element_type=F32)
    cs, per = S5_CHUNK, 128 // S5_CH
    nck = u.shape[0] // cs
    for j in range(D // 128):
        u_scr[j] = u[:, j * 128:(j + 1) * 128]
    lane = lax.broadcasted_iota(jnp.int32, (nck, 128), 1)
    for j in range(D // 128):
        xs = [u_scr[j, pl.ds(s, nck, stride=cs), :] for s in range(cs)]
        for gp in range(per):
            for half in range(cs // per):
                acc = None
                for s8 in range(per):
                    piece = xs[half * per + s8]
                    shift = ((s8 - gp) * S5_CH) % 128
                    if shift:
                        piece = pltpu.roll(piece, shift, 1)
                    acc = piece if acc is None else jnp.where(_window_of(lane, s8), piece, acc)
                o_ref[j * per + gp, :, half * 128:(half + 1) * 128] = acc.astype(BF16)


def _s5_in(x, mod, norm_g, w, np_rows, seq_rows):
    n = x.shape[0]
    tm = ROW_TILE
    rowmap = _mod_row_map(np_rows // tm, seq_rows // tm)
    nck = tm // S5_CHUNK
    return pl.pallas_call(
        _s5_in_body,
        grid=(n // tm,),
        in_specs=[pl.BlockSpec((tm, D), lambda i: (i, 0)),
                  pl.BlockSpec((1, 1, 6 * D), lambda i: (rowmap(i), 0, 0)),
                  pl.BlockSpec((1, D), lambda i: (0, 0)),
                  pl.BlockSpec((D, D), lambda i: (0, 0))],
        out_specs=pl.BlockSpec((S5_GROUPS, nck, S5_CHUNK * S5_CH), lambda i: (0, i, 0)),
        out_shape=jax.ShapeDtypeStruct((S5_GROUPS, n // S5_CHUNK, S5_CHUNK * S5_CH), BF16),
        scratch_shapes=[pltpu.VMEM((D // 128, tm, 128), F32)],
        compiler_params=_cp("arbitrary"),
        name="s5_in",
    )(x, mod, norm_g, w)


def _s5_core_body(u_ref, m_ref, win_ref, cout_ref, pw_ref, *rest, kseq, has_init, emit_states):
    rest = list(rest)
    h0_ref = rest.pop(0) if has_init else None
    y_ref = rest.pop(0)
    sf_ref = rest.pop(0) if emit_states else None
    kb = u_ref.shape[1]
    row = lax.broadcasted_iota(jnp.int32, (kb, 128), 0)
    kk = row & (kseq - 1)

    def cmul(a1, a2, x):
        return a1 * x + a2 * pltpu.roll(x, 64, 1)

    for gi in range(u_ref.shape[0]):
        u = u_ref[gi]
        y = jnp.dot(u, m_ref[gi], preferred_element_type=F32)
        for d in range(2):
            s = jnp.dot(u, win_ref[gi, d], preferred_element_type=F32)
            edge = (kk == 0) if d == 0 else (kk == kseq - 1)
            if has_init:
                h0 = h0_ref[gi, d]
                s = s + jnp.where(edge, cmul(pw_ref[gi, d, 0, 0:1], pw_ref[gi, d, 0, 1:2], h0), 0.0)
            sw = pltpu.roll(s, 64, 1)
            step, lvl = 1, 0
            while step < kseq:
                shift = step if d == 0 else kb - step
                ok = (kk >= step) if d == 0 else (kk < kseq - step)
                a1, a2 = pw_ref[gi, d, lvl, 0:1], pw_ref[gi, d, lvl, 1:2]
                sh, swh = pltpu.roll(s, shift, 0), pltpu.roll(sw, shift, 0)
                s = s + jnp.where(ok, a1 * sh + a2 * swh, 0.0)
                sw = sw + jnp.where(ok, a1 * swh - a2 * sh, 0.0)
                step, lvl = step * 2, lvl + 1
            if emit_states:
                sf_ref[gi, d] = s
            hin = pltpu.roll(s, 1, 0) if d == 0 else pltpu.roll(s, kb - 1, 0)
            if has_init:
                hin = jnp.where(edge, h0, hin)
            else:
                hin = jnp.where(edge, 0.0, hin)
            y = y + jnp.dot(hin.astype(BF16), cout_ref[gi, d], preferred_element_type=F32)
        y_ref[gi] = _gelu(y).astype(BF16)


def _s5_core(ug, mats, row0, rows, kseq, h0=None, emit_states=False):
    m, win, cout, pw = mats
    kb = 128
    b0 = row0 // kb
    nlv = pw.shape[2]
    gb = S5_GROUP_BLOCK
    in_specs = [pl.BlockSpec((gb, kb, 256), lambda g, i: (g, b0 + i, 0)),
                pl.BlockSpec((gb, 256, 256), lambda g, i: (g, 0, 0)),
                pl.BlockSpec((gb, 2, 256, 128), lambda g, i: (g, 0, 0, 0)),
                pl.BlockSpec((gb, 2, 128, 256), lambda g, i: (g, 0, 0, 0)),
                pl.BlockSpec((gb, 2, nlv, 2, 128), lambda g, i: (g, 0, 0, 0, 0))]
    args = [ug, m, win, cout, pw]
    if h0 is not None:
        assert kseq == kb
        in_specs.append(pl.BlockSpec((gb, None, 2, 1, 128), lambda g, i: (g, i, 0, 0, 0)))
        args.append(h0)
    out_specs = [pl.BlockSpec((gb, kb, 256), lambda g, i: (g, i, 0))]
    out_shape = [jax.ShapeDtypeStruct((S5_GROUPS, rows, 256), BF16)]
    if emit_states:
        out_specs.append(pl.BlockSpec((gb, 2, kb, 128), lambda g, i: (g, 0, i, 0)))
        out_shape.append(jax.ShapeDtypeStruct((S5_GROUPS, 2, rows, 128), F32))
    return pl.pallas_call(
        functools.partial(_s5_core_body, kseq=kseq, has_init=h0 is not None, emit_states=emit_states),
        grid=(S5_GROUPS // gb, rows // kb),
        in_specs=in_specs,
        out_specs=out_specs,
        out_shape=out_shape,
        compiler_params=_cp("arbitrary", "arbitrary"),
        name="s5_core_ctx" if h0 is not None else "s5_core",
    )(*args)


def _s5_out_body(x_ref, mod_ref, yp_ref, ys_ref, w_ref, o_ref, y_scr, *, np_blocks):
    cs, per = S5_CHUNK, 128 // S5_CH
    nck = yp_ref.shape[1]
    lane = lax.broadcasted_iota(jnp.int32, (nck, 128), 1)
    ctx = pl.program_id(0) < np_blocks
    for j in range(D // 128):
        gs = slice(j * per, (j + 1) * per)
        yj = jnp.where(ctx, yp_ref[gs], ys_ref[gs])
        src = [[yj[gp, :, half * 128:(half + 1) * 128].astype(F32) for half in range(cs // per)]
               for gp in range(per)]
        for t in range(cs):
            half, t8 = divmod(t, per)
            acc = None
            for gp in range(per):
                piece = src[gp][half]
                shift = ((gp - t8) * S5_CH) % 128
                if shift:
                    piece = pltpu.roll(piece, shift, 1)
                acc = piece if acc is None else jnp.where(_window_of(lane, gp), piece, acc)
            y_scr[j, pl.ds(t, nck, stride=cs), :] = acc
    y = jnp.concatenate([y_scr[j] for j in range(D // 128)], axis=1).astype(BF16)
    zz = jnp.dot(y, w_ref[...], preferred_element_type=F32)
    za = zz[:, 0:D]
    zb = zz[:, D:2 * D]
    o_ref[...] = x_ref[...] + mod_ref[0][:, 2 * D:3 * D] * (za / (1.0 + jnp.exp(-zb)))


def _s5_out(x, mod, y_p, y_s, w, np_rows, seq_rows):
    n = x.shape[0]
    tm = ROW_TILE
    npb = np_rows // tm
    rowmap = _mod_row_map(npb, seq_rows // tm)
    sp, ss = _split_specs((S5_GROUPS, tm // S5_CHUNK, S5_CHUNK * S5_CH), npb, 1)
    return pl.pallas_call(
        functools.partial(_s5_out_body, np_blocks=npb),
        grid=(n // tm,),
        in_specs=[pl.BlockSpec((tm, D), lambda i: (i, 0)),
                  pl.BlockSpec((1, 1, 6 * D), lambda i: (rowmap(i), 0, 0)),
                  sp, ss,
                  pl.BlockSpec((D, 2 * D), lambda i: (0, 0))],
        out_specs=pl.BlockSpec((tm, D), lambda i: (i, 0)),
        out_shape=jax.ShapeDtypeStruct((n, D), F32),
        scratch_shapes=[pltpu.VMEM((D // 128, tm, 128), F32)],
        compiler_params=_cp("arbitrary"),
        name="s5_out",
    )(x, mod, y_p, y_s, w)


def _s5_matrices(a_re, a_im, log_dt, b_re, b_im, c_re, c_im, d_skip):
    cs = S5_CHUNK
    dt = jnp.exp(log_dt)[..., None]
    lr, li = a_re * dt, a_im * dt

    def lam_pow(tau):
        mag = jnp.exp(lr[:, :, None, :] * tau[:, None])
        ang = li[:, :, None, :] * tau[:, None]
        return mag * jnp.cos(ang), mag * jnp.sin(ang)

    l1r, l1i = jnp.exp(lr) * jnp.cos(li), jnp.exp(lr) * jnp.sin(li)
    den = a_re * a_re + a_im * a_im
    cr = ((l1r - 1.0) * a_re + l1i * a_im) / den
    ci = (l1i * a_re - (l1r - 1.0) * a_im) / den
    bt_re = b_re.transpose(0, 1, 3, 2)
    bt_im = b_im.transpose(0, 1, 3, 2)
    bbr = cr[:, :, None, :] * bt_re - ci[:, :, None, :] * bt_im
    bbi = cr[:, :, None, :] * bt_im + ci[:, :, None, :] * bt_re

    pr, pi = lam_pow(jnp.arange(cs + 1, dtype=F32))
    clr = c_re[:, :, None] * pr[:, :, :, None, :] - c_im[:, :, None] * pi[:, :, :, None, :]
    cli = c_re[:, :, None] * pi[:, :, :, None, :] + c_im[:, :, None] * pr[:, :, :, None, :]
    cl = jnp.concatenate([clr[:, :, :cs], -cli[:, :, :cs]], axis=-1).reshape(2, S5_GROUPS, cs * S5_CH, 2 * S5_P)
    bb = jnp.concatenate([bbr, bbi], axis=-1)
    kern = jnp.einsum('dgeq,dgjq->dgej', bb, cl, precision=HI)
    w = cs * S5_CH
    kern_b = kern[1].reshape(S5_GROUPS, S5_CH, cs, S5_CH)[:, :, ::-1].reshape(S5_GROUPS, S5_CH, w)
    zpad = lambda a, lo, hi: jnp.pad(a, ((0, 0), (0, 0), (lo, hi)))
    mf = jnp.stack([zpad(kern[0][:, :, :w - S5_CH * s], S5_CH * s, 0) for s in range(cs)], axis=1)
    mb = jnp.stack([zpad(kern_b[:, :, S5_CH * (cs - 1 - s):], 0, S5_CH * (cs - 1 - s)) for s in range(cs)], axis=1)
    m = (mf + mb).reshape(S5_GROUPS, w, w)
    m = m + jnp.eye(cs * S5_CH, dtype=F32) * jnp.tile(d_skip.reshape(S5_GROUPS, 1, S5_CH), (1, 1, cs))

    def win_dir(d, powers):
        qr_, qi_ = pr[d][:, powers][:, :, None, :], pi[d][:, powers][:, :, None, :]
        wr = qr_ * bbr[d][:, None] - qi_ * bbi[d][:, None]
        wi = qr_ * bbi[d][:, None] + qi_ * bbr[d][:, None]
        return jnp.concatenate([wr, wi], axis=-1).reshape(S5_GROUPS, cs * S5_CH, 2 * S5_P)

    win = jnp.stack([win_dir(0, jnp.arange(cs - 1, -1, -1)), win_dir(1, jnp.arange(cs))], axis=1)

    def cout_dir(d, powers):
        z = jnp.concatenate([clr[d][:, powers], -cli[d][:, powers]], axis=-1)
        return z.reshape(S5_GROUPS, cs * S5_CH, 2 * S5_P).transpose(0, 2, 1)

    cout = jnp.stack([cout_dir(0, jnp.arange(1, cs + 1)), cout_dir(1, jnp.arange(cs, 0, -1))], axis=1)

    qr, qi = lam_pow(cs * (2.0 ** jnp.arange(7, dtype=F32)))
    a1 = jnp.concatenate([qr, qr], axis=-1)
    a2 = jnp.concatenate([-qi, qi], axis=-1)
    pw = jnp.stack([a1, a2], axis=3).transpose(1, 0, 2, 3, 4)
    return m.astype(BF16), win.astype(BF16), cout.astype(BF16), pw


def _rope_tables(dec_seq):
    rows = dec_seq // GRID_W
    row_id = jnp.repeat(jnp.arange(rows), GRID_W).astype(F32)
    col_id = jnp.tile(jnp.arange(GRID_W), rows).astype(F32)
    quarter = DA_DIM // 4
    inv = ROPE_BASE ** (-jnp.arange(quarter, dtype=F32) / quarter)
    ang_r = row_id[:, None] * inv
    ang_c = col_id[:, None] * inv
    ang = jnp.concatenate([ang_r, ang_r, ang_c, ang_c], axis=-1)
    ang = jnp.tile(ang, (1, 2 * DA_HEADS))
    pad = jnp.zeros((ROW_TILE, ang.shape[1]), F32)
    ang = jnp.concatenate([pad, ang], axis=0)
    return jnp.cos(ang), jnp.sin(ang)


def kernel(x_prompt, x_sample, cache_da_k, cache_da_v, state_gla, state_s5, c, c_ctx,
           ada_w, ada_b, norm_g, final_norm_g, mix0_w_in, mix0_w_out, da_lam, da_subln_g,
           gla_gate_up, gla_gate_bias, gla_norm_g, s5_w_in, s5_a_re, s5_a_im, s5_log_dt,
           s5_b_re, s5_b_im, s5_c_re, s5_c_im, s5_d, s5_w_out,
           peer_w_query, peer_sub_keys, peer_u, peer_v):
    bp, lp, _ = x_prompt.shape
    bs, ls, _ = x_sample.shape
    npr, nsr = bp * lp, bs * ls
    assert npr % ls == 0 and ls % ROW_TILE == 0 and lp % ROW_TILE == 0
    x_p, x_s = x_prompt.reshape(npr, D), x_sample.reshape(nsr, D)

    cond = jnp.concatenate([c_ctx[None], c, jnp.zeros((15 - bs, D), F32)], axis=0)
    mod_all = _ada_mod(cond, ada_w, ada_b)
    mods = [mod_all[l].reshape(16, 1, 6 * D) for l in range(mod_all.shape[0])]
    fg = final_norm_g.reshape(1, D)

    lam_init = 0.8 - 0.6 * math.exp(-0.3 * 0)
    cos_t, sin_t = _rope_tables(ls)
    w_in = jnp.pad(mix0_w_in[0], ((0, 0), (0, IN0_PAD - mix0_w_in.shape[2]))).astype(BF16)
    q, kb, vb, k32, v32, gqk, gv, gr, low = _inproj0(
        x_p, x_s, mods[0], norm_g[0, 0].reshape(1, D), w_in, cos_t, sin_t, npr, ls)
    sub_g = da_subln_g[0].reshape(1, 2 * DA_DIM)
    past = cache_da_k.shape[2]
    oda_p = _diff_attention(q, kb, vb, da_lam[0], sub_g, lam_init, 0, bp, lp)
    oda_s = _diff_attention(q, kb, vb, da_lam[0], sub_g, lam_init, npr, bs, ls,
                            cache=(cache_da_k.reshape(bs, -1, past, 512), cache_da_v.reshape(bs, -1, past, 512), 0))
    hk = GLA_HEADS * GLA_DK
    gup = jnp.zeros((2, 128, hk), F32)
    for d in range(2):
        gup = gup.at[d, d * GLA_RANK:(d + 1) * GLA_RANK].set(gla_gate_up[0, d])
    gbias = gla_gate_bias[0].reshape(2, 1, hk)
    ng = gla_norm_g[0].reshape(1, GLA_DV)
    s0 = state_gla[:, 0].transpose(0, 1, 4, 2, 3).reshape(bs, 2, GLA_DV, hk)
    og_p, st_p = _gla(gqk, gv, gr, low, gup, gbias, ng, 0, bp, lp)
    og_s, _ = _gla(gqk, gv, gr, low, gup, gbias, ng, npr, bs, ls, s0=s0)
    x = _outproj0(x_p, x_s, mods[0], oda_p, oda_s, og_p, og_s, mix0_w_out[0].astype(BF16), npr, ls)
    tables = _peer_tables(peer_w_query, peer_sub_keys, peer_u, peer_v)
    x = _peer(x, mods[0], norm_g[0, 1].reshape(1, D), tables, 0, fg, npr, ls, final_norm=False)

    ug = _s5_in(x, mods[1], norm_g[1, 0].reshape(1, D), s5_w_in[0].astype(BF16), npr, ls)
    cs = S5_CHUNK
    mats = _s5_matrices(s5_a_re[0], s5_a_im[0], s5_log_dt[0], s5_b_re[0], s5_b_im[0],
                        s5_c_re[0], s5_c_im[0], s5_d[0])
    h0 = state_s5[:, 0]
    h0 = h0.transpose(2, 0, 1, 4, 3).reshape(S5_GROUPS, bs, 2, 1, 2 * S5_P)
    y_p, sf = _s5_core(ug, mats, 0, npr // cs, lp // cs, emit_states=True)
    (y_s,) = _s5_core(ug, mats, npr // cs, nsr // cs, ls // cs, h0=h0)
    x = _s5_out(x, mods[1], y_p, y_s, s5_w_out[0].astype(BF16), npr, ls)
    y_p, y_s = _peer(x, mods[1], norm_g[1, 1].reshape(1, D), tables, 1, fg, npr, ls, final_norm=True, split=True)

    y_prompt = y_p.reshape(bp, lp, D)
    y_sample = y_s.reshape(bs, ls, D)
    new_k = k32[:npr].reshape(bp, 1, lp, 2 * DA_HEADS, DA_DIM)
    new_v = v32[:npr].reshape(bp, 1, lp, DA_HEADS, 2 * DA_DIM)
    new_gla = st_p.reshape(bp, 2, GLA_DV, GLA_HEADS, GLA_DK).transpose(0, 1, 3, 4, 2)[:, None]
    kc = lp // cs
    fin = jnp.stack([sf[:, 0, kc - 1::kc], sf[:, 1, 0::kc]], axis=1)
    new_s5 = fin.reshape(S5_GROUPS, 2, bp, 2, S5_P).transpose(2, 1, 0, 4, 3)[:, None]
    return (y_prompt, y_sample, new_k, new_v, new_gla, new_s5)
```
